```python
import jax, jax.numpy as jnp
from jax import lax
import numpy as np

D_MODEL = 2048
BATCH = 1
SEQ = 16384
DEPTH = 4
DEC_BATCH = 32
DEC_SEQ = 64
PAST_LEN = 4096

CHUNK = 64
N_MIXERS = 3
N_A = (DEPTH + 2) // 3
N_B = (DEPTH + 1) // 3
N_C = DEPTH // 3
NORM_EPS = 1e-5
D_FF = 4 * D_MODEL

ATTN_HEADS = 32
ATTN_KV_HEADS = 4
ATTN_GROUP = ATTN_HEADS // ATTN_KV_HEADS
HEAD_DIM = D_MODEL // ATTN_HEADS
WINDOW = 128
WINDOW_CHUNKS = WINDOW // CHUNK
ROT_DIM = HEAD_DIM // 4
ROPE_THETA = 500000.0
QKV_DIM = (ATTN_HEADS + 2 * ATTN_KV_HEADS) * HEAD_DIM

RET_HEADS = 8
RET_DK = D_MODEL // RET_HEADS
RET_DV = 2 * RET_DK
RET_THETA = 10000.0
RET_GN_EPS = 1e-5
RET_IN_DIM = 2 * RET_HEADS * RET_DK + 2 * RET_HEADS * RET_DV

RWKV_HS = 64
RWKV_HEADS = D_MODEL // RWKV_HS
DECAY_LORA = 96
AAA_LORA = 96
GATE_LORA = 256
RWKV_GN_EPS = 64e-5

kernel_name = 'hybrid_swa_retnet_rwkv7_stream_step'

f32 = jnp.float32


def rmsnorm(x, g):
    xf = x.astype(f32)
    y = xf * lax.rsqrt(jnp.mean(xf * xf, axis=-1, keepdims=True) + NORM_EPS)
    return (y * g).astype(x.dtype)


def rope(x, pos, rot_dim, theta):
    half = rot_dim // 2
    inv = theta ** (-jnp.arange(half, dtype=f32) / half)
    ang = pos.astype(f32)[:, None] * inv[None, :]
    cos = jnp.cos(ang)[:, None, :]
    sin = jnp.sin(ang)[:, None, :]
    xf = x.astype(f32)
    x1, x2 = xf[..., :half], xf[..., half:rot_dim]
    out = jnp.concatenate([x1 * cos - x2 * sin, x2 * cos + x1 * sin, xf[..., rot_dim:]], axis=-1)
    return out.astype(x.dtype)


def sq_relu_mlp(h, w_up, w_down):
    return jnp.square(jax.nn.relu(h @ w_up)) @ w_down


def attn_qkv(h, w_qkv, pos):
    B, T, _ = h.shape
    qkv = h @ w_qkv
    qd, kd = ATTN_HEADS * HEAD_DIM, ATTN_KV_HEADS * HEAD_DIM
    q = qkv[..., :qd].reshape(B, T, ATTN_HEADS, HEAD_DIM)
    k = qkv[..., qd:qd + kd].reshape(B, T, ATTN_KV_HEADS, HEAD_DIM)
    v = qkv[..., qd + kd:].reshape(B, T, ATTN_KV_HEADS, HEAD_DIM)
    return rope(q, pos, ROT_DIM, ROPE_THETA), rope(k, pos, ROT_DIM, ROPE_THETA), v


def sink_attend(q, k, v, sinks, mask):
    s = jnp.einsum('bnqkgd,bnskd->bnkgqs', q, k).astype(f32) * (HEAD_DIM ** -0.5)
    if mask is not None:
        s = jnp.where(mask, s, -jnp.inf)
    sink = jnp.broadcast_to(sinks.astype(f32).reshape(ATTN_KV_HEADS, ATTN_GROUP, 1, 1), s.shape[:-1] + (1,))
    p = jax.nn.softmax(jnp.concatenate([s, sink], axis=-1), axis=-1)[..., :-1]
    return jnp.einsum('bnkgqs,bnskd->bnqkgd', p.astype(v.dtype), v)


def attn_prompt(h, w_qkv, sinks, w_o):
    B, T, _ = h.shape
    nc = T // CHUNK
    q, k, v = attn_qkv(h, w_qkv, jnp.arange(T))
    qc = q.reshape(B, nc, CHUNK, ATTN_KV_HEADS, ATTN_GROUP, HEAD_DIM)

    def band(x):
        xp = jnp.concatenate([jnp.zeros((B, WINDOW, ATTN_KV_HEADS, HEAD_DIM), x.dtype), x], axis=1)
        xp = xp.reshape(B, nc + WINDOW_CHUNKS, CHUNK, ATTN_KV_HEADS, HEAD_DIM)
        return jnp.concatenate([xp[:, j:j + nc] for j in range(WINDOW_CHUNKS + 1)], axis=2)

    kb, vb = band(k), band(v)
    key_chunk = jnp.arange(nc)[:, None] + jnp.arange((WINDOW_CHUNKS + 1) * CHUNK)[None, :] // CHUNK - WINDOW_CHUNKS
    mask = (key_chunk >= 0)[None, :, None, None, None, :]
    o = sink_attend(qc, kb, vb, sinks, mask).reshape(B, T, ATTN_HEADS * HEAD_DIM)
    return o @ w_o, k[:, -WINDOW:], v[:, -WINDOW:]


def attn_sample(h, cache_k, cache_v, w_qkv, sinks, w_o):
    B, T, _ = h.shape
    q, k, v = attn_qkv(h, w_qkv, PAST_LEN + jnp.arange(T))
    k_all = jnp.concatenate([cache_k.astype(k.dtype), k], axis=1)
    v_all = jnp.concatenate([cache_v.astype(v.dtype), v], axis=1)
    qc = q.reshape(B, 1, T, ATTN_KV_HEADS, ATTN_GROUP, HEAD_DIM)
    o = sink_attend(qc, k_all[:, None], v_all[:, None], sinks, None).reshape(B, T, ATTN_HEADS * HEAD_DIM)
    return o @ w_o, k_all[:, -WINDOW:], v_all[:, -WINDOW:]


def ret_project(h, w_in, pos):
    B, T, _ = h.shape
    proj = h @ w_in
    qd, vd = RET_HEADS * RET_DK, RET_HEADS * RET_DV
    q = proj[..., :qd].reshape(B, T, RET_HEADS, RET_DK)
    k = proj[..., qd:2 * qd].reshape(B, T, RET_HEADS, RET_DK)
    v = proj[..., 2 * qd:2 * qd + vd].reshape(B, T, RET_HEADS, RET_DV)
    g = proj[..., 2 * qd + vd:]
    q = rope(q, pos, RET_DK, RET_THETA)
    k = rope(k, pos, RET_DK, RET_THETA) * (RET_DK ** -0.5)
    return q, k, v, g


def retention_chunk(S, q, k, v):
    L = q.shape[1]
    lg = jnp.log1p(-jnp.exp2(-5.0 - jnp.arange(RET_HEADS, dtype=f32)))
    idx = jnp.arange(L, dtype=f32)
    diff = idx[:, None] - idx[None, :]
    decay = jnp.where(diff >= 0, jnp.exp(lg[:, None, None] * jnp.maximum(diff, 0.0)[None]), 0.0)
    qf, kf, vf = q.astype(f32), k.astype(f32), v.astype(f32)
    scores = jnp.einsum('blhd,bmhd->bhlm', qf, kf) * decay[None]
    xi = jnp.exp(lg[None, :] * (idx + 1.0)[:, None])
    zeta = jnp.exp(lg[None, :] * (L - 1.0 - idx)[:, None])
    o = jnp.einsum('bhlm,bmhe->blhe', scores, vf) + jnp.einsum('blhd,bhde->blhe', qf * xi[None, :, :, None], S)
    S_new = jnp.exp(lg * L)[None, :, None, None] * S + jnp.einsum('blhd,blhe->bhde', kf * zeta[None, :, :, None], vf)
    return S_new, o


def ret_output(o, g, gn_w, w_o):
    B, T = o.shape[:2]
    mu = jnp.mean(o, axis=-1, keepdims=True)
    var = jnp.mean(jnp.square(o - mu), axis=-1, keepdims=True)
    on = ((o - mu) * lax.rsqrt(var + RET_GN_EPS)).reshape(B, T, RET_HEADS * RET_DV) * gn_w
    return (jax.nn.silu(g.astype(f32)) * on).astype(g.dtype) @ w_o


def retention_prompt(h, w_in, gn_w, w_o):
    B, T, _ = h.shape
    nc = T // CHUNK
    q, k, v, g = ret_project(h, w_in, jnp.arange(T))
    to_chunks = lambda x: jnp.moveaxis(x.reshape((B, nc, CHUNK) + x.shape[2:]), 1, 0)
    S0 = jnp.zeros((B, RET_HEADS, RET_DK, RET_DV), f32)
    S, o = lax.scan(lambda s, qkv: retention_chunk(s, *qkv), S0, (to_chunks(q), to_chunks(k), to_chunks(v)))
    o = jnp.moveaxis(o, 0, 1).reshape(B, T, RET_HEADS, RET_DV)
    return ret_output(o, g, gn_w, w_o), S.astype(h.dtype)


def retention_sample(h, S_prev, w_in, gn_w, w_o):
    T = h.shape[1]
    q, k, v, g = ret_project(h, w_in, PAST_LEN + jnp.arange(T))
    S, o = retention_chunk(S_prev.astype(f32), q, k, v)
    return ret_output(o, g, gn_w, w_o), S.astype(S_prev.dtype)


def rwkv_mix(h, shift_prev, S0, mu, w_rkv, w_o, w0, w1, w2, a0, a1, a2, g1, g2, k_k, k_a, r_k, ln_w, ln_b):
    B, T, D = h.shape
    x_prev = jnp.concatenate([shift_prev[:, None].astype(h.dtype), h[:, :-1]], axis=1)
    xx = x_prev - h
    lerp = lambda i: h + xx * mu[i]
    r = lerp(0) @ w_rkv[0]
    k = lerp(2) @ w_rkv[1]
    v = lerp(3) @ w_rkv[2]
    w_log = -jax.nn.softplus(-(w0 + jnp.tanh(lerp(1) @ w1) @ w2).astype(f32)) - 0.5
    a = jax.nn.sigmoid((a0 + (lerp(4) @ a1) @ a2).astype(f32))
    g = jax.nn.sigmoid(lerp(5) @ g1) @ g2
    heads = lambda x: x.astype(f32).reshape(B, T, RWKV_HEADS, RWKV_HS)
    per_head = lambda p: p.astype(f32).reshape(RWKV_HEADS, RWKV_HS)
    r, k, v, a = heads(r), heads(k), heads(v), heads(a)
    decay = jnp.exp(-jnp.exp(heads(w_log)))
    kk = k * per_head(k_k)
    kk = kk / jnp.maximum(jnp.sqrt(jnp.sum(kk * kk, axis=-1, keepdims=True)), 1e-12)
    k = k * (1.0 + (a - 1.0) * per_head(k_a))

    def step(S, inp):
        r_t, k_t, v_t, w_t, kk_t, a_t = inp
        S = (S * w_t[:, :, None, :]
             + jnp.einsum('bhvk,bhk->bhv', S, -kk_t)[..., None] * (kk_t * a_t)[:, :, None, :]
             + v_t[..., None] * k_t[:, :, None, :])
        return S, jnp.einsum('bhvk,bhk->bhv', S, r_t)

    xs = tuple(jnp.moveaxis(t, 1, 0) for t in (r, k, v, decay, kk, a))
    S, o = lax.scan(step, S0.astype(f32), xs)
    o = jnp.moveaxis(o, 0, 1)
    mu_o = jnp.mean(o, axis=-1, keepdims=True)
    var_o = jnp.mean(jnp.square(o - mu_o), axis=-1, keepdims=True)
    on = ((o - mu_o) * lax.rsqrt(var_o + RWKV_GN_EPS)).reshape(B, T, D) * ln_w + ln_b
    bonus = (jnp.sum(r * k * r_k.astype(f32), axis=-1, keepdims=True) * v).reshape(B, T, D)
    out = ((on + bonus) * g.astype(f32)).astype(h.dtype) @ w_o
    return out, h[:, -1], S.astype(S0.dtype)


def setup_inputs(seed: int = 0) -> dict:
    key = jax.random.key(seed)
    ks = iter(jax.random.split(key, 48))

    def nrm(shape, scale):
        return jax.random.normal(next(ks), shape, f32) * scale

    def gain(shape):
        return 1.0 + nrm(shape, 0.02)

    D = D_MODEL
    return {
        'x_prompt': nrm((BATCH, SEQ, D), 1.0),
        'x_sample': nrm((DEC_BATCH, DEC_SEQ, D), 1.0),
        'cache_attn_k': nrm((N_A, DEC_BATCH, WINDOW, ATTN_KV_HEADS, HEAD_DIM), 1.0),
        'cache_attn_v': nrm((N_A, DEC_BATCH, WINDOW, ATTN_KV_HEADS, HEAD_DIM), 1.0),
        'state_ret': nrm((N_B, DEC_BATCH, RET_HEADS, RET_DK, RET_DV), 1.0),
        'state_rwkv': nrm((N_C, DEC_BATCH, RWKV_HEADS, RWKV_HS, RWKV_HS), 0.2),
        'state_rwkv_shift': nrm((N_C, DEC_BATCH, D), 1.0),
        'norm_mix': gain((DEPTH, D)),
        'norm_mlp': gain((DEPTH, D)),
        'norm_final': gain((D,)),
        'attn_w_qkv': nrm((N_A, D, QKV_DIM), D ** -0.5),
        'attn_sinks': nrm((N_A, ATTN_HEADS), 1.0),
        'attn_w_o': nrm((N_A, ATTN_HEADS * HEAD_DIM, D), (ATTN_HEADS * HEAD_DIM) ** -0.5),
        'ret_w_in': nrm((N_B, D, RET_IN_DIM), D ** -0.5),
        'ret_gn_w': gain((N_B, RET_HEADS * RET_DV)),
        'ret_w_o': nrm((N_B, RET_HEADS * RET_DV, D), (RET_HEADS * RET_DV) ** -0.5),
        'rwkv_mu': jax.random.uniform(next(ks), (N_C, 6, D), f32),
        'rwkv_w_rkv': nrm((N_C, 3, D, D), D ** -0.5),
        'rwkv_w_o': nrm((N_C, D, D), D ** -0.5),
        'rwkv_w0': -2.0 + nrm((N_C, D), 0.5),
        'rwkv_w1': nrm((N_C, D, DECAY_LORA), D ** -0.5),
        'rwkv_w2': nrm((N_C, DECAY_LORA, D), 0.5 * DECAY_LORA ** -0.5),
        'rwkv_a0': nrm((N_C, D), 0.1),
        'rwkv_a1': nrm((N_C, D, AAA_LORA), D ** -0.5),
        'rwkv_a2': nrm((N_C, AAA_LORA, D), 0.5 * AAA_LORA ** -0.5),
        'rwkv_g1': nrm((N_C, D, GATE_LORA), D ** -0.5),
        'rwkv_g2': nrm((N_C, GATE_LORA, D), GATE_LORA ** -0.5),
        'rwkv_k_k': 0.85 + nrm((N_C, D), 0.05),
        'rwkv_k_a': 1.0 + nrm((N_C, D), 0.05),
        'rwkv_r_k': nrm((N_C, RWKV_HEADS, RWKV_HS), 0.1),
        'rwkv_ln_w': gain((N_C, D)),
        'rwkv_ln_b': nrm((N_C, D), 0.02),
        'mlp_w_up': nrm((DEPTH, D, D_FF), D ** -0.5),
        'mlp_w_down': nrm((DEPTH, D_FF, D), D_FF ** -0.5),
    }


def reference(x_prompt, x_sample, cache_attn_k, cache_attn_v, state_ret, state_rwkv, state_rwkv_shift,
              norm_mix, norm_mlp, norm_final,
              attn_w_qkv, attn_sinks, attn_w_o,
              ret_w_in, ret_gn_w, ret_w_o,
              rwkv_mu, rwkv_w_rkv, rwkv_w_o, rwkv_w0, rwkv_w1, rwkv_w2, rwkv_a0, rwkv_a1, rwkv_a2,
              rwkv_g1, rwkv_g2, rwkv_k_k, rwkv_k_a, rwkv_r_k, rwkv_ln_w, rwkv_ln_b,
              mlp_w_up, mlp_w_down):
    xp, xs = x_prompt, x_sample
    kp_l, vp_l, ks_l, vs_l = [], [], [], []
    rp_l, rs_l = [], []
    wp_l, ws_l, shp_l, shs_l = [], [], [], []
    for i in range(DEPTH):
        hp, hs = rmsnorm(xp, norm_mix[i]), rmsnorm(xs, norm_mix[i])
        j = i // N_MIXERS
        kind = i % N_MIXERS
        if kind == 0:
            mp, kp, vp = attn_prompt(hp, attn_w_qkv[j], attn_sinks[j], attn_w_o[j])
            ms, ks_, vs_ = attn_sample(hs, cache_attn_k[j], cache_attn_v[j], attn_w_qkv[j], attn_sinks[j], attn_w_o[j])
            kp_l.append(kp); vp_l.append(vp); ks_l.append(ks_); vs_l.append(vs_)
        elif kind == 1:
            mp, Sp = retention_prompt(hp, ret_w_in[j], ret_gn_w[j], ret_w_o[j])
            ms, Ss = retention_sample(hs, state_ret[j], ret_w_in[j], ret_gn_w[j], ret_w_o[j])
            rp_l.append(Sp); rs_l.append(Ss)
        else:
            prm = (rwkv_mu[j], rwkv_w_rkv[j], rwkv_w_o[j], rwkv_w0[j], rwkv_w1[j], rwkv_w2[j],
                   rwkv_a0[j], rwkv_a1[j], rwkv_a2[j], rwkv_g1[j], rwkv_g2[j],
                   rwkv_k_k[j], rwkv_k_a[j], rwkv_r_k[j], rwkv_ln_w[j], rwkv_ln_b[j])
            Bp = hp.shape[0]
            mp, shp, Sp = rwkv_mix(hp, jnp.zeros((Bp, D_MODEL), hp.dtype),
                                   jnp.zeros((Bp, RWKV_HEADS, RWKV_HS, RWKV_HS), state_rwkv.dtype), *prm)
            ms, shs, Ss = rwkv_mix(hs, state_rwkv_shift[j], state_rwkv[j], *prm)
            wp_l.append(Sp); ws_l.append(Ss); shp_l.append(shp); shs_l.append(shs)
        xp = xp + mp
        xs = xs + ms
        xp = xp + sq_relu_mlp(rmsnorm(xp, norm_mlp[i]), mlp_w_up[i], mlp_w_down[i])
        xs = xs + sq_relu_mlp(rmsnorm(xs, norm_mlp[i]), mlp_w_up[i], mlp_w_down[i])
    y_prompt = rmsnorm(xp, norm_final)
    y_sample = rmsnorm(xs, norm_final)
    return (y_prompt, y_sample,
            jnp.stack(kp_l), jnp.stack(vp_l), jnp.stack(ks_l), jnp.stack(vs_l),
            jnp.stack(rp_l), jnp.stack(rs_l),
            jnp.stack(wp_l), jnp.stack(ws_l), jnp.stack(shp_l), jnp.stack(shs_l))
```

```python
import functools

import jax
import jax.numpy as jnp
from jax import lax
from jax.experimental import pallas as pl
from jax.experimental.pallas import tpu as pltpu

f32 = jnp.float32
bf16 = jnp.bfloat16

CHUNK = 64
NORM_EPS = 1e-5
PAST_LEN = 4096

ATTN_HEADS = 32
ATTN_KV_HEADS = 4
ATTN_GROUP = ATTN_HEADS // ATTN_KV_HEADS
HEAD_DIM = 64
WINDOW = 128
ROT_DIM = HEAD_DIM // 4
ROPE_THETA = 500000.0

RET_HEADS = 8
RET_DK = 256
RET_DV = 512
RET_THETA = 10000.0
RET_GN_EPS = 1e-5

RWKV_HS = 64
RWKV_GN_EPS = 64e-5
LORA_PAD = 512

VMEM_LIMIT = 52 * 1024 * 1024

NT_DIMS = (((1,), (1,)), ((), ()))


def _pick(n, cands):
    for c in cands:
        if n % c == 0:
            return c
    raise ValueError(f"no tile for {n} in {cands}")


def _params(sem):
    return pltpu.CompilerParams(dimension_semantics=sem, vmem_limit_bytes=VMEM_LIMIT)


def _rms(x, g):
    return x * lax.rsqrt(jnp.mean(x * x, axis=-1, keepdims=True) + NORM_EPS) * g


def _epi_attn(blk, mode, tabs):
    if mode == "plain":
        return blk
    c_ref, s1_ref, s2_ref = tabs
    out = (blk * c_ref[...] + pltpu.roll(blk, 8, 1) * s1_ref[...]
           + pltpu.roll(blk, 256 - 8, 1) * s2_ref[...])
    return out * (HEAD_DIM ** -0.5) if mode == "q" else out


def _epi_ret(blk, mode, tabs):
    if mode == "plain":
        return blk
    cos_ref, sin_ref = tabs
    c, s = cos_ref[...], sin_ref[...]
    x1, x2 = blk[:, :128], blk[:, 128:]
    out = jnp.concatenate([x1 * c - x2 * s, x2 * c + x1 * s], axis=1)
    return out * (RET_DK ** -0.5) if mode == "k" else out


def _normmm_kernel(*refs, modes, ntab, epi):
    x_ref, g_ref, w_ref = refs[:3]
    tabs = refs[3:3 + ntab]
    o_ref, xn_ref = refs[3 + ntab], refs[4 + ntab]
    j = pl.program_id(1)

    @pl.when(j == 0)
    def _():
        xn_ref[...] = _rms(x_ref[...], g_ref[...]).astype(bf16)

    acc = jnp.dot(xn_ref[...], w_ref[...], preferred_element_type=f32)
    groups = {}
    for jj, tile_modes in enumerate(modes):
        groups.setdefault(tuple(tile_modes), []).append(jj)
    for tile_modes, jjs in groups.items():
        cond = functools.reduce(jnp.logical_or, [j == jj for jj in jjs])

        @pl.when(cond)
        def _(tile_modes=tile_modes):
            for s, mode in enumerate(tile_modes):
                blk = acc[:, s * 256:(s + 1) * 256]
                o_ref[:, s * 256:(s + 1) * 256] = epi(blk, mode, tabs).astype(o_ref.dtype)


def _norm_matmul(x, g, w, tabs, modes, epi, out_dtype, tm, name):
    T, D = x.shape
    N = w.shape[1]
    ntile = len(modes)
    tn = N // ntile
    assert tn * ntile == N and tn == 256 * len(modes[0])
    tab_specs = [pl.BlockSpec((tm, t.shape[1]), lambda i, j: (i, 0)) for t in tabs]
    return pl.pallas_call(
        functools.partial(_normmm_kernel, modes=modes, ntab=len(tabs), epi=epi),
        out_shape=jax.ShapeDtypeStruct((T, N), out_dtype),
        grid=(T // tm, ntile),
        in_specs=[pl.BlockSpec((tm, D), lambda i, j: (i, 0)),
                  pl.BlockSpec((1, D), lambda i, j: (0, 0)),
                  pl.BlockSpec((D, tn), lambda i, j: (0, j))] + tab_specs,
        out_specs=pl.BlockSpec((tm, tn), lambda i, j: (i, j)),
        scratch_shapes=[pltpu.VMEM((tm, D), bf16)],
        compiler_params=_params(("parallel", "arbitrary")),
        name=name,
    )(x, g, w, *tabs)


def _mmres_kernel(a_ref, w_ref, r_ref, o_ref):
    o_ref[...] = r_ref[...] + jnp.dot(a_ref[...], w_ref[...], preferred_element_type=f32)


def _matmul_residual(a, w, res, name):
    T, K = a.shape
    N = w.shape[1]
    tm = _pick(T, (512, 256, 128, 64))
    tn = _pick(N, (1024, 512, 256))
    return pl.pallas_call(
        _mmres_kernel,
        out_shape=jax.ShapeDtypeStruct((T, N), f32),
        grid=(T // tm, N // tn),
        in_specs=[pl.BlockSpec((tm, K), lambda i, j: (i, 0)),
                  pl.BlockSpec((K, tn), lambda i, j: (0, j)),
                  pl.BlockSpec((tm, tn), lambda i, j: (i, j))],
        out_specs=pl.BlockSpec((tm, tn), lambda i, j: (i, j)),
        compiler_params=_params(("parallel", "parallel")),
        name=name,
    )(a, w, res)


def _mlp_kernel(x_ref, g_ref, wu_ref, wd_ref, gf_ref, o_ref, xn_ref, *, nk, final):
    k = pl.program_id(1)

    @pl.when(k == 0)
    def _():
        x = x_ref[...]
        xn_ref[...] = _rms(x, g_ref[...]).astype(bf16)
        o_ref[...] = x

    h = jnp.dot(xn_ref[...], wu_ref[...], preferred_element_type=f32)
    h = jnp.square(jnp.maximum(h, 0.0)).astype(bf16)
    o_ref[...] += jnp.dot(h, wd_ref[...], preferred_element_type=f32)

    if final:
        @pl.when(k == nk - 1)
        def _():
            o_ref[...] = _rms(o_ref[...], gf_ref[...])


def _mlp(x, g, w_up, w_down, g_final, final, name):
    T, D = x.shape
    F = w_up.shape[1]
    tm = _pick(T, (512, 256, 128, 64))
    tf = _pick(F, (1024, 512))
    nk = F // tf
    return pl.pallas_call(
        functools.partial(_mlp_kernel, nk=nk, final=final),
        out_shape=jax.ShapeDtypeStruct((T, D), f32),
        grid=(T // tm, nk),
        in_specs=[pl.BlockSpec((tm, D), lambda i, k: (i, 0)),
                  pl.BlockSpec((1, D), lambda i, k: (0, 0)),
                  pl.BlockSpec((D, tf), lambda i, k: (0, k)),
                  pl.BlockSpec((tf, D), lambda i, k: (k, 0)),
                  pl.BlockSpec((1, D), lambda i, k: (0, 0))],
        out_specs=pl.BlockSpec((tm, D), lambda i, k: (i, 0)),
        scratch_shapes=[pltpu.VMEM((tm, D), bf16)],
        compiler_params=_params(("parallel", "arbitrary")),
        name=name,
    )(x, g, w_up, w_down, g_final)


def _attn_kernel(sink_ref, q_ref, k0_ref, k1_ref, k2_ref, v0_ref, v1_ref, v2_ref, o_ref, *, npc):
    c = pl.program_id(0)
    nkeys = 3 * CHUNK
    key_blk = lax.broadcasted_iota(jnp.int32, (1, nkeys), 1) // CHUNK
    valid = jnp.logical_or(c >= npc, c + key_blk >= 2)
    bias = jnp.where(valid, 0.0, -jnp.inf).astype(f32)
    kcat = jnp.concatenate([k0_ref[...], k1_ref[...], k2_ref[...]], axis=0)
    vcat = jnp.concatenate([v0_ref[...], v1_ref[...], v2_ref[...]], axis=0)
    for kv in range(ATTN_KV_HEADS):
        kg = kcat[:, kv * HEAD_DIM:(kv + 1) * HEAD_DIM].astype(bf16)
        vg = vcat[:, kv * HEAD_DIM:(kv + 1) * HEAD_DIM].astype(bf16)
        heads = [kv * ATTN_GROUP + i for i in range(ATTN_GROUP)]
        qg = jnp.concatenate([q_ref[:, h * HEAD_DIM:(h + 1) * HEAD_DIM] for h in heads],
                             axis=0).astype(bf16)
        s = lax.dot_general(qg, kg, NT_DIMS, preferred_element_type=f32) + bias
        sk = jnp.concatenate([jnp.full((CHUNK, 1), sink_ref[h], f32) for h in heads], axis=0)
        m = jnp.maximum(jnp.max(s, axis=-1, keepdims=True), sk)
        p = jnp.exp(s - m)
        denom = jnp.sum(p, axis=-1, keepdims=True) + jnp.exp(sk - m)
        o = jnp.dot(p.astype(bf16), vg, preferred_element_type=f32) / denom
        for i, h in enumerate(heads):
            o_ref[:, h * HEAD_DIM:(h + 1) * HEAD_DIM] = o[i * CHUNK:(i + 1) * CHUNK].astype(o_ref.dtype)


def _attention(qkv, kext, vext, sinks, npc, nchunks):
    T = qkv.shape[0]
    qd = ATTN_HEADS * HEAD_DIM
    kvd = ATTN_KV_HEADS * HEAD_DIM

    def kv_spec(j):
        def imap(c):
            return (jnp.where(c < npc, c, npc + 2 + 3 * (c - npc)) + j, 0)
        return pl.BlockSpec((CHUNK, kvd), imap)

    return pl.pallas_call(
        functools.partial(_attn_kernel, npc=npc),
        out_shape=jax.ShapeDtypeStruct((T, qd), bf16),
        grid=(nchunks,),
        in_specs=[pl.BlockSpec(memory_space=pltpu.SMEM),
                  pl.BlockSpec((CHUNK, qd), lambda c: (c, 0))]
                 + [kv_spec(j) for j in range(3)] + [kv_spec(j) for j in range(3)],
        out_specs=pl.BlockSpec((CHUNK, qd), lambda c: (c, 0)),
        compiler_params=_params(("parallel",)),
        name="attn_core",
    )(sinks, qkv, kext, kext, kext, vext, vext, vext)


def _ret_kernel(lg_ref, q_ref, k_ref, v_ref, g_ref, s0_ref, gnw_ref, y_ref, sout_ref, S_ref, *, npc):
    h = pl.program_id(0)
    c = pl.program_id(1)
    L = CHUNK
    lg = lg_ref[h]

    @pl.when(c == 0)
    def _():
        S_ref[...] = jnp.zeros_like(S_ref)

    @pl.when(c >= npc)
    def _():
        S_ref[...] = s0_ref[0, 0]

    q = q_ref[...]
    k = k_ref[...]
    v = v_ref[...]
    row = lax.broadcasted_iota(jnp.int32, (L, L), 0)
    col = lax.broadcasted_iota(jnp.int32, (L, L), 1)
    diff = (row - col).astype(f32)
    decay = jnp.where(diff >= 0, jnp.exp(lg * jnp.maximum(diff, 0.0)), 0.0)
    idx = lax.broadcasted_iota(jnp.int32, (L, 1), 0).astype(f32)
    xi = jnp.exp(lg * (idx + 1.0))
    zeta = jnp.exp(lg * (L - 1.0 - idx))

    S = S_ref[...]
    scores = lax.dot_general(q, k, NT_DIMS, preferred_element_type=f32) * decay
    o = (jnp.dot(scores.astype(bf16), v, preferred_element_type=f32)
         + jnp.dot((q.astype(f32) * xi).astype(bf16), S.astype(bf16), preferred_element_type=f32))
    kz_t = (k.astype(f32) * zeta).T.astype(bf16)
    S_new = jnp.exp(lg * L) * S + jnp.dot(kz_t, v, preferred_element_type=f32)
    S_ref[...] = S_new

    @pl.when(jnp.logical_or(c == npc - 1, c >= npc))
    def _():
        sout_ref[0, 0] = S_new

    mu = jnp.mean(o, axis=-1, keepdims=True)
    d = o - mu
    var = jnp.mean(d * d, axis=-1, keepdims=True)
    on = d * lax.rsqrt(var + RET_GN_EPS) * gnw_ref[...]
    y_ref[...] = (jax.nn.silu(g_ref[...].astype(f32)) * on).astype(y_ref.dtype)


def _retention(proj, state, gn_w, lg, npc, nchunks):
    T = proj.shape[0]
    nb = state.shape[0]
    qb = RET_HEADS
    vb = 2 * RET_HEADS * RET_DK // RET_DV
    gb = vb + RET_HEADS

    def sin_map(h, c):
        return (jnp.maximum(c - npc, 0), h, 0, 0)

    def sout_map(h, c):
        return (jnp.maximum(c - npc + 1, 0), h, 0, 0)

    return pl.pallas_call(
        functools.partial(_ret_kernel, npc=npc),
        out_shape=(jax.ShapeDtypeStruct((T, RET_HEADS * RET_DV), bf16),
                   jax.ShapeDtypeStruct((nb + 1, RET_HEADS, RET_DK, RET_DV), f32)),
        grid=(RET_HEADS, nchunks),
        in_specs=[pl.BlockSpec(memory_space=pltpu.SMEM),
                  pl.BlockSpec((CHUNK, RET_DK), lambda h, c: (c, h)),
                  pl.BlockSpec((CHUNK, RET_DK), lambda h, c: (c, qb + h)),
                  pl.BlockSpec((CHUNK, RET_DV), lambda h, c: (c, vb + h)),
                  pl.BlockSpec((CHUNK, RET_DV), lambda h, c: (c, gb + h)),
                  pl.BlockSpec((1, 1, RET_DK, RET_DV), sin_map),
                  pl.BlockSpec((1, RET_DV), lambda h, c: (0, h))],
        out_specs=(pl.BlockSpec((CHUNK, RET_DV), lambda h, c: (c, h)),
                   pl.BlockSpec((1, 1, RET_DK, RET_DV), sout_map)),
        scratch_shapes=[pltpu.VMEM((RET_DK, RET_DV), f32)],
        compiler_params=_params(("parallel", "arbitrary")),
        name="ret_core",
    )(lg, proj, proj, proj, proj, state, gn_w)


def _rmsnorm_kernel(x_ref, g_ref, o_ref):
    o_ref[...] = _rms(x_ref[...], g_ref[...])


def _rmsnorm(x, g):
    T, D = x.shape
    tm = _pick(T, (512, 256, 128, 64))
    return pl.pallas_call(
        _rmsnorm_kernel,
        out_shape=jax.ShapeDtypeStruct((T, D), f32),
        grid=(T // tm,),
        in_specs=[pl.BlockSpec((tm, D), lambda i: (i, 0)), pl.BlockSpec((1, D), lambda i: (0, 0))],
        out_specs=pl.BlockSpec((tm, D), lambda i: (i, 0)),
        compiler_params=_params(("parallel",)),
        name="rwkv_norm",
    )(x, g)


def _lerpmm_kernel(h_ref, hp_ref, mu_ref, w_ref, o_ref, l_ref, *, tiles_per_d, n_big):
    j = pl.program_id(1)

    @pl.when(jnp.logical_or(j % tiles_per_d == 0, j >= n_big))
    def _():
        h = h_ref[...]
        l_ref[...] = (h + (hp_ref[...] - h) * mu_ref[0]).astype(bf16)

    o_ref[...] = jnp.dot(l_ref[...], w_ref[...], preferred_element_type=f32)


def _lerp_matmul(h, hprev, mu_tiles, wcat, tn, tiles_per_d, n_big):
    T, D = h.shape
    N = wcat.shape[1]
    tm = _pick(T, (512, 256, 128, 64))
    return pl.pallas_call(
        functools.partial(_lerpmm_kernel, tiles_per_d=tiles_per_d, n_big=n_big),
        out_shape=jax.ShapeDtypeStruct((T, N), f32),
        grid=(T // tm, N // tn),
        in_specs=[pl.BlockSpec((tm, D), lambda i, j: (i, 0)),
                  pl.BlockSpec((tm, D), lambda i, j: (i, 0)),
                  pl.BlockSpec((1, 1, D), lambda i, j: (j, 0, 0)),
                  pl.BlockSpec((D, tn), lambda i, j: (0, j))],
        out_specs=pl.BlockSpec((tm, tn), lambda i, j: (i, j)),
        scratch_shapes=[pltpu.VMEM((tm, D), bf16)],
        compiler_params=_params(("parallel", "arbitrary")),
        name="rwkv_proj",
    )(h, hprev, mu_tiles, wcat)


def _softplus(z):
    return jnp.maximum(z, 0.0) + jnp.log(1.0 + jnp.exp(-jnp.abs(z)))


def _rwkv_lora_kernel(p_ref, w2_ref, a2_ref, g2_ref, w0_ref, a0_ref, lw_ref, a_ref, g_ref):
    P = LORA_PAD
    pw = jnp.tanh(p_ref[:, :P]).astype(bf16)
    pa = p_ref[:, P:2 * P].astype(bf16)
    pg = jax.nn.sigmoid(p_ref[:, 2 * P:]).astype(bf16)
    wl = w0_ref[...] + jnp.dot(pw, w2_ref[...], preferred_element_type=f32)
    w_log = -_softplus(-wl) - 0.5
    lw_ref[...] = -jnp.exp(w_log)
    a_ref[...] = jax.nn.sigmoid(a0_ref[...] + jnp.dot(pa, a2_ref[...], preferred_element_type=f32))
    g_ref[...] = jnp.dot(pg, g2_ref[...], preferred_element_type=f32)


def _rwkv_lora(proj, col_block, w2p, a2p, g2p, w0, a0):
    T = proj.shape[0]
    D = w2p.shape[1]
    P = LORA_PAD
    tm = _pick(T, (256, 128, 64))
    wspec = pl.BlockSpec((P, D), lambda i: (0, 0))
    vspec = pl.BlockSpec((1, D), lambda i: (0, 0))
    ospec = pl.BlockSpec((tm, D), lambda i: (i, 0))
    oshape = jax.ShapeDtypeStruct((T, D), f32)
    return pl.pallas_call(
        _rwkv_lora_kernel,
        out_shape=(oshape, oshape, oshape),
        grid=(T // tm,),
        in_specs=[pl.BlockSpec((tm, 3 * P), lambda i: (i, col_block)), wspec, wspec, wspec, vspec, vspec],
        out_specs=(ospec, ospec, ospec),
        compiler_params=_params(("parallel",)),
        name="rwkv_lora",
    )(proj, w2p, a2p, g2p, w0, a0)


def _rwkv_head_chunk(r, k, v, lw, a, g, S0, k_k, k_a, r_k, ln_w, ln_b):
    L, N = r.shape
    kk = k * k_k
    kk = kk / jnp.maximum(jnp.sqrt(jnp.sum(kk * kk, axis=-1, keepdims=True)), 1e-12)
    kmod = k * (1.0 + (a - 1.0) * k_a)
    alpha = -kk
    beta = kk * a

    row = lax.broadcasted_iota(jnp.int32, (L, L), 0)
    col = lax.broadcasted_iota(jnp.int32, (L, L), 1)
    incl = row >= col
    strict = row > col
    cum = jnp.dot(incl.astype(f32), lw, preferred_element_type=f32, precision=lax.Precision.HIGHEST)
    cum_l = cum[L - 1:L, :]
    e_in = jnp.exp(cum)
    e_neg = jnp.exp(-cum)
    e_ex = jnp.exp(cum - lw)
    e_end = jnp.exp(cum_l - cum)

    lhs = jnp.concatenate([alpha * e_ex, r * e_in], axis=0).astype(bf16)
    rhs = jnp.concatenate([beta * e_neg, kmod * e_neg], axis=0).astype(bf16)
    G = lax.dot_general(lhs, rhs, NT_DIMS, preferred_element_type=f32)
    LS = lax.dot_general(lhs, S0.astype(bf16), NT_DIMS, preferred_element_type=f32)
    a_ab = jnp.where(strict, G[:L, :L], 0.0)
    a_ak = jnp.where(strict, G[:L, L:], 0.0)
    a_rb = jnp.where(incl, G[L:, :L], 0.0)
    a_rk = jnp.where(incl, G[L:, L:], 0.0)

    vb = v.astype(bf16)
    x = LS[:L] + jnp.dot(a_ak.astype(bf16), vb, preferred_element_type=f32)
    n = a_ab
    steps = max(1, (L - 1).bit_length())
    for i in range(steps):
        nb = n.astype(bf16)
        if i < steps - 1:
            z = jnp.dot(nb, jnp.concatenate([x, n], axis=1).astype(bf16), preferred_element_type=f32)
            x = x + z[:, :N]
            n = z[:, N:]
        else:
            x = x + jnp.dot(nb, x.astype(bf16), preferred_element_type=f32)
    uv = jnp.concatenate([x, v], axis=0)
    o = LS[L:] + jnp.dot(jnp.concatenate([a_rb, a_rk], axis=1).astype(bf16), uv.astype(bf16),
                         preferred_element_type=f32)
    bk = jnp.concatenate([beta * e_end, kmod * e_end], axis=0).astype(bf16)
    S_new = S0 * jnp.exp(cum_l) + jnp.dot(uv.T.astype(bf16), bk, preferred_element_type=f32)

    mu = jnp.mean(o, axis=-1, keepdims=True)
    d = o - mu
    var = jnp.mean(d * d, axis=-1, keepdims=True)
    on = d * lax.rsqrt(var + RWKV_GN_EPS) * ln_w + ln_b
    bonus = jnp.sum(r * kmod * r_k, axis=-1, keepdims=True) * v
    return (on + bonus) * g, S_new


def _rwkv_core_kernel(r_ref, k_ref, v_ref, lw_ref, a_ref, g_ref, s0_ref,
                      kk_ref, ka_ref, rk_ref, lnw_ref, lnb_ref, y_ref, sout_ref, S_ref, *, npc):
    c = pl.program_id(1)
    N = RWKV_HS

    @pl.when(c == 0)
    def _():
        S_ref[...] = jnp.zeros_like(S_ref)

    @pl.when(c >= npc)
    def _():
        S_ref[...] = s0_ref[0]

    ys = []
    for hh in range(2):
        sl = slice(hh * N, (hh + 1) * N)
        y, S_new = _rwkv_head_chunk(
            r_ref[:, sl], k_ref[:, sl], v_ref[:, sl], lw_ref[:, sl], a_ref[:, sl], g_ref[:, sl],
            S_ref[hh], kk_ref[:, sl], ka_ref[:, sl], rk_ref[:, sl], lnw_ref[:, sl], lnb_ref[:, sl])
        S_ref[hh] = S_new
        ys.append(y)

        @pl.when(jnp.logical_or(c == npc - 1, c >= npc))
        def _(S_new=S_new, hh=hh):
            sout_ref[0, hh] = S_new

    y_ref[...] = jnp.concatenate(ys, axis=1).astype(y_ref.dtype)


def _rwkv_core(proj, lw, a, g, state, k_k, k_a, r_k, ln_w, ln_b, npc, nchunks):
    T, D = lw.shape
    N = RWKV_HS
    nh = D // N
    nb = state.shape[0]
    kb = D // (2 * N)

    def tok(off):
        return pl.BlockSpec((CHUNK, 2 * N), lambda p, c: (c, off + p))

    pspec = pl.BlockSpec((1, 2 * N), lambda p, c: (0, p))

    def sin_map(p, c):
        return (jnp.maximum(c - npc, 0), p, 0, 0)

    def sout_map(p, c):
        return (jnp.maximum(c - npc + 1, 0), p, 0, 0)

    return pl.pallas_call(
        functools.partial(_rwkv_core_kernel, npc=npc),
        out_shape=(jax.ShapeDtypeStruct((T, D), bf16),
                   jax.ShapeDtypeStruct((nb + 1, nh, N, N), f32)),
        grid=(nh // 2, nchunks),
        in_specs=[tok(0), tok(kb), tok(2 * kb), tok(0), tok(0), tok(0),
                  pl.BlockSpec((1, 2, N, N), sin_map),
                  pspec, pspec, pspec, pspec, pspec],
        out_specs=(tok(0), pl.BlockSpec((1, 2, N, N), sout_map)),
        scratch_shapes=[pltpu.VMEM((2, N, N), f32)],
        compiler_params=_params(("parallel", "arbitrary")),
        name="rwkv_core",
    )(proj, proj, proj, lw, a, g, state, k_k, k_a, r_k, ln_w, ln_b)


def _positions(n_prompt, n_b, n_s):
    return jnp.concatenate([jnp.arange(n_prompt), jnp.tile(PAST_LEN + jnp.arange(n_s), n_b)]).astype(f32)


def _attn_tables(pos):
    half = ROT_DIM // 2
    inv = ROPE_THETA ** (-jnp.arange(half, dtype=f32) / half)
    ang = pos[:, None] * inv[None, :]
    cos, sin = jnp.cos(ang), jnp.sin(ang)
    T = pos.shape[0]
    zeros = jnp.zeros((T, half), f32)
    rest = HEAD_DIM - ROT_DIM
    c = jnp.concatenate([cos, cos, jnp.ones((T, rest), f32)], axis=1)
    s1 = jnp.concatenate([zeros, sin, jnp.zeros((T, rest), f32)], axis=1)
    s2 = jnp.concatenate([-sin, zeros, jnp.zeros((T, rest), f32)], axis=1)
    rep = 256 // HEAD_DIM
    return tuple(jnp.tile(t, (1, rep)) for t in (c, s1, s2))


def _ret_tables(pos):
    half = RET_DK // 2
    inv = RET_THETA ** (-jnp.arange(half, dtype=f32) / half)
    ang = pos[:, None] * inv[None, :]
    return jnp.cos(ang), jnp.sin(ang)


def _pad_cols(w, n):
    return jnp.pad(w, ((0, 0), (0, n - w.shape[1])))


def _pad_rows(w, n):
    return jnp.pad(w, ((0, n - w.shape[0]), (0, 0)))


def kernel(x_prompt, x_sample, cache_attn_k, cache_attn_v, state_ret, state_rwkv, state_rwkv_shift,
           norm_mix, norm_mlp, norm_final,
           attn_w_qkv, attn_sinks, attn_w_o,
           ret_w_in, ret_gn_w, ret_w_o,
           rwkv_mu, rwkv_w_rkv, rwkv_w_o, rwkv_w0, rwkv_w1, rwkv_w2, rwkv_a0, rwkv_a1, rwkv_a2,
           rwkv_g1, rwkv_g2, rwkv_k_k, rwkv_k_a, rwkv_r_k, rwkv_ln_w, rwkv_ln_b,
           mlp_w_up, mlp_w_down):
    bp, tp, D = x_prompt.shape
    nb, ns, _ = x_sample.shape
    assert bp == 1 and ns == CHUNK and tp % CHUNK == 0
    depth = norm_mix.shape[0]
    npc = tp // CHUNK
    nchunks = npc + nb
    T = tp + nb * ns
    qd = ATTN_HEADS * HEAD_DIM
    kvd = ATTN_KV_HEADS * HEAD_DIM

    x = jnp.concatenate([x_prompt.reshape(tp, D), x_sample.reshape(nb * ns, D)], axis=0)
    pos = _positions(tp, nb, ns)
    attn_tabs = _attn_tables(pos)
    ret_tabs = _ret_tables(pos)
    lg = jnp.log1p(-jnp.exp2(-5.0 - jnp.arange(RET_HEADS, dtype=f32)))
    tm_proj = _pick(T, (512, 256, 128, 64))

    kp_l, vp_l, ks_l, vs_l, rp_l, rs_l, wp_l, ws_l, shp_l, shs_l = ([] for _ in range(10))
    for i in range(depth):
        j, kind = divmod(i, 3)
        g_mix = norm_mix[i][None, :]
        if kind == 0:
            modes = [["q"] * 5, ["q"] * 3 + ["k", "plain"]]
            qkv = _norm_matmul(x, g_mix, attn_w_qkv[j].astype(bf16), attn_tabs, modes, _epi_attn,
                               f32, tm_proj, "attn_qkv")
            k_new = qkv[:, qd:qd + kvd]
            v_new = qkv[:, qd + kvd:]

            def ext(new, cache):
                samp = jnp.concatenate([cache.reshape(nb, WINDOW, kvd), new[tp:].reshape(nb, ns, kvd)], axis=1)
                return jnp.concatenate([jnp.zeros((WINDOW, kvd), f32), new[:tp],
                                        samp.reshape(nb * (WINDOW + ns), kvd)], axis=0), samp

            kext, k_samp = ext(k_new, cache_attn_k[j])
            vext, v_samp = ext(v_new, cache_attn_v[j])
            o = _attention(qkv, kext, vext, attn_sinks[j], npc, nchunks)
            x = _matmul_residual(o, attn_w_o[j].astype(bf16), x, "attn_out")
            kp_l.append(k_new[tp - WINDOW:tp].reshape(1, WINDOW, ATTN_KV_HEADS, HEAD_DIM))
            vp_l.append(v_new[tp - WINDOW:tp].reshape(1, WINDOW, ATTN_KV_HEADS, HEAD_DIM))
            ks_l.append(k_samp[:, -WINDOW:].reshape(nb, WINDOW, ATTN_KV_HEADS, HEAD_DIM))
            vs_l.append(v_samp[:, -WINDOW:].reshape(nb, WINDOW, ATTN_KV_HEADS, HEAD_DIM))
        elif kind == 1:
            nq = RET_HEADS * RET_DK // 1024
            nv = RET_HEADS * RET_DV // 1024
            modes = [["q"] * 4] * nq + [["k"] * 4] * nq + [["plain"] * 4] * (2 * nv)
            proj = _norm_matmul(x, g_mix, ret_w_in[j].astype(bf16), ret_tabs, modes, _epi_ret,
                                bf16, tm_proj, "ret_proj")
            y, s_all = _retention(proj, state_ret[j], ret_gn_w[j][None, :], lg, npc, nchunks)
            x = _matmul_residual(y, ret_w_o[j].astype(bf16), x, "ret_out")
            rp_l.append(s_all[:1])
            rs_l.append(s_all[1:])
        else:
            h = _rmsnorm(x, g_mix)
            hp3 = h[:tp]
            hs3 = h[tp:].reshape(nb, ns, D)
            hprev = jnp.concatenate(
                [jnp.zeros((1, D), f32), hp3[:-1],
                 jnp.concatenate([state_rwkv_shift[j][:, None, :], hs3[:, :-1]], axis=1).reshape(nb * ns, D)],
                axis=0)
            P = LORA_PAD
            tn = 512
            wcat = jnp.concatenate(
                [rwkv_w_rkv[j][0], rwkv_w_rkv[j][1], rwkv_w_rkv[j][2],
                 _pad_cols(rwkv_w1[j], P), _pad_cols(rwkv_a1[j], P), _pad_cols(rwkv_g1[j], P)],
                axis=1).astype(bf16)
            tiles_per_d = D // tn
            mu = rwkv_mu[j]
            mu_tiles = jnp.concatenate(
                [jnp.repeat(mu[jnp.array([0, 2, 3])], tiles_per_d, axis=0), mu[jnp.array([1, 4, 5])]],
                axis=0)[:, None, :]
            proj = _lerp_matmul(h, hprev, mu_tiles, wcat, tn, tiles_per_d, 3 * tiles_per_d)
            lw, a, g = _rwkv_lora(
                proj, 3 * D // (3 * P),
                _pad_rows(rwkv_w2[j], P).astype(bf16), _pad_rows(rwkv_a2[j], P).astype(bf16),
                _pad_rows(rwkv_g2[j], P).astype(bf16), rwkv_w0[j][None, :], rwkv_a0[j][None, :])
            y, s_all = _rwkv_core(proj, lw, a, g, state_rwkv[j],
                                  rwkv_k_k[j][None, :], rwkv_k_a[j][None, :], rwkv_r_k[j].reshape(1, D),
                                  rwkv_ln_w[j][None, :], rwkv_ln_b[j][None, :], npc, nchunks)
            x = _matmul_residual(y, rwkv_w_o[j].astype(bf16), x, "rwkv_out")
            wp_l.append(s_all[:1])
            ws_l.append(s_all[1:])
            shp_l.append(h[tp - 1:tp])
            shs_l.append(hs3[:, -1])
        x = _mlp(x, norm_mlp[i][None, :], mlp_w_up[i].astype(bf16), mlp_w_down[i].astype(bf16),
                 norm_final[None, :], i == depth - 1, "mlp")

    y_prompt = x[:tp].reshape(1, tp, D)
    y_sample = x[tp:].reshape(nb, ns, D)
    return (y_prompt, y_sample,
            jnp.stack(kp_l), jnp.stack(vp_l), jnp.stack(ks_l), jnp.stack(vs_l),
            jnp.stack(rp_l), jnp.stack(rs_l),
            jnp.stack(wp_l), jnp.stack(ws_l), jnp.stack(shp_l), jnp.stack(shs_l))
```

```python
import functools

import jax
import jax.numpy as jnp
from jax import lax
from jax.experimental import pallas as pl
from jax.experimental.pallas import tpu as pltpu

f32 = jnp.float32
bf16 = jnp.bfloat16

CHUNK = 64
NORM_EPS = 1e-5
PAST_LEN = 4096

ATTN_HEADS = 32
ATTN_KV_HEADS = 4
ATTN_GROUP = ATTN_HEADS // ATTN_KV_HEADS
HEAD_DIM = 64
WINDOW = 128
ROT_DIM = HEAD_DIM // 4
ROPE_THETA = 500000.0

RET_HEADS = 8
RET_DK = 256
RET_DV = 512
RET_THETA = 10000.0
RET_GN_EPS = 1e-5

RWKV_HS = 64
RWKV_GN_EPS = 64e-5
LORA_PAD = 512

VMEM_LIMIT = 52 * 1024 * 1024

NT_DIMS = (((1,), (1,)), ((), ()))


def _pick(n, cands):
    for c in cands:
        if n % c == 0:
            return c
    raise ValueError(f"no tile for {n} in {cands}")


def _params(sem):
    return pltpu.CompilerParams(dimension_semantics=sem, vmem_limit_bytes=VMEM_LIMIT)


def _rms(x, g):
    return x * lax.rsqrt(jnp.mean(x * x, axis=-1, keepdims=True) + NORM_EPS) * g


def _epi_attn(blk, mode, tabs):
    if mode == "plain":
        return blk
    c_ref, s1_ref, s2_ref = tabs
    out = (blk * c_ref[...] + pltpu.roll(blk, 8, 1) * s1_ref[...]
           + pltpu.roll(blk, 256 - 8, 1) * s2_ref[...])
    return out * (HEAD_DIM ** -0.5) if mode == "q" else out


def _epi_ret(blk, mode, tabs):
    if mode == "plain":
        return blk
    cos_ref, sin_ref = tabs
    c, s = cos_ref[...], sin_ref[...]
    x1, x2 = blk[:, :128], blk[:, 128:]
    out = jnp.concatenate([x1 * c - x2 * s, x2 * c + x1 * s], axis=1)
    return out * (RET_DK ** -0.5) if mode == "k" else out


def _normmm_kernel(*refs, modes, ntab, epi):
    x_ref, g_ref, w_ref = refs[:3]
    tabs = refs[3:3 + ntab]
    o_ref, xn_ref = refs[3 + ntab], refs[4 + ntab]
    j = pl.program_id(1)

    @pl.when(j == 0)
    def _():
        xn_ref[...] = _rms(x_ref[...], g_ref[...]).astype(bf16)

    acc = jnp.dot(xn_ref[...], w_ref[...], preferred_element_type=f32)
    groups = {}
    for jj, tile_modes in enumerate(modes):
        groups.setdefault(tuple(tile_modes), []).append(jj)
    for tile_modes, jjs in groups.items():
        cond = functools.reduce(jnp.logical_or, [j == jj for jj in jjs])

        @pl.when(cond)
        def _(tile_modes=tile_modes):
            for s, mode in enumerate(tile_modes):
                blk = acc[:, s * 256:(s + 1) * 256]
                o_ref[:, s * 256:(s + 1) * 256] = epi(blk, mode, tabs).astype(o_ref.dtype)


def _norm_matmul(x, g, w, tabs, modes, epi, out_dtype, tm, name):
    T, D = x.shape
    N = w.shape[1]
    ntile = len(modes)
    tn = N // ntile
    assert tn * ntile == N and tn == 256 * len(modes[0])
    tab_specs = [pl.BlockSpec((tm, t.shape[1]), lambda i, j: (i, 0)) for t in tabs]
    return pl.pallas_call(
        functools.partial(_normmm_kernel, modes=modes, ntab=len(tabs), epi=epi),
        out_shape=jax.ShapeDtypeStruct((T, N), out_dtype),
        grid=(T // tm, ntile),
        in_specs=[pl.BlockSpec((tm, D), lambda i, j: (i, 0)),
                  pl.BlockSpec((1, D), lambda i, j: (0, 0)),
                  pl.BlockSpec((D, tn), lambda i, j: (0, j))] + tab_specs,
        out_specs=pl.BlockSpec((tm, tn), lambda i, j: (i, j)),
        scratch_shapes=[pltpu.VMEM((tm, D), bf16)],
        compiler_params=_params(("parallel", "arbitrary")),
        name=name,
    )(x, g, w, *tabs)


def _mmres_kernel(a_ref, w_ref, r_ref, o_ref):
    o_ref[...] = r_ref[...] + jnp.dot(a_ref[...], w_ref[...], preferred_element_type=f32)


def _matmul_residual(a, w, res, name):
    T, K = a.shape
    N = w.shape[1]
    tm = _pick(T, (512, 256, 128, 64))
    tn = _pick(N, (1024, 512, 256))
    return pl.pallas_call(
        _mmres_kernel,
        out_shape=jax.ShapeDtypeStruct((T, N), f32),
        grid=(T // tm, N // tn),
        in_specs=[pl.BlockSpec((tm, K), lambda i, j: (i, 0)),
                  pl.BlockSpec((K, tn), lambda i, j: (0, j)),
                  pl.BlockSpec((tm, tn), lambda i, j: (i, j))],
        out_specs=pl.BlockSpec((tm, tn), lambda i, j: (i, j)),
        compiler_params=_params(("parallel", "parallel")),
        name=name,
    )(a, w, res)


def _mlp_kernel(x_ref, g_ref, wu_ref, wd_ref, gf_ref, o_ref, xn_ref, *, nk, final):
    k = pl.program_id(1)

    @pl.when(k == 0)
    def _():
        x = x_ref[...]
        xn_ref[...] = _rms(x, g_ref[...]).astype(bf16)
        o_ref[...] = x

    h = jnp.dot(xn_ref[...], wu_ref[...], preferred_element_type=f32)
    h = jnp.square(jnp.maximum(h, 0.0)).astype(bf16)
    o_ref[...] += jnp.dot(h, wd_ref[...], preferred_element_type=f32)

    if final:
        @pl.when(k == nk - 1)
        def _():
            o_ref[...] = _rms(o_ref[...], gf_ref[...])


def _mlp(x, g, w_up, w_down, g_final, final, name):
    T, D = x.shape
    F = w_up.shape[1]
    tm = _pick(T, (512, 256, 128, 64))
    tf = _pick(F, (1024, 512))
    nk = F // tf
    return pl.pallas_call(
        functools.partial(_mlp_kernel, nk=nk, final=final),
        out_shape=jax.ShapeDtypeStruct((T, D), f32),
        grid=(T // tm, nk),
        in_specs=[pl.BlockSpec((tm, D), lambda i, k: (i, 0)),
                  pl.BlockSpec((1, D), lambda i, k: (0, 0)),
                  pl.BlockSpec((D, tf), lambda i, k: (0, k)),
                  pl.BlockSpec((tf, D), lambda i, k: (k, 0)),
                  pl.BlockSpec((1, D), lambda i, k: (0, 0))],
        out_specs=pl.BlockSpec((tm, D), lambda i, k: (i, 0)),
        scratch_shapes=[pltpu.VMEM((tm, D), bf16)],
        compiler_params=_params(("parallel", "arbitrary")),
        name=name,
    )(x, g, w_up, w_down, g_final)


def _attn_kernel(sink_ref, q_ref, k0_ref, k1_ref, k2_ref, v0_ref, v1_ref, v2_ref, o_ref, *, npc):
    c = pl.program_id(0)
    nkeys = 3 * CHUNK
    key_blk = lax.broadcasted_iota(jnp.int32, (1, nkeys), 1) // CHUNK
    valid = jnp.logical_or(c >= npc, c + key_blk >= 2)
    bias = jnp.where(valid, 0.0, -jnp.inf).astype(f32)
    kcat = jnp.concatenate([k0_ref[...], k1_ref[...], k2_ref[...]], axis=0)
    vcat = jnp.concatenate([v0_ref[...], v1_ref[...], v2_ref[...]], axis=0)
    for kv in range(ATTN_KV_HEADS):
        kg = kcat[:, kv * HEAD_DIM:(kv + 1) * HEAD_DIM].astype(bf16)
        vg = vcat[:, kv * HEAD_DIM:(kv + 1) * HEAD_DIM].astype(bf16)
        heads = [kv * ATTN_GROUP + i for i in range(ATTN_GROUP)]
        qg = jnp.concatenate([q_ref[:, h * HEAD_DIM:(h + 1) * HEAD_DIM] for h in heads],
                             axis=0).astype(bf16)
        s = lax.dot_general(qg, kg, NT_DIMS, preferred_element_type=f32) + bias
        sk = jnp.concatenate([jnp.full((CHUNK, 1), sink_ref[h], f32) for h in heads], axis=0)
        m = jnp.maximum(jnp.max(s, axis=-1, keepdims=True), sk)
        p = jnp.exp(s - m)
        denom = jnp.sum(p, axis=-1, keepdims=True) + jnp.exp(sk - m)
        o = jnp.dot(p.astype(bf16), vg, preferred_element_type=f32) / denom
        for i, h in enumerate(heads):
            o_ref[:, h * HEAD_DIM:(h + 1) * HEAD_DIM] = o[i * CHUNK:(i + 1) * CHUNK].astype(o_ref.dtype)


def _attention(qkv, kext, vext, sinks, npc, nchunks):
    T = qkv.shape[0]
    qd = ATTN_HEADS * HEAD_DIM
    kvd = ATTN_KV_HEADS * HEAD_DIM

    def kv_spec(j):
        def imap(c):
            return (jnp.where(c < npc, c, npc + 2 + 3 * (c - npc)) + j, 0)
        return pl.BlockSpec((CHUNK, kvd), imap)

    return pl.pallas_call(
        functools.partial(_attn_kernel, npc=npc),
        out_shape=jax.ShapeDtypeStruct((T, qd), bf16),
        grid=(nchunks,),
        in_specs=[pl.BlockSpec(memory_space=pltpu.SMEM),
                  pl.BlockSpec((CHUNK, qd), lambda c: (c, 0))]
                 + [kv_spec(j) for j in range(3)] + [kv_spec(j) for j in range(3)],
        out_specs=pl.BlockSpec((CHUNK, qd), lambda c: (c, 0)),
        compiler_params=_params(("parallel",)),
        name="attn_core",
    )(sinks, qkv, kext, kext, kext, vext, vext, vext)


def _ret_kernel(lg_ref, q_ref, k_ref, v_ref, g_ref, s0_ref, gnw_ref, y_ref, sout_ref, S_ref, *, npc):
    h = pl.program_id(0)
    c = pl.program_id(1)
    L = CHUNK
    lg = lg_ref[h]

    @pl.when(c == 0)
    def _():
        S_ref[...] = jnp.zeros_like(S_ref)

    @pl.when(c >= npc)
    def _():
        S_ref[...] = s0_ref[0, 0]

    q = q_ref[...]
    k = k_ref[...]
    v = v_ref[...]
    row = lax.broadcasted_iota(jnp.int32, (L, L), 0)
    col = lax.broadcasted_iota(jnp.int32, (L, L), 1)
    diff = (row - col).astype(f32)
    decay = jnp.where(diff >= 0, jnp.exp(lg * jnp.maximum(diff, 0.0)), 0.0)
    idx = lax.broadcasted_iota(jnp.int32, (L, 1), 0).astype(f32)
    xi = jnp.exp(lg * (idx + 1.0))
    zeta = jnp.exp(lg * (L - 1.0 - idx))

    S = S_ref[...]
    scores = lax.dot_general(q, k, NT_DIMS, preferred_element_type=f32) * decay
    o = (jnp.dot(scores.astype(bf16), v, preferred_element_type=f32)
         + jnp.dot((q.astype(f32) * xi).astype(bf16), S.astype(bf16), preferred_element_type=f32))
    kz_t = (k.astype(f32) * zeta).T.astype(bf16)
    S_new = jnp.exp(lg * L) * S + jnp.dot(kz_t, v, preferred_element_type=f32)
    S_ref[...] = S_new

    @pl.when(jnp.logical_or(c == npc - 1, c >= npc))
    def _():
        sout_ref[0, 0] = S_new

    mu = jnp.mean(o, axis=-1, keepdims=True)
    d = o - mu
    var = jnp.mean(d * d, axis=-1, keepdims=True)
    on = d * lax.rsqrt(var + RET_GN_EPS) * gnw_ref[...]
    y_ref[...] = (jax.nn.silu(g_ref[...].astype(f32)) * on).astype(y_ref.dtype)


def _retention(proj, state, gn_w, lg, npc, nchunks):
    T = proj.shape[0]
    nb = state.shape[0]
    qb = RET_HEADS
    vb = 2 * RET_HEADS * RET_DK // RET_DV
    gb = vb + RET_HEADS

    def sin_map(h, c):
        return (jnp.maximum(c - npc, 0), h, 0, 0)

    def sout_map(h, c):
        return (jnp.maximum(c - npc + 1, 0), h, 0, 0)

    return pl.pallas_call(
        functools.partial(_ret_kernel, npc=npc),
        out_shape=(jax.ShapeDtypeStruct((T, RET_HEADS * RET_DV), bf16),
                   jax.ShapeDtypeStruct((nb + 1, RET_HEADS, RET_DK, RET_DV), f32)),
        grid=(RET_HEADS, nchunks),
        in_specs=[pl.BlockSpec(memory_space=pltpu.SMEM),
                  pl.BlockSpec((CHUNK, RET_DK), lambda h, c: (c, h)),
                  pl.BlockSpec((CHUNK, RET_DK), lambda h, c: (c, qb + h)),
                  pl.BlockSpec((CHUNK, RET_DV), lambda h, c: (c, vb + h)),
                  pl.BlockSpec((CHUNK, RET_DV), lambda h, c: (c, gb + h)),
                  pl.BlockSpec((1, 1, RET_DK, RET_DV), sin_map),
                  pl.BlockSpec((1, RET_DV), lambda h, c: (0, h))],
        out_specs=(pl.BlockSpec((CHUNK, RET_DV), lambda h, c: (c, h)),
                   pl.BlockSpec((1, 1, RET_DK, RET_DV), sout_map)),
        scratch_shapes=[pltpu.VMEM((RET_DK, RET_DV), f32)],
        compiler_params=_params(("parallel", "arbitrary")),
        name="ret_core",
    )(lg, proj, proj, proj, proj, state, gn_w)


def _rmsnorm_kernel(x_ref, g_ref, o_ref):
    o_ref[...] = _rms(x_ref[...], g_ref[...])


def _rmsnorm(x, g):
    T, D = x.shape
    tm = _pick(T, (512, 256, 128, 64))
    return pl.pallas_call(
        _rmsnorm_kernel,
        out_shape=jax.ShapeDtypeStruct((T, D), f32),
        grid=(T // tm,),
        in_specs=[pl.BlockSpec((tm, D), lambda i: (i, 0)), pl.BlockSpec((1, D), lambda i: (0, 0))],
        out_specs=pl.BlockSpec((tm, D), lambda i: (i, 0)),
        compiler_params=_params(("parallel",)),
        name="rwkv_norm",
    )(x, g)


def _lerpmm_kernel(h_ref, hp_ref, mu_ref, w_ref, o_ref, l_ref, *, tiles_per_d, n_big):
    j = pl.program_id(1)

    @pl.when(jnp.logical_or(j % tiles_per_d == 0, j >= n_big))
    def _():
        h = h_ref[...]
        l_ref[...] = (h + (hp_ref[...] - h) * mu_ref[0]).astype(bf16)

    o_ref[...] = jnp.dot(l_ref[...], w_ref[...], preferred_element_type=f32)


def _lerp_matmul(h, hprev, mu_tiles, wcat, tn, tiles_per_d, n_big):
    T, D = h.shape
    N = wcat.shape[1]
    tm = _pick(T, (512, 256, 128, 64))
    return pl.pallas_call(
        functools.partial(_lerpmm_kernel, tiles_per_d=tiles_per_d, n_big=n_big),
        out_shape=jax.ShapeDtypeStruct((T, N), f32),
        grid=(T // tm, N // tn),
        in_specs=[pl.BlockSpec((tm, D), lambda i, j: (i, 0)),
                  pl.BlockSpec((tm, D), lambda i, j: (i, 0)),
                  pl.BlockSpec((1, 1, D), lambda i, j: (j, 0, 0)),
                  pl.BlockSpec((D, tn), lambda i, j: (0, j))],
        out_specs=pl.BlockSpec((tm, tn), lambda i, j: (i, j)),
        scratch_shapes=[pltpu.VMEM((tm, D), bf16)],
        compiler_params=_params(("parallel", "arbitrary")),
        name="rwkv_proj",
    )(h, hprev, mu_tiles, wcat)


def _softplus(z):
    return jnp.maximum(z, 0.0) + jnp.log(1.0 + jnp.exp(-jnp.abs(z)))


SEG_TILE = 256


def _head_sum(x):
    rows, D = x.shape
    r = lax.broadcasted_iota(jnp.int32, (SEG_TILE, SEG_TILE), 0) // RWKV_HS
    c = lax.broadcasted_iota(jnp.int32, (SEG_TILE, SEG_TILE), 1) // RWKV_HS
    ones = (r == c).astype(bf16)
    out = []
    for j in range(D // SEG_TILE):
        blk = x[:, j * SEG_TILE:(j + 1) * SEG_TILE]
        hi = blk.astype(bf16)
        lo = (blk - hi.astype(f32)).astype(bf16)
        out.append(jnp.dot(hi, ones, preferred_element_type=f32)
                   + jnp.dot(lo, ones, preferred_element_type=f32))
    return jnp.concatenate(out, axis=1)


def _rwkv_prep_kernel(p_ref, r_ref, k_ref, v_ref, w2_ref, a2_ref, g2_ref, w0_ref, a0_ref,
                      kk_ref, ka_ref, rk_ref,
                      at_ref, rt_ref, bt_ref, kt_ref, bh_ref, kh_ref, v16_ref, gl_ref, g_ref, bonus_ref):
    P = LORA_PAD
    tm = r_ref.shape[0]
    pw = jnp.tanh(p_ref[:, :P]).astype(bf16)
    pa = p_ref[:, P:2 * P].astype(bf16)
    pg = jax.nn.sigmoid(p_ref[:, 2 * P:]).astype(bf16)
    wl = w0_ref[...] + jnp.dot(pw, w2_ref[...], preferred_element_type=f32)
    lw = -jnp.exp(-_softplus(-wl) - 0.5)
    a = jax.nn.sigmoid(a0_ref[...] + jnp.dot(pa, a2_ref[...], preferred_element_type=f32))
    g_ref[...] = jnp.dot(pg, g2_ref[...], preferred_element_type=f32)

    row = lax.broadcasted_iota(jnp.int32, (tm, tm), 0)
    col = lax.broadcasted_iota(jnp.int32, (tm, tm), 1)
    same = (row // CHUNK) == (col // CHUNK)
    tri = jnp.logical_and(same, row >= col).astype(f32)
    cum = jnp.dot(tri, lw, preferred_element_type=f32, precision=lax.Precision.HIGHEST)
    tot = jnp.dot(same.astype(f32), lw, preferred_element_type=f32, precision=lax.Precision.HIGHEST)

    r, k, v = r_ref[...], k_ref[...], v_ref[...]
    kk = k * kk_ref[...]
    kk = kk / jnp.maximum(jnp.sqrt(_head_sum(kk * kk)), 1e-12)
    kmod = k * (1.0 + (a - 1.0) * ka_ref[...])
    beta = kk * a
    e_neg = jnp.exp(-cum)
    e_end = jnp.exp(tot - cum)
    at_ref[...] = (-kk * jnp.exp(cum - lw)).astype(bf16)
    rt_ref[...] = (r * jnp.exp(cum)).astype(bf16)
    bt_ref[...] = (beta * e_neg).astype(bf16)
    kt_ref[...] = (kmod * e_neg).astype(bf16)
    bh_ref[...] = (beta * e_end).astype(bf16)
    kh_ref[...] = (kmod * e_end).astype(bf16)
    v16_ref[...] = v.astype(bf16)
    bonus_ref[...] = _head_sum(r * kmod * rk_ref[...]) * v
    for ci in range(tm // CHUNK):
        gl_ref[ci] = jnp.exp(tot[ci * CHUNK:ci * CHUNK + 1, :])


def _rwkv_prep(proj, lora_block, w2p, a2p, g2p, w0, a0, k_k, k_a, r_k):
    T = proj.shape[0]
    D = w2p.shape[1]
    P = LORA_PAD
    tm = _pick(T, (128, 64))
    wspec = pl.BlockSpec((P, D), lambda i: (0, 0))
    vspec = pl.BlockSpec((1, D), lambda i: (0, 0))
    ospec = pl.BlockSpec((tm, D), lambda i: (i, 0))
    o16 = jax.ShapeDtypeStruct((T, D), bf16)
    o32 = jax.ShapeDtypeStruct((T, D), f32)
    return pl.pallas_call(
        _rwkv_prep_kernel,
        out_shape=(o16,) * 7 + (jax.ShapeDtypeStruct((T // CHUNK, 1, D), f32), o32, o32),
        grid=(T // tm,),
        in_specs=[pl.BlockSpec((tm, 3 * P), lambda i: (i, lora_block)),
                  pl.BlockSpec((tm, D), lambda i: (i, 0)),
                  pl.BlockSpec((tm, D), lambda i: (i, 1)),
                  pl.BlockSpec((tm, D), lambda i: (i, 2)),
                  wspec, wspec, wspec, vspec, vspec, vspec, vspec, vspec],
        out_specs=(ospec,) * 7 + (pl.BlockSpec((tm // CHUNK, 1, D), lambda i: (i, 0, 0)), ospec, ospec),
        compiler_params=_params(("parallel",)),
        name="rwkv_prep",
    )(proj, proj, proj, proj, w2p, a2p, g2p, w0, a0, k_k, k_a, r_k)


def _rwkv_heads_chunk(heads):
    L, N = heads[0][6].shape
    nh = len(heads)
    dot = functools.partial(jnp.dot, preferred_element_type=f32)
    row = lax.broadcasted_iota(jnp.int32, (L, L), 0)
    col = lax.broadcasted_iota(jnp.int32, (L, L), 1)
    incl = row >= col
    strict = row > col
    lhs = [jnp.concatenate([h[0], h[1]], axis=0) for h in heads]
    rhs = [jnp.concatenate([h[2], h[3]], axis=0) for h in heads]
    G = [lax.dot_general(lhs[i], rhs[i], NT_DIMS, preferred_element_type=f32) for i in range(nh)]
    LS = [lax.dot_general(lhs[i], heads[i][7].astype(bf16), NT_DIMS, preferred_element_type=f32)
          for i in range(nh)]
    n = [jnp.where(strict, g[:L, :L], 0.0) for g in G]
    a_ak = [jnp.where(strict, g[:L, L:], 0.0).astype(bf16) for g in G]
    a_r = [jnp.concatenate([jnp.where(incl, g[L:, :L], 0.0), jnp.where(incl, g[L:, L:], 0.0)],
                           axis=1).astype(bf16) for g in G]
    x = [LS[i][:L] + dot(a_ak[i], heads[i][6]) for i in range(nh)]
    steps = max(1, (L - 1).bit_length())
    for s in range(steps):
        if s < steps - 1:
            z = [dot(n[i].astype(bf16), jnp.concatenate([x[i], n[i]], axis=1).astype(bf16))
                 for i in range(nh)]
            x = [x[i] + z[i][:, :N] for i in range(nh)]
            n = [z[i][:, N:] for i in range(nh)]
        else:
            x = [x[i] + dot(n[i].astype(bf16), x[i].astype(bf16)) for i in range(nh)]
    uv = [jnp.concatenate([x[i], heads[i][6].astype(f32)], axis=0) for i in range(nh)]
    o = [LS[i][L:] + dot(a_r[i], uv[i].astype(bf16)) for i in range(nh)]
    S_new = [heads[i][7] * heads[i][8]
             + dot(uv[i].T.astype(bf16), jnp.concatenate([heads[i][4], heads[i][5]], axis=0))
             for i in range(nh)]
    return list(zip(o, S_new))


def _rwkv_core_kernel(at_ref, rt_ref, bt_ref, kt_ref, bh_ref, kh_ref, v_ref, gl_ref, s0_ref,
                      o_ref, sout_ref, S_ref, *, npc, nhead):
    c = pl.program_id(1)
    N = RWKV_HS

    @pl.when(c == 0)
    def _():
        S_ref[...] = jnp.zeros_like(S_ref)

    @pl.when(c >= npc)
    def _():
        S_ref[...] = s0_ref[0]

    gl = gl_ref[0]
    heads = []
    for hh in range(nhead):
        sl = slice(hh * N, (hh + 1) * N)
        heads.append((at_ref[:, sl], rt_ref[:, sl], bt_ref[:, sl], kt_ref[:, sl], bh_ref[:, sl],
                      kh_ref[:, sl], v_ref[:, sl], S_ref[hh], gl[:, sl]))
    res = _rwkv_heads_chunk(heads)
    for hh in range(nhead):
        S_ref[hh] = res[hh][1]
    o_ref[...] = jnp.concatenate([o for o, _ in res], axis=1)

    @pl.when(jnp.logical_or(c == npc - 1, c >= npc))
    def _():
        sout_ref[0] = S_ref[...]


RWKV_HEADS_PER_STEP = 32


def _rwkv_core(ops, v16, gl, state, npc, nchunks):
    T, D = v16.shape
    N = RWKV_HS
    nh = D // N
    nb = state.shape[0]
    G = RWKV_HEADS_PER_STEP
    W = G * N
    tok = pl.BlockSpec((CHUNK, W), lambda p, c: (c, p))

    def sin_map(p, c):
        return (jnp.maximum(c - npc, 0), p, 0, 0)

    def sout_map(p, c):
        return (jnp.maximum(c - npc + 1, 0), p, 0, 0)

    return pl.pallas_call(
        functools.partial(_rwkv_core_kernel, npc=npc, nhead=G),
        out_shape=(jax.ShapeDtypeStruct((T, D), f32),
                   jax.ShapeDtypeStruct((nb + 1, nh, N, N), f32)),
        grid=(nh // G, nchunks),
        in_specs=[tok] * 7 + [pl.BlockSpec((1, 1, W), lambda p, c: (c, 0, p)),
                              pl.BlockSpec((1, G, N, N), sin_map)],
        out_specs=(tok, pl.BlockSpec((1, G, N, N), sout_map)),
        scratch_shapes=[pltpu.VMEM((G, N, N), f32)],
        compiler_params=_params(("parallel", "arbitrary")),
        name="rwkv_core",
    )(*ops, v16, gl, state)


def _rwkv_out_kernel(o_ref, bonus_ref, g_ref, lnw_ref, lnb_ref, w_ref, res_ref, out_ref, y_ref):
    j = pl.program_id(1)

    @pl.when(j == 0)
    def _():
        o = o_ref[...]
        mu = _head_sum(o) * (1.0 / RWKV_HS)
        d = o - mu
        var = _head_sum(d * d) * (1.0 / RWKV_HS)
        on = d * lax.rsqrt(var + RWKV_GN_EPS) * lnw_ref[...] + lnb_ref[...]
        y_ref[...] = ((on + bonus_ref[...]) * g_ref[...]).astype(bf16)

    out_ref[...] = res_ref[...] + jnp.dot(y_ref[...], w_ref[...], preferred_element_type=f32)


def _rwkv_out(o, bonus, g, ln_w, ln_b, w, res):
    T, D = o.shape
    N = w.shape[1]
    tm = _pick(T, (256, 128, 64))
    tn = _pick(N, (1024, 512, 256))
    tspec = pl.BlockSpec((tm, D), lambda i, j: (i, 0))
    vspec = pl.BlockSpec((1, D), lambda i, j: (0, 0))
    return pl.pallas_call(
        _rwkv_out_kernel,
        out_shape=jax.ShapeDtypeStruct((T, N), f32),
        grid=(T // tm, N // tn),
        in_specs=[tspec, tspec, tspec, vspec, vspec,
                  pl.BlockSpec((D, tn), lambda i, j: (0, j)),
                  pl.BlockSpec((tm, tn), lambda i, j: (i, j))],
        out_specs=pl.BlockSpec((tm, tn), lambda i, j: (i, j)),
        scratch_shapes=[pltpu.VMEM((tm, D), bf16)],
        compiler_params=_params(("parallel", "arbitrary")),
        name="rwkv_out",
    )(o, bonus, g, ln_w, ln_b, w, res)


def _positions(n_prompt, n_b, n_s):
    return jnp.concatenate([jnp.arange(n_prompt), jnp.tile(PAST_LEN + jnp.arange(n_s), n_b)]).astype(f32)


def _attn_tables(pos):
    half = ROT_DIM // 2
    inv = ROPE_THETA ** (-jnp.arange(half, dtype=f32) / half)
    ang = pos[:, None] * inv[None, :]
    cos, sin = jnp.cos(ang), jnp.sin(ang)
    T = pos.shape[0]
    zeros = jnp.zeros((T, half), f32)
    rest = HEAD_DIM - ROT_DIM
    c = jnp.concatenate([cos, cos, jnp.ones((T, rest), f32)], axis=1)
    s1 = jnp.concatenate([zeros, sin, jnp.zeros((T, rest), f32)], axis=1)
    s2 = jnp.concatenate([-sin, zeros, jnp.zeros((T, rest), f32)], axis=1)
    rep = 256 // HEAD_DIM
    return tuple(jnp.tile(t, (1, rep)) for t in (c, s1, s2))


def _ret_tables(pos):
    half = RET_DK // 2
    inv = RET_THETA ** (-jnp.arange(half, dtype=f32) / half)
    ang = pos[:, None] * inv[None, :]
    return jnp.cos(ang), jnp.sin(ang)


def _pad_cols(w, n):
    return jnp.pad(w, ((0, 0), (0, n - w.shape[1])))


def _pad_rows(w, n):
    return jnp.pad(w, ((0, n - w.shape[0]), (0, 0)))


def kernel(x_prompt, x_sample, cache_attn_k, cache_attn_v, state_ret, state_rwkv, state_rwkv_shift,
           norm_mix, norm_mlp, norm_final,
           attn_w_qkv, attn_sinks, attn_w_o,
           ret_w_in, ret_gn_w, ret_w_o,
           rwkv_mu, rwkv_w_rkv, rwkv_w_o, rwkv_w0, rwkv_w1, rwkv_w2, rwkv_a0, rwkv_a1, rwkv_a2,
           rwkv_g1, rwkv_g2, rwkv_k_k, rwkv_k_a, rwkv_r_k, rwkv_ln_w, rwkv_ln_b,
           mlp_w_up, mlp_w_down):
    bp, tp, D = x_prompt.shape
    nb, ns, _ = x_sample.shape
    assert bp == 1 and ns == CHUNK and tp % CHUNK == 0
    depth = norm_mix.shape[0]
    npc = tp // CHUNK
    nchunks = npc + nb
    T = tp + nb * ns
    qd = ATTN_HEADS * HEAD_DIM
    kvd = ATTN_KV_HEADS * HEAD_DIM

    x = jnp.concatenate([x_prompt.reshape(tp, D), x_sample.reshape(nb * ns, D)], axis=0)
    pos = _positions(tp, nb, ns)
    attn_tabs = _attn_tables(pos)
    ret_tabs = _ret_tables(pos)
    lg = jnp.log1p(-jnp.exp2(-5.0 - jnp.arange(RET_HEADS, dtype=f32)))
    tm_proj = _pick(T, (512, 256, 128, 64))

    kp_l, vp_l, ks_l, vs_l, rp_l, rs_l, wp_l, ws_l, shp_l, shs_l = ([] for _ in range(10))
    for i in range(depth):
        j, kind = divmod(i, 3)
        g_mix = norm_mix[i][None, :]
        if kind == 0:
            modes = [["q"] * 5, ["q"] * 3 + ["k", "plain"]]
            qkv = _norm_matmul(x, g_mix, attn_w_qkv[j].astype(bf16), attn_tabs, modes, _epi_attn,
                               f32, tm_proj, "attn_qkv")
            k_new = qkv[:, qd:qd + kvd]
            v_new = qkv[:, qd + kvd:]

            def ext(new, cache):
                samp = jnp.concatenate([cache.reshape(nb, WINDOW, kvd), new[tp:].reshape(nb, ns, kvd)], axis=1)
                return jnp.concatenate([jnp.zeros((WINDOW, kvd), f32), new[:tp],
                                        samp.reshape(nb * (WINDOW + ns), kvd)], axis=0), samp

            kext, k_samp = ext(k_new, cache_attn_k[j])
            vext, v_samp = ext(v_new, cache_attn_v[j])
            o = _attention(qkv, kext, vext, attn_sinks[j], npc, nchunks)
            x = _matmul_residual(o, attn_w_o[j].astype(bf16), x, "attn_out")
            kp_l.append(k_new[tp - WINDOW:tp].reshape(1, WINDOW, ATTN_KV_HEADS, HEAD_DIM))
            vp_l.append(v_new[tp - WINDOW:tp].reshape(1, WINDOW, ATTN_KV_HEADS, HEAD_DIM))
            ks_l.append(k_samp[:, -WINDOW:].reshape(nb, WINDOW, ATTN_KV_HEADS, HEAD_DIM))
            vs_l.append(v_samp[:, -WINDOW:].reshape(nb, WINDOW, ATTN_KV_HEADS, HEAD_DIM))
        elif kind == 1:
            nq = RET_HEADS * RET_DK // 1024
            nv = RET_HEADS * RET_DV // 1024
            modes = [["q"] * 4] * nq + [["k"] * 4] * nq + [["plain"] * 4] * (2 * nv)
            proj = _norm_matmul(x, g_mix, ret_w_in[j].astype(bf16), ret_tabs, modes, _epi_ret,
                                bf16, tm_proj, "ret_proj")
            y, s_all = _retention(proj, state_ret[j], ret_gn_w[j][None, :], lg, npc, nchunks)
            x = _matmul_residual(y, ret_w_o[j].astype(bf16), x, "ret_out")
            rp_l.append(s_all[:1])
            rs_l.append(s_all[1:])
        else:
            h = _rmsnorm(x, g_mix)
            hp3 = h[:tp]
            hs3 = h[tp:].reshape(nb, ns, D)
            hprev = jnp.concatenate(
                [jnp.zeros((1, D), f32), hp3[:-1],
                 jnp.concatenate([state_rwkv_shift[j][:, None, :], hs3[:, :-1]], axis=1).reshape(nb * ns, D)],
                axis=0)
            P = LORA_PAD
            tn = 512
            wcat = jnp.concatenate(
                [rwkv_w_rkv[j][0], rwkv_w_rkv[j][1], rwkv_w_rkv[j][2],
                 _pad_cols(rwkv_w1[j], P), _pad_cols(rwkv_a1[j], P), _pad_cols(rwkv_g1[j], P)],
                axis=1).astype(bf16)
            tiles_per_d = D // tn
            mu = rwkv_mu[j]
            mu_tiles = jnp.concatenate(
                [jnp.repeat(mu[jnp.array([0, 2, 3])], tiles_per_d, axis=0), mu[jnp.array([1, 4, 5])]],
                axis=0)[:, None, :]
            proj = _lerp_matmul(h, hprev, mu_tiles, wcat, tn, tiles_per_d, 3 * tiles_per_d)
            prep = _rwkv_prep(
                proj, 3 * D // (3 * P),
                _pad_rows(rwkv_w2[j], P).astype(bf16), _pad_rows(rwkv_a2[j], P).astype(bf16),
                _pad_rows(rwkv_g2[j], P).astype(bf16), rwkv_w0[j][None, :], rwkv_a0[j][None, :],
                rwkv_k_k[j][None, :], rwkv_k_a[j][None, :], rwkv_r_k[j].reshape(1, D))
            ops, v16, gl, g, bonus = prep[:6], prep[6], prep[7], prep[8], prep[9]
            o, s_all = _rwkv_core(ops, v16, gl, state_rwkv[j], npc, nchunks)
            x = _rwkv_out(o, bonus, g, rwkv_ln_w[j][None, :], rwkv_ln_b[j][None, :],
                          rwkv_w_o[j].astype(bf16), x)
            wp_l.append(s_all[:1])
            ws_l.append(s_all[1:])
            shp_l.append(h[tp - 1:tp])
            shs_l.append(hs3[:, -1])
        x = _mlp(x, norm_mlp[i][None, :], mlp_w_up[i].astype(bf16), mlp_w_down[i].astype(bf16),
                 norm_final[None, :], i == depth - 1, "mlp")

    y_prompt = x[:tp].reshape(1, tp, D)
    y_sample = x[tp:].reshape(nb, ns, D)
    return (y_prompt, y_sample,
            jnp.stack(kp_l), jnp.stack(vp_l), jnp.stack(ks_l), jnp.stack(vs_l),
            jnp.stack(rp_l), jnp.stack(rs_l),
            jnp.stack(wp_l), jnp.stack(ws_l), jnp.stack(shp_l), jnp.stack(shs_l))
```

```python
import functools

import jax
import jax.numpy as jnp
from jax import lax
from jax.experimental import pallas as pl
from jax.experimental.pallas import tpu as pltpu

f32 = jnp.float32
bf16 = jnp.bfloat16

CHUNK = 64
NORM_EPS = 1e-5
PAST_LEN = 4096

ATTN_HEADS = 32
ATTN_KV_HEADS = 4
ATTN_GROUP = ATTN_HEADS // ATTN_KV_HEADS
HEAD_DIM = 64
WINDOW = 128
ROT_DIM = HEAD_DIM // 4
ROPE_THETA = 500000.0

RET_HEADS = 8
RET_DK = 256
RET_DV = 512
RET_THETA = 10000.0
RET_GN_EPS = 1e-5

RWKV_HS = 64
RWKV_GN_EPS = 64e-5
LORA_PAD = 512

VMEM_LIMIT = 52 * 1024 * 1024

NT_DIMS = (((1,), (1,)), ((), ()))


def _pick(n, cands):
    for c in cands:
        if n % c == 0:
            return c
    raise ValueError(f"no tile for {n} in {cands}")


def _params(sem):
    return pltpu.CompilerParams(dimension_semantics=sem, vmem_limit_bytes=VMEM_LIMIT)


def _rms(x, g):
    return x * lax.rsqrt(jnp.mean(x * x, axis=-1, keepdims=True) + NORM_EPS) * g


def _epi_attn(blk, mode, tabs):
    if mode == "plain":
        return blk
    c_ref, s1_ref, s2_ref = tabs
    out = (blk * c_ref[...] + pltpu.roll(blk, 8, 1) * s1_ref[...]
           + pltpu.roll(blk, 256 - 8, 1) * s2_ref[...])
    return out * (HEAD_DIM ** -0.5) if mode == "q" else out


def _epi_ret(blk, mode, tabs):
    if mode == "plain":
        return blk
    cos_ref, sin_ref = tabs
    c, s = cos_ref[...], sin_ref[...]
    x1, x2 = blk[:, :128], blk[:, 128:]
    out = jnp.concatenate([x1 * c - x2 * s, x2 * c + x1 * s], axis=1)
    return out * (RET_DK ** -0.5) if mode == "k" else out


def _normmm_kernel(*refs, modes, ntab, epi):
    x_ref, g_ref, w_ref = refs[:3]
    tabs = refs[3:3 + ntab]
    o_ref, xn_ref = refs[3 + ntab], refs[4 + ntab]
    j = pl.program_id(1)

    @pl.when(j == 0)
    def _():
        xn_ref[...] = _rms(x_ref[...], g_ref[...]).astype(bf16)

    acc = jnp.dot(xn_ref[...], w_ref[...], preferred_element_type=f32)
    groups = {}
    for jj, tile_modes in enumerate(modes):
        groups.setdefault(tuple(tile_modes), []).append(jj)
    for tile_modes, jjs in groups.items():
        cond = functools.reduce(jnp.logical_or, [j == jj for jj in jjs])

        @pl.when(cond)
        def _(tile_modes=tile_modes):
            for s, mode in enumerate(tile_modes):
                blk = acc[:, s * 256:(s + 1) * 256]
                o_ref[:, s * 256:(s + 1) * 256] = epi(blk, mode, tabs).astype(o_ref.dtype)


def _norm_matmul(x, g, w, tabs, modes, epi, out_dtype, tm, name):
    T, D = x.shape
    N = w.shape[1]
    ntile = len(modes)
    tn = N // ntile
    assert tn * ntile == N and tn == 256 * len(modes[0])
    tab_specs = [pl.BlockSpec((tm, t.shape[1]), lambda i, j: (i, 0)) for t in tabs]
    return pl.pallas_call(
        functools.partial(_normmm_kernel, modes=modes, ntab=len(tabs), epi=epi),
        out_shape=jax.ShapeDtypeStruct((T, N), out_dtype),
        grid=(T // tm, ntile),
        in_specs=[pl.BlockSpec((tm, D), lambda i, j: (i, 0)),
                  pl.BlockSpec((1, D), lambda i, j: (0, 0)),
                  pl.BlockSpec((D, tn), lambda i, j: (0, j))] + tab_specs,
        out_specs=pl.BlockSpec((tm, tn), lambda i, j: (i, j)),
        scratch_shapes=[pltpu.VMEM((tm, D), bf16)],
        compiler_params=_params(("parallel", "arbitrary")),
        name=name,
    )(x, g, w, *tabs)


def _mmres_kernel(a_ref, w_ref, r_ref, o_ref):
    o_ref[...] = r_ref[...] + jnp.dot(a_ref[...], w_ref[...], preferred_element_type=f32)


def _matmul_residual(a, w, res, name):
    T, K = a.shape
    N = w.shape[1]
    tm = _pick(T, (512, 256, 128, 64))
    tn = _pick(N, (1024, 512, 256))
    return pl.pallas_call(
        _mmres_kernel,
        out_shape=jax.ShapeDtypeStruct((T, N), f32),
        grid=(T // tm, N // tn),
        in_specs=[pl.BlockSpec((tm, K), lambda i, j: (i, 0)),
                  pl.BlockSpec((K, tn), lambda i, j: (0, j)),
                  pl.BlockSpec((tm, tn), lambda i, j: (i, j))],
        out_specs=pl.BlockSpec((tm, tn), lambda i, j: (i, j)),
        compiler_params=_params(("parallel", "parallel")),
        name=name,
    )(a, w, res)


def _mmres2_kernel(ap_ref, as_ref, w_ref, r_ref, o_ref, *, np_blocks):
    i = pl.program_id(0)

    @pl.when(i < np_blocks)
    def _():
        o_ref[...] = r_ref[...] + jnp.dot(ap_ref[...], w_ref[...], preferred_element_type=f32)

    @pl.when(i >= np_blocks)
    def _():
        o_ref[...] = r_ref[...] + jnp.dot(as_ref[...], w_ref[...], preferred_element_type=f32)


def _matmul_residual2(a_p, a_s, w, res, name):
    tp, K = a_p.shape
    ts = a_s.shape[0]
    N = w.shape[1]
    tm = _pick(ts, (512, 256, 128, 64))
    assert tp % tm == 0
    np_blocks = tp // tm
    tn = _pick(N, (1024, 512, 256))
    return pl.pallas_call(
        functools.partial(_mmres2_kernel, np_blocks=np_blocks),
        out_shape=jax.ShapeDtypeStruct((tp + ts, N), f32),
        grid=((tp + ts) // tm, N // tn),
        in_specs=[pl.BlockSpec((tm, K), lambda i, j: (jnp.minimum(i, np_blocks - 1), 0)),
                  pl.BlockSpec((tm, K), lambda i, j: (jnp.maximum(i - np_blocks, 0), 0)),
                  pl.BlockSpec((K, tn), lambda i, j: (0, j)),
                  pl.BlockSpec((tm, tn), lambda i, j: (i, j))],
        out_specs=pl.BlockSpec((tm, tn), lambda i, j: (i, j)),
        compiler_params=_params(("parallel", "parallel")),
        name=name,
    )(a_p, a_s, w, res)


def _mlp_kernel(x_ref, g_ref, wu_ref, wd_ref, gf_ref, o_ref, xn_ref, *, nk, final):
    k = pl.program_id(1)

    @pl.when(k == 0)
    def _():
        x = x_ref[...]
        xn_ref[...] = _rms(x, g_ref[...]).astype(bf16)
        o_ref[...] = x

    h = jnp.dot(xn_ref[...], wu_ref[...], preferred_element_type=f32)
    h = jnp.square(jnp.maximum(h, 0.0)).astype(bf16)
    o_ref[...] += jnp.dot(h, wd_ref[...], preferred_element_type=f32)

    if final:
        @pl.when(k == nk - 1)
        def _():
            o_ref[...] = _rms(o_ref[...], gf_ref[...])


def _mlp(x, g, w_up, w_down, layer, g_final, final, name):
    T, D = x.shape
    F = w_up.shape[2]
    tm = _pick(T, (512, 256, 128, 64))
    tf = _pick(F, (1024, 512))
    nk = F // tf
    return pl.pallas_call(
        functools.partial(_mlp_kernel, nk=nk, final=final),
        out_shape=jax.ShapeDtypeStruct((T, D), f32),
        grid=(T // tm, nk),
        in_specs=[pl.BlockSpec((tm, D), lambda i, k: (i, 0)),
                  pl.BlockSpec((1, D), lambda i, k: (0, 0)),
                  pl.BlockSpec((None, D, tf), lambda i, k: (layer, 0, k)),
                  pl.BlockSpec((None, tf, D), lambda i, k: (layer, k, 0)),
                  pl.BlockSpec((1, D), lambda i, k: (0, 0))],
        out_specs=pl.BlockSpec((tm, D), lambda i, k: (i, 0)),
        scratch_shapes=[pltpu.VMEM((tm, D), bf16)],
        compiler_params=_params(("parallel", "arbitrary")),
        name=name,
    )(x, g, w_up, w_down, g_final)


def _attn_kernel(sb_ref, q_ref, k0_ref, k1_ref, k2_ref, v0_ref, v1_ref, v2_ref, o_ref, *, npc):
    c = pl.program_id(0)
    nkeys = 3 * CHUNK
    ncols = nkeys + CHUNK
    PW = 2 * HEAD_DIM
    pairs = ATTN_GROUP // 2
    col_blk = lax.broadcasted_iota(jnp.int32, (1, ncols), 1) // CHUNK
    valid = jnp.logical_or(jnp.logical_or(c >= npc, c + col_blk >= 2), col_blk >= 3)
    bias = jnp.where(valid, 0.0, -jnp.inf).astype(f32)
    zpad = jnp.zeros((CHUNK, ATTN_KV_HEADS * PW), bf16)
    kcat = jnp.concatenate([k0_ref[...], k1_ref[...], k2_ref[...], zpad], axis=0)
    vcat = jnp.concatenate([v0_ref[...], v1_ref[...], v2_ref[...], zpad], axis=0)
    lo_half = lax.broadcasted_iota(jnp.int32, (1, PW), 1) < HEAD_DIM
    ones = jnp.ones((ncols, PW), bf16)
    kvs = range(ATTN_KV_HEADS)

    def stacked_q(kv):
        parts = []
        for p in range(pairs):
            col = (kv * pairs + p) * PW
            qp = q_ref[:, col:col + PW]
            parts += [jnp.where(lo_half, qp, 0.0), jnp.where(lo_half, 0.0, qp)]
        return jnp.concatenate(parts, axis=0).astype(bf16)

    s = [lax.dot_general(stacked_q(kv), kcat[:, kv * PW:(kv + 1) * PW], NT_DIMS,
                         preferred_element_type=f32) + sb_ref[kv] + bias for kv in kvs]
    p = [jnp.exp(s[kv] - jnp.max(s[kv], axis=-1, keepdims=True)).astype(bf16) for kv in kvs]
    oa = [jnp.dot(p[kv], jnp.concatenate([vcat[:, kv * PW:(kv + 1) * PW], ones], axis=1),
                  preferred_element_type=f32) for kv in kvs]
    for kv in kvs:
        on = oa[kv][:, :PW] / oa[kv][:, PW:]
        for pi in range(pairs):
            r0 = 2 * pi * CHUNK
            blk = jnp.where(lo_half, on[r0:r0 + CHUNK], on[r0 + CHUNK:r0 + 2 * CHUNK])
            col = (kv * pairs + pi) * PW
            o_ref[:, col:col + PW] = blk.astype(o_ref.dtype)


def _attention(qkv, kext, vext, sinks, npc, nchunks):
    T = qkv.shape[0]
    qd = ATTN_HEADS * HEAD_DIM
    kvd = kext.shape[1]

    def kv_spec(j):
        def imap(c):
            return (jnp.where(c < npc, c, npc + 2 + 3 * (c - npc)) + j, 0)
        return pl.BlockSpec((CHUNK, kvd), imap)

    nkeys = 3 * CHUNK
    col = jnp.arange(nkeys + CHUNK)[None, None, :]
    sink_rows = jnp.repeat(sinks.astype(f32).reshape(ATTN_KV_HEADS, ATTN_GROUP), CHUNK, axis=1)[:, :, None]
    sink_bias = jnp.where(col < nkeys, 0.0, jnp.where(col == nkeys, sink_rows, -jnp.inf)).astype(f32)

    return pl.pallas_call(
        functools.partial(_attn_kernel, npc=npc),
        out_shape=jax.ShapeDtypeStruct((T, qd), bf16),
        grid=(nchunks,),
        in_specs=[pl.BlockSpec(sink_bias.shape, lambda c: (0, 0, 0)),
                  pl.BlockSpec((CHUNK, qd), lambda c: (c, 0))]
                 + [kv_spec(j) for j in range(3)] + [kv_spec(j) for j in range(3)],
        out_specs=pl.BlockSpec((CHUNK, qd), lambda c: (c, 0)),
        compiler_params=_params(("parallel",)),
        name="attn_core",
    )(sink_bias, qkv, kext, kext, kext, vext, vext, vext)


def _ret_kernel(lg_ref, q_ref, k_ref, v_ref, g_ref, s0_ref, gnw_ref, y_ref, sout_ref, S_ref, *, from_state):
    h = pl.program_id(0)
    c = pl.program_id(1)
    L = q_ref.shape[0]
    lg = lg_ref[h]

    if from_state:
        S_ref[...] = s0_ref[0, 0]
    else:
        @pl.when(c == 0)
        def _():
            S_ref[...] = jnp.zeros_like(S_ref)

    q = q_ref[...]
    k = k_ref[...]
    v = v_ref[...]
    row = lax.broadcasted_iota(jnp.int32, (L, L), 0)
    col = lax.broadcasted_iota(jnp.int32, (L, L), 1)
    diff = (row - col).astype(f32)
    decay = jnp.where(diff >= 0, jnp.exp(lg * jnp.maximum(diff, 0.0)), 0.0)
    idx = lax.broadcasted_iota(jnp.int32, (L, 1), 0).astype(f32)
    xi = jnp.exp(lg * (idx + 1.0))
    zeta = jnp.exp(lg * (L - 1.0 - idx))

    S = S_ref[...]
    scores = lax.dot_general(q, k, NT_DIMS, preferred_element_type=f32) * decay
    o = (jnp.dot(scores.astype(bf16), v, preferred_element_type=f32)
         + jnp.dot((q.astype(f32) * xi).astype(bf16), S.astype(bf16), preferred_element_type=f32))
    kz_t = (k.astype(f32) * zeta).T.astype(bf16)
    S_new = jnp.exp(lg * L) * S + jnp.dot(kz_t, v, preferred_element_type=f32)
    S_ref[...] = S_new
    sout_ref[0, 0] = S_new

    mu = jnp.mean(o, axis=-1, keepdims=True)
    d = o - mu
    var = jnp.mean(d * d, axis=-1, keepdims=True)
    on = d * lax.rsqrt(var + RET_GN_EPS) * gnw_ref[...]
    y_ref[...] = (jax.nn.silu(g_ref[...].astype(f32)) * on).astype(y_ref.dtype)


def _retention(proj, state, gn_w, lg, row0, nrows, L, from_state, name):
    nsteps = nrows // L
    b0 = row0 // L
    assert nsteps * L == nrows and b0 * L == row0
    nseq = nsteps if from_state else 1
    qb = RET_HEADS
    vb = 2 * RET_HEADS * RET_DK // RET_DV
    gb = vb + RET_HEADS
    state_map = (lambda h, c: (c, h, 0, 0)) if from_state else (lambda h, c: (0, h, 0, 0))

    return pl.pallas_call(
        functools.partial(_ret_kernel, from_state=from_state),
        out_shape=(jax.ShapeDtypeStruct((nrows, RET_HEADS * RET_DV), bf16),
                   jax.ShapeDtypeStruct((nseq, RET_HEADS, RET_DK, RET_DV), f32)),
        grid=(RET_HEADS, nsteps),
        in_specs=[pl.BlockSpec(memory_space=pltpu.SMEM),
                  pl.BlockSpec((L, RET_DK), lambda h, c: (b0 + c, h)),
                  pl.BlockSpec((L, RET_DK), lambda h, c: (b0 + c, qb + h)),
                  pl.BlockSpec((L, RET_DV), lambda h, c: (b0 + c, vb + h)),
                  pl.BlockSpec((L, RET_DV), lambda h, c: (b0 + c, gb + h)),
                  pl.BlockSpec((1, 1, RET_DK, RET_DV), state_map),
                  pl.BlockSpec((1, RET_DV), lambda h, c: (0, h))],
        out_specs=(pl.BlockSpec((L, RET_DV), lambda h, c: (c, h)),
                   pl.BlockSpec((1, 1, RET_DK, RET_DV), state_map)),
        scratch_shapes=[pltpu.VMEM((RET_DK, RET_DV), f32)],
        compiler_params=_params(("parallel", "arbitrary")),
        name=name,
    )(lg, proj, proj, proj, proj, state, gn_w)


SUBLANES = 8


def _lerpmm_kernel(x_ref, xprev_ref, g_ref, start_ref, mu_ref, w_ref, o_ref, hlast_ref,
                   h_ref, xx_ref, l_ref, *, tiles_per_d, n_big, npc):
    i = pl.program_id(0)
    j = pl.program_id(1)
    tm = x_ref.shape[0]
    cpb = tm // CHUNK

    @pl.when(j == 0)
    def _():
        g = g_ref[...]
        h = _rms(x_ref[...], g)
        h_ref[...] = h
        xx_ref[...] = pltpu.roll(h, 1, 0) - h
        prev = _rms(xprev_ref[...], g)[SUBLANES - 1:SUBLANES, :]
        xx_ref[0:1, :] = prev - h[0:1, :]
        for ci in range(cpb):
            gc = i * cpb + ci
            r0 = ci * CHUNK

            @pl.when(jnp.logical_or(gc == 0, gc >= npc))
            def _(ci=ci, r0=r0):
                xx_ref[r0:r0 + 1, :] = start_ref[ci] - h_ref[r0:r0 + 1, :]

            hlast_ref[ci] = h[r0 + CHUNK - 1:r0 + CHUNK, :]

    @pl.when(jnp.logical_or(j % tiles_per_d == 0, j >= n_big))
    def _():
        l_ref[...] = (h_ref[...] + xx_ref[...] * mu_ref[0]).astype(bf16)

    o_ref[...] = jnp.dot(l_ref[...], w_ref[...], preferred_element_type=f32)


def _lerp_matmul(x, g, starts, mu_tiles, wcat, tn, tiles_per_d, n_big, npc):
    T, D = x.shape
    N = wcat.shape[1]
    tm = _pick(T, (512, 256, 128, 64))
    cpb = tm // CHUNK
    return pl.pallas_call(
        functools.partial(_lerpmm_kernel, tiles_per_d=tiles_per_d, n_big=n_big, npc=npc),
        out_shape=(jax.ShapeDtypeStruct((T, N), f32),
                   jax.ShapeDtypeStruct((T // CHUNK, 1, D), f32)),
        grid=(T // tm, N // tn),
        in_specs=[pl.BlockSpec((tm, D), lambda i, j: (i, 0)),
                  pl.BlockSpec((SUBLANES, D), lambda i, j: (jnp.maximum(i * (tm // SUBLANES) - 1, 0), 0)),
                  pl.BlockSpec((1, D), lambda i, j: (0, 0)),
                  pl.BlockSpec((cpb, 1, D), lambda i, j: (i, 0, 0)),
                  pl.BlockSpec((1, 1, D), lambda i, j: (j, 0, 0)),
                  pl.BlockSpec((D, tn), lambda i, j: (0, j))],
        out_specs=(pl.BlockSpec((tm, tn), lambda i, j: (i, j)),
                   pl.BlockSpec((cpb, 1, D), lambda i, j: (i, 0, 0))),
        scratch_shapes=[pltpu.VMEM((tm, D), f32), pltpu.VMEM((tm, D), f32), pltpu.VMEM((tm, D), bf16)],
        compiler_params=_params(("parallel", "arbitrary")),
        name="rwkv_proj",
    )(x, x, g, starts, mu_tiles, wcat)


def _softplus(z):
    return jnp.maximum(z, 0.0) + jnp.log(1.0 + jnp.exp(-jnp.abs(z)))


SEG_TILE = 256


def _head_sum(x):
    rows, D = x.shape
    r = lax.broadcasted_iota(jnp.int32, (SEG_TILE, SEG_TILE), 0) // RWKV_HS
    c = lax.broadcasted_iota(jnp.int32, (SEG_TILE, SEG_TILE), 1) // RWKV_HS
    ones = (r == c).astype(bf16)
    out = []
    for j in range(D // SEG_TILE):
        blk = x[:, j * SEG_TILE:(j + 1) * SEG_TILE]
        hi = blk.astype(bf16)
        lo = (blk - hi.astype(f32)).astype(bf16)
        out.append(jnp.dot(hi, ones, preferred_element_type=f32)
                   + jnp.dot(lo, ones, preferred_element_type=f32))
    return jnp.concatenate(out, axis=1)


def _rwkv_prep_kernel(p_ref, r_ref, k_ref, v_ref, w2_ref, a2_ref, g2_ref, w0_ref, a0_ref,
                      kk_ref, ka_ref, rk_ref,
                      at_ref, rt_ref, bt_ref, kt_ref, bh_ref, kh_ref, v16_ref, gl_ref, g_ref, bonus_ref):
    P = LORA_PAD
    tm = r_ref.shape[0]
    pw = jnp.tanh(p_ref[:, :P]).astype(bf16)
    pa = p_ref[:, P:2 * P].astype(bf16)
    pg = jax.nn.sigmoid(p_ref[:, 2 * P:]).astype(bf16)
    wl = w0_ref[...] + jnp.dot(pw, w2_ref[...], preferred_element_type=f32)
    lw = -jnp.exp(-_softplus(-wl) - 0.5)
    a = jax.nn.sigmoid(a0_ref[...] + jnp.dot(pa, a2_ref[...], preferred_element_type=f32))
    g_ref[...] = jnp.dot(pg, g2_ref[...], preferred_element_type=f32)

    row = lax.broadcasted_iota(jnp.int32, (tm, tm), 0)
    col = lax.broadcasted_iota(jnp.int32, (tm, tm), 1)
    same = (row // CHUNK) == (col // CHUNK)
    tri = jnp.logical_and(same, row >= col).astype(f32)
    cum = jnp.dot(tri, lw, preferred_element_type=f32, precision=lax.Precision.HIGHEST)
    tot = jnp.dot(same.astype(f32), lw, preferred_element_type=f32, precision=lax.Precision.HIGHEST)

    r, k, v = r_ref[...], k_ref[...], v_ref[...]
    kk = k * kk_ref[...]
    kk = kk / jnp.maximum(jnp.sqrt(_head_sum(kk * kk)), 1e-12)
    kmod = k * (1.0 + (a - 1.0) * ka_ref[...])
    beta = kk * a
    e_neg = jnp.exp(-cum)
    e_end = jnp.exp(tot - cum)
    at_ref[...] = (-kk * jnp.exp(cum - lw)).astype(bf16)
    rt_ref[...] = (r * jnp.exp(cum)).astype(bf16)
    bt_ref[...] = (beta * e_neg).astype(bf16)
    kt_ref[...] = (kmod * e_neg).astype(bf16)
    bh_ref[...] = (beta * e_end).astype(bf16)
    kh_ref[...] = (kmod * e_end).astype(bf16)
    v16_ref[...] = v.astype(bf16)
    bonus_ref[...] = _head_sum(r * kmod * rk_ref[...]) * v
    for ci in range(tm // CHUNK):
        gl_ref[ci] = jnp.exp(tot[ci * CHUNK:ci * CHUNK + 1, :])


def _rwkv_prep(proj, lora_block, w2p, a2p, g2p, w0, a0, k_k, k_a, r_k):
    T = proj.shape[0]
    D = w2p.shape[1]
    P = LORA_PAD
    tm = _pick(T, (128, 64))
    wspec = pl.BlockSpec((P, D), lambda i: (0, 0))
    vspec = pl.BlockSpec((1, D), lambda i: (0, 0))
    ospec = pl.BlockSpec((tm, D), lambda i: (i, 0))
    o16 = jax.ShapeDtypeStruct((T, D), bf16)
    o32 = jax.ShapeDtypeStruct((T, D), f32)
    return pl.pallas_call(
        _rwkv_prep_kernel,
        out_shape=(o16,) * 7 + (jax.ShapeDtypeStruct((T // CHUNK, 1, D), f32), o32, o32),
        grid=(T // tm,),
        in_specs=[pl.BlockSpec((tm, 3 * P), lambda i: (i, lora_block)),
                  pl.BlockSpec((tm, D), lambda i: (i, 0)),
                  pl.BlockSpec((tm, D), lambda i: (i, 1)),
                  pl.BlockSpec((tm, D), lambda i: (i, 2)),
                  wspec, wspec, wspec, vspec, vspec, vspec, vspec, vspec],
        out_specs=(ospec,) * 7 + (pl.BlockSpec((tm // CHUNK, 1, D), lambda i: (i, 0, 0)), ospec, ospec),
        compiler_params=_params(("parallel",)),
        name="rwkv_prep",
    )(proj, proj, proj, proj, w2p, a2p, g2p, w0, a0, k_k, k_a, r_k)


def _rwkv_heads_chunk(heads):
    L, N = heads[0][6].shape
    nh = len(heads)
    dot = functools.partial(jnp.dot, preferred_element_type=f32)
    row = lax.broadcasted_iota(jnp.int32, (L, L), 0)
    col = lax.broadcasted_iota(jnp.int32, (L, L), 1)
    incl = row >= col
    strict = row > col
    lhs = [jnp.concatenate([h[0], h[1]], axis=0) for h in heads]
    rhs = [jnp.concatenate([h[2], h[3]], axis=0) for h in heads]
    G = [lax.dot_general(lhs[i], rhs[i], NT_DIMS, preferred_element_type=f32) for i in range(nh)]
    LS = [lax.dot_general(lhs[i], heads[i][7].astype(bf16), NT_DIMS, preferred_element_type=f32)
          for i in range(nh)]
    n = [jnp.where(strict, g[:L, :L], 0.0) for g in G]
    a_ak = [jnp.where(strict, g[:L, L:], 0.0).astype(bf16) for g in G]
    a_r = [jnp.concatenate([jnp.where(incl, g[L:, :L], 0.0), jnp.where(incl, g[L:, L:], 0.0)],
                           axis=1).astype(bf16) for g in G]
    x = [LS[i][:L] + dot(a_ak[i], heads[i][6]) for i in range(nh)]
    steps = max(1, (L - 1).bit_length())
    for s in range(steps):
        if s < steps - 1:
            z = [dot(n[i].astype(bf16), jnp.concatenate([x[i], n[i]], axis=1).astype(bf16))
                 for i in range(nh)]
            x = [x[i] + z[i][:, :N] for i in range(nh)]
            n = [z[i][:, N:] for i in range(nh)]
        else:
            x = [x[i] + dot(n[i].astype(bf16), x[i].astype(bf16)) for i in range(nh)]
    uv = [jnp.concatenate([x[i], heads[i][6].astype(f32)], axis=0) for i in range(nh)]
    o = [LS[i][L:] + dot(a_r[i], uv[i].astype(bf16)) for i in range(nh)]
    S_new = [heads[i][7] * heads[i][8]
             + dot(uv[i].T.astype(bf16), jnp.concatenate([heads[i][4], heads[i][5]], axis=0))
             for i in range(nh)]
    return list(zip(o, S_new))


def _rwkv_core_kernel(at_ref, rt_ref, bt_ref, kt_ref, bh_ref, kh_ref, v_ref, gl_ref, s0_ref,
                      o_ref, sout_ref, S_ref, *, npc, nhead):
    c = pl.program_id(1)
    N = RWKV_HS

    @pl.when(c == 0)
    def _():
        S_ref[...] = jnp.zeros_like(S_ref)

    @pl.when(c >= npc)
    def _():
        S_ref[...] = s0_ref[0]

    gl = gl_ref[0]
    heads = []
    for hh in range(nhead):
        sl = slice(hh * N, (hh + 1) * N)
        heads.append((at_ref[:, sl], rt_ref[:, sl], bt_ref[:, sl], kt_ref[:, sl], bh_ref[:, sl],
                      kh_ref[:, sl], v_ref[:, sl], S_ref[hh], gl[:, sl]))
    res = _rwkv_heads_chunk(heads)
    for hh in range(nhead):
        S_ref[hh] = res[hh][1]
    o_ref[...] = jnp.concatenate([o for o, _ in res], axis=1)

    @pl.when(jnp.logical_or(c == npc - 1, c >= npc))
    def _():
        sout_ref[0] = S_ref[...]


RWKV_HEADS_PER_STEP = 32


def _rwkv_core(ops, v16, gl, state, npc, nchunks):
    T, D = v16.shape
    N = RWKV_HS
    nh = D // N
    nb = state.shape[0]
    G = RWKV_HEADS_PER_STEP
    W = G * N
    tok = pl.BlockSpec((CHUNK, W), lambda p, c: (c, p))

    def sin_map(p, c):
        return (jnp.maximum(c - npc, 0), p, 0, 0)

    def sout_map(p, c):
        return (jnp.maximum(c - npc + 1, 0), p, 0, 0)

    return pl.pallas_call(
        functools.partial(_rwkv_core_kernel, npc=npc, nhead=G),
        out_shape=(jax.ShapeDtypeStruct((T, D), f32),
                   jax.ShapeDtypeStruct((nb + 1, nh, N, N), f32)),
        grid=(nh // G, nchunks),
        in_specs=[tok] * 7 + [pl.BlockSpec((1, 1, W), lambda p, c: (c, 0, p)),
                              pl.BlockSpec((1, G, N, N), sin_map)],
        out_specs=(tok, pl.BlockSpec((1, G, N, N), sout_map)),
        scratch_shapes=[pltpu.VMEM((G, N, N), f32)],
        compiler_params=_params(("parallel", "arbitrary")),
        name="rwkv_core",
    )(*ops, v16, gl, state)


def _rwkv_out_kernel(o_ref, bonus_ref, g_ref, lnw_ref, lnb_ref, w_ref, res_ref, out_ref, y_ref):
    j = pl.program_id(1)

    @pl.when(j == 0)
    def _():
        o = o_ref[...]
        mu = _head_sum(o) * (1.0 / RWKV_HS)
        d = o - mu
        var = _head_sum(d * d) * (1.0 / RWKV_HS)
        on = d * lax.rsqrt(var + RWKV_GN_EPS) * lnw_ref[...] + lnb_ref[...]
        y_ref[...] = ((on + bonus_ref[...]) * g_ref[...]).astype(bf16)

    out_ref[...] = res_ref[...] + jnp.dot(y_ref[...], w_ref[...], preferred_element_type=f32)


def _rwkv_out(o, bonus, g, ln_w, ln_b, w, res):
    T, D = o.shape
    N = w.shape[1]
    tm = _pick(T, (256, 128, 64))
    tn = _pick(N, (1024, 512, 256))
    tspec = pl.BlockSpec((tm, D), lambda i, j: (i, 0))
    vspec = pl.BlockSpec((1, D), lambda i, j: (0, 0))
    return pl.pallas_call(
        _rwkv_out_kernel,
        out_shape=jax.ShapeDtypeStruct((T, N), f32),
        grid=(T // tm, N // tn),
        in_specs=[tspec, tspec, tspec, vspec, vspec,
                  pl.BlockSpec((D, tn), lambda i, j: (0, j)),
                  pl.BlockSpec((tm, tn), lambda i, j: (i, j))],
        out_specs=pl.BlockSpec((tm, tn), lambda i, j: (i, j)),
        scratch_shapes=[pltpu.VMEM((tm, D), bf16)],
        compiler_params=_params(("parallel", "arbitrary")),
        name="rwkv_out",
    )(o, bonus, g, ln_w, ln_b, w, res)


def _positions(n_prompt, n_b, n_s):
    return jnp.concatenate([jnp.arange(n_prompt), jnp.tile(PAST_LEN + jnp.arange(n_s), n_b)]).astype(f32)


def _attn_tables(pos):
    half = ROT_DIM // 2
    inv = ROPE_THETA ** (-jnp.arange(half, dtype=f32) / half)
    ang = pos[:, None] * inv[None, :]
    cos, sin = jnp.cos(ang), jnp.sin(ang)
    T = pos.shape[0]
    zeros = jnp.zeros((T, half), f32)
    rest = HEAD_DIM - ROT_DIM
    c = jnp.concatenate([cos, cos, jnp.ones((T, rest), f32)], axis=1)
    s1 = jnp.concatenate([zeros, sin, jnp.zeros((T, rest), f32)], axis=1)
    s2 = jnp.concatenate([-sin, zeros, jnp.zeros((T, rest), f32)], axis=1)
    rep = 256 // HEAD_DIM
    return tuple(jnp.tile(t, (1, rep)) for t in (c, s1, s2))


def _ret_tables(pos):
    half = RET_DK // 2
    inv = RET_THETA ** (-jnp.arange(half, dtype=f32) / half)
    ang = pos[:, None] * inv[None, :]
    return jnp.cos(ang), jnp.sin(ang)


def _pad_cols(w, n):
    return jnp.pad(w, ((0, 0), (0, n - w.shape[1])))


def _pad_rows(w, n):
    return jnp.pad(w, ((0, n - w.shape[0]), (0, 0)))


def kernel(x_prompt, x_sample, cache_attn_k, cache_attn_v, state_ret, state_rwkv, state_rwkv_shift,
           norm_mix, norm_mlp, norm_final,
           attn_w_qkv, attn_sinks, attn_w_o,
           ret_w_in, ret_gn_w, ret_w_o,
           rwkv_mu, rwkv_w_rkv, rwkv_w_o, rwkv_w0, rwkv_w1, rwkv_w2, rwkv_a0, rwkv_a1, rwkv_a2,
           rwkv_g1, rwkv_g2, rwkv_k_k, rwkv_k_a, rwkv_r_k, rwkv_ln_w, rwkv_ln_b,
           mlp_w_up, mlp_w_down):
    bp, tp, D = x_prompt.shape
    nb, ns, _ = x_sample.shape
    assert bp == 1 and ns == CHUNK and tp % CHUNK == 0
    depth = norm_mix.shape[0]
    npc = tp // CHUNK
    nchunks = npc + nb
    T = tp + nb * ns
    qd = ATTN_HEADS * HEAD_DIM
    kvd = ATTN_KV_HEADS * HEAD_DIM

    x = jnp.concatenate([x_prompt.reshape(tp, D), x_sample.reshape(nb * ns, D)], axis=0)
    pos = _positions(tp, nb, ns)
    attn_tabs = _attn_tables(pos)
    ret_tabs = _ret_tables(pos)
    lg = jnp.log1p(-jnp.exp2(-5.0 - jnp.arange(RET_HEADS, dtype=f32)))
    tm_proj = _pick(T, (512, 256, 128, 64))
    tm_ret = _pick(T, (1024, 512, 256, 128, 64))
    w_up16 = mlp_w_up.astype(bf16)
    w_down16 = mlp_w_down.astype(bf16)

    kp_l, vp_l, ks_l, vs_l, rp_l, rs_l, wp_l, ws_l, shp_l, shs_l = ([] for _ in range(10))
    for i in range(depth):
        j, kind = divmod(i, 3)
        g_mix = norm_mix[i][None, :]
        if kind == 0:
            modes = [["q"] * 5, ["q"] * 3 + ["k", "plain"]]
            qkv = _norm_matmul(x, g_mix, attn_w_qkv[j].astype(bf16), attn_tabs, modes, _epi_attn,
                               f32, tm_proj, "attn_qkv")
            k_new = qkv[:, qd:qd + kvd]
            v_new = qkv[:, qd + kvd:]

            def ext(new, cache):
                samp = jnp.concatenate([cache.reshape(nb, WINDOW, kvd), new[tp:].reshape(nb, ns, kvd)], axis=1)
                rows = jnp.concatenate([jnp.zeros((WINDOW, kvd), f32), new[:tp],
                                        samp.reshape(nb * (WINDOW + ns), kvd)], axis=0)
                r4 = rows.astype(bf16).reshape(rows.shape[0], ATTN_KV_HEADS, 1, HEAD_DIM)
                dup = jnp.broadcast_to(r4, (rows.shape[0], ATTN_KV_HEADS, 2, HEAD_DIM))
                return dup.reshape(rows.shape[0], 2 * kvd), samp

            kext, k_samp = ext(k_new, cache_attn_k[j])
            vext, v_samp = ext(v_new, cache_attn_v[j])
            o = _attention(qkv, kext, vext, attn_sinks[j], npc, nchunks)
            x = _matmul_residual(o, attn_w_o[j].astype(bf16), x, "attn_out")
            kp_l.append(k_new[tp - WINDOW:tp].reshape(1, WINDOW, ATTN_KV_HEADS, HEAD_DIM))
            vp_l.append(v_new[tp - WINDOW:tp].reshape(1, WINDOW, ATTN_KV_HEADS, HEAD_DIM))
            ks_l.append(k_samp[:, -WINDOW:].reshape(nb, WINDOW, ATTN_KV_HEADS, HEAD_DIM))
            vs_l.append(v_samp[:, -WINDOW:].reshape(nb, WINDOW, ATTN_KV_HEADS, HEAD_DIM))
        elif kind == 1:
            nq = RET_HEADS * RET_DK // 1024
            nv = RET_HEADS * RET_DV // 1024
            modes = [["q"] * 4] * nq + [["k"] * 4] * nq + [["plain"] * 4] * (2 * nv)
            proj = _norm_matmul(x, g_mix, ret_w_in[j].astype(bf16), ret_tabs, modes, _epi_ret,
                                bf16, tm_ret, "ret_proj")
            gnw = ret_gn_w[j][None, :]
            lp = _pick(tp, (256, 128, 64))
            y_p, s_p = _retention(proj, state_ret[j], gnw, lg, 0, tp, lp, False, "ret_prompt")
            y_s, s_s = _retention(proj, state_ret[j], gnw, lg, tp, nb * ns, ns, True, "ret_sample")
            x = _matmul_residual2(y_p, y_s, ret_w_o[j].astype(bf16), x, "ret_out")
            rp_l.append(s_p)
            rs_l.append(s_s)
        else:
            starts = jnp.concatenate([jnp.zeros((npc, D), f32), state_rwkv_shift[j]], axis=0)[:, None, :]
            P = LORA_PAD
            tn = 512
            wcat = jnp.concatenate(
                [rwkv_w_rkv[j][0], rwkv_w_rkv[j][1], rwkv_w_rkv[j][2],
                 _pad_cols(rwkv_w1[j], P), _pad_cols(rwkv_a1[j], P), _pad_cols(rwkv_g1[j], P)],
                axis=1).astype(bf16)
            tiles_per_d = D // tn
            mu = rwkv_mu[j]
            mu_tiles = jnp.concatenate(
                [jnp.repeat(mu[jnp.array([0, 2, 3])], tiles_per_d, axis=0), mu[jnp.array([1, 4, 5])]],
                axis=0)[:, None, :]
            proj, h_last = _lerp_matmul(x, g_mix, starts, mu_tiles, wcat, tn, tiles_per_d,
                                        3 * tiles_per_d, npc)
            prep = _rwkv_prep(
                proj, 3 * D // (3 * P),
                _pad_rows(rwkv_w2[j], P).astype(bf16), _pad_rows(rwkv_a2[j], P).astype(bf16),
                _pad_rows(rwkv_g2[j], P).astype(bf16), rwkv_w0[j][None, :], rwkv_a0[j][None, :],
                rwkv_k_k[j][None, :], rwkv_k_a[j][None, :], rwkv_r_k[j].reshape(1, D))
            ops, v16, gl, g, bonus = prep[:6], prep[6], prep[7], prep[8], prep[9]
            o, s_all = _rwkv_core(ops, v16, gl, state_rwkv[j], npc, nchunks)
            x = _rwkv_out(o, bonus, g, rwkv_ln_w[j][None, :], rwkv_ln_b[j][None, :],
                          rwkv_w_o[j].astype(bf16), x)
            wp_l.append(s_all[:1])
            ws_l.append(s_all[1:])
            shp_l.append(h_last[npc - 1])
            shs_l.append(h_last[npc:, 0])
        x = _mlp(x, norm_mlp[i][None, :], w_up16, w_down16, i, norm_final[None, :], i == depth - 1, "mlp")

    y_prompt = x[:tp].reshape(1, tp, D)
    y_sample = x[tp:].reshape(nb, ns, D)
    return (y_prompt, y_sample,
            jnp.stack(kp_l), jnp.stack(vp_l), jnp.stack(ks_l), jnp.stack(vs_l),
            jnp.stack(rp_l), jnp.stack(rs_l),
            jnp.stack(wp_l), jnp.stack(ws_l), jnp.stack(shp_l), jnp.stack(shs_l))
```

```python
import functools

import jax
import jax.numpy as jnp
from jax import lax
from jax.experimental import pallas as pl
from jax.experimental.pallas import tpu as pltpu

f32 = jnp.float32
bf16 = jnp.bfloat16

CHUNK = 64
NORM_EPS = 1e-5
PAST_LEN = 4096

ATTN_HEADS = 32
ATTN_KV_HEADS = 4
ATTN_GROUP = ATTN_HEADS // ATTN_KV_HEADS
HEAD_DIM = 64
WINDOW = 128
ROT_DIM = HEAD_DIM // 4
ROPE_THETA = 500000.0

RET_HEADS = 8
RET_DK = 256
RET_DV = 512
RET_THETA = 10000.0
RET_GN_EPS = 1e-5

RWKV_HS = 64
RWKV_GN_EPS = 64e-5
LORA_PAD = 512

VMEM_LIMIT = 52 * 1024 * 1024

NT_DIMS = (((1,), (1,)), ((), ()))


def _pick(n, cands):
    for c in cands:
        if n % c == 0:
            return c
    raise ValueError(f"no tile for {n} in {cands}")


def _params(sem):
    return pltpu.CompilerParams(dimension_semantics=sem, vmem_limit_bytes=VMEM_LIMIT)


def _rms(x, g):
    return x * lax.rsqrt(jnp.mean(x * x, axis=-1, keepdims=True) + NORM_EPS) * g


def _epi_attn(blk, mode, tabs):
    if mode == "plain":
        return blk
    c_ref, s1_ref, s2_ref = tabs
    out = (blk * c_ref[...] + pltpu.roll(blk, 8, 1) * s1_ref[...]
           + pltpu.roll(blk, 256 - 8, 1) * s2_ref[...])
    return out * (HEAD_DIM ** -0.5) if mode == "q" else out


def _epi_ret(blk, mode, tabs):
    if mode == "plain":
        return blk
    cos_ref, sin_ref = tabs
    c, s = cos_ref[...], sin_ref[...]
    x1, x2 = blk[:, :128], blk[:, 128:]
    out = jnp.concatenate([x1 * c - x2 * s, x2 * c + x1 * s], axis=1)
    return out * (RET_DK ** -0.5) if mode == "k" else out


EPI_COLS = 256


def _normmm_kernel(*refs, routes, ntab, nout, epi):
    x_ref, g_ref, w_ref = refs[:3]
    tabs = refs[3:3 + ntab]
    outs = refs[3 + ntab:3 + ntab + nout]
    xn_ref = refs[3 + ntab + nout]
    j = pl.program_id(1)

    @pl.when(j == 0)
    def _():
        xn_ref[...] = _rms(x_ref[...], g_ref[...]).astype(bf16)

    groups = {}
    for jj, tile_routes in enumerate(routes):
        groups.setdefault(tuple(tile_routes), []).append(jj)
    for tile_routes, jjs in groups.items():
        cond = functools.reduce(jnp.logical_or, [j == jj for jj in jjs])

        @pl.when(cond)
        def _(tile_routes=tile_routes):
            for s, (mode, oi, col) in enumerate(tile_routes):
                acc = jnp.dot(xn_ref[...], w_ref[:, s * EPI_COLS:(s + 1) * EPI_COLS],
                              preferred_element_type=f32)
                outs[oi][:, col:col + EPI_COLS] = epi(acc, mode, tabs).astype(outs[oi].dtype)


def _norm_matmul(x, g, w, tabs, routes, epi, out_shapes, out_specs, tm, name):
    T, D = x.shape
    N = w.shape[1]
    ntile = len(routes)
    tn = N // ntile
    assert tn * ntile == N and tn == EPI_COLS * len(routes[0])
    tab_specs = [pl.BlockSpec((tm, t.shape[1]), lambda i, j: (i, 0)) for t in tabs]
    return pl.pallas_call(
        functools.partial(_normmm_kernel, routes=routes, ntab=len(tabs), nout=len(out_shapes), epi=epi),
        out_shape=tuple(out_shapes),
        grid=(T // tm, ntile),
        in_specs=[pl.BlockSpec((tm, D), lambda i, j: (i, 0)),
                  pl.BlockSpec((1, D), lambda i, j: (0, 0)),
                  pl.BlockSpec((D, tn), lambda i, j: (0, j))] + tab_specs,
        out_specs=tuple(out_specs),
        scratch_shapes=[pltpu.VMEM((tm, D), bf16)],
        compiler_params=_params(("parallel", "arbitrary")),
        name=name,
    )(x, g, w, *tabs)


def _mmres_kernel(a_ref, w_ref, r_ref, o_ref):
    o_ref[...] = r_ref[...] + jnp.dot(a_ref[...], w_ref[...], preferred_element_type=f32)


def _matmul_residual(a, w, res, name):
    T, K = a.shape
    N = w.shape[1]
    tm = _pick(T, (512, 256, 128, 64))
    tn = _pick(N, (1024, 512, 256))
    return pl.pallas_call(
        _mmres_kernel,
        out_shape=jax.ShapeDtypeStruct((T, N), f32),
        grid=(T // tm, N // tn),
        in_specs=[pl.BlockSpec((tm, K), lambda i, j: (i, 0)),
                  pl.BlockSpec((K, tn), lambda i, j: (0, j)),
                  pl.BlockSpec((tm, tn), lambda i, j: (i, j))],
        out_specs=pl.BlockSpec((tm, tn), lambda i, j: (i, j)),
        compiler_params=_params(("parallel", "parallel")),
        name=name,
    )(a, w, res)


def _mmres2_kernel(ap_ref, as_ref, w_ref, r_ref, o_ref, *, np_blocks):
    i = pl.program_id(0)

    @pl.when(i < np_blocks)
    def _():
        o_ref[...] = r_ref[...] + jnp.dot(ap_ref[...], w_ref[...], preferred_element_type=f32)

    @pl.when(i >= np_blocks)
    def _():
        o_ref[...] = r_ref[...] + jnp.dot(as_ref[...], w_ref[...], preferred_element_type=f32)


def _matmul_residual2(a_p, a_s, w, res, name):
    tp, K = a_p.shape
    ts = a_s.shape[0]
    N = w.shape[1]
    tm = _pick(ts, (512, 256, 128, 64))
    assert tp % tm == 0
    np_blocks = tp // tm
    tn = _pick(N, (1024, 512, 256))
    return pl.pallas_call(
        functools.partial(_mmres2_kernel, np_blocks=np_blocks),
        out_shape=jax.ShapeDtypeStruct((tp + ts, N), f32),
        grid=((tp + ts) // tm, N // tn),
        in_specs=[pl.BlockSpec((tm, K), lambda i, j: (jnp.minimum(i, np_blocks - 1), 0)),
                  pl.BlockSpec((tm, K), lambda i, j: (jnp.maximum(i - np_blocks, 0), 0)),
                  pl.BlockSpec((K, tn), lambda i, j: (0, j)),
                  pl.BlockSpec((tm, tn), lambda i, j: (i, j))],
        out_specs=pl.BlockSpec((tm, tn), lambda i, j: (i, j)),
        compiler_params=_params(("parallel", "parallel")),
        name=name,
    )(a_p, a_s, w, res)


def _mlp_kernel(x_ref, g_ref, wu_ref, wd_ref, gf_ref, o_ref, xn_ref, *, nk, final):
    k = pl.program_id(1)

    @pl.when(k == 0)
    def _():
        x = x_ref[...]
        xn_ref[...] = _rms(x, g_ref[...]).astype(bf16)
        o_ref[...] = x

    h = jnp.dot(xn_ref[...], wu_ref[...], preferred_element_type=f32)
    h = jnp.square(jnp.maximum(h, 0.0)).astype(bf16)
    o_ref[...] += jnp.dot(h, wd_ref[...], preferred_element_type=f32)

    if final:
        @pl.when(k == nk - 1)
        def _():
            o_ref[...] = _rms(o_ref[...], gf_ref[...])


def _mlp(x, g, w_up, w_down, layer, g_final, final, name):
    T, D = x.shape
    F = w_up.shape[2]
    tm = _pick(T, (512, 256, 128, 64))
    tf = _pick(F, (1024, 512))
    nk = F // tf
    return pl.pallas_call(
        functools.partial(_mlp_kernel, nk=nk, final=final),
        out_shape=jax.ShapeDtypeStruct((T, D), f32),
        grid=(T // tm, nk),
        in_specs=[pl.BlockSpec((tm, D), lambda i, k: (i, 0)),
                  pl.BlockSpec((1, D), lambda i, k: (0, 0)),
                  pl.BlockSpec((None, D, tf), lambda i, k: (layer, 0, k)),
                  pl.BlockSpec((None, tf, D), lambda i, k: (layer, k, 0)),
                  pl.BlockSpec((1, D), lambda i, k: (0, 0))],
        out_specs=pl.BlockSpec((tm, D), lambda i, k: (i, 0)),
        scratch_shapes=[pltpu.VMEM((tm, D), bf16)],
        compiler_params=_params(("parallel", "arbitrary")),
        name=name,
    )(x, g, w_up, w_down, g_final)


def _attn_kernel(sb_ref, q_ref, k0_ref, k1_ref, k2_ref, v0_ref, v1_ref, v2_ref, o_ref, *, npc):
    c = pl.program_id(0)
    nkeys = 3 * CHUNK
    ncols = nkeys + CHUNK
    PW = 2 * HEAD_DIM
    pairs = ATTN_GROUP // 2
    col_blk = lax.broadcasted_iota(jnp.int32, (1, ncols), 1) // CHUNK
    valid = jnp.logical_or(jnp.logical_or(c >= npc, c + col_blk >= 2), col_blk >= 3)
    bias = jnp.where(valid, 0.0, -jnp.inf).astype(f32)
    zpad = jnp.zeros((CHUNK, ATTN_KV_HEADS * PW), bf16)
    kcat = jnp.concatenate([k0_ref[...], k1_ref[...], k2_ref[...], zpad], axis=0)
    vcat = jnp.concatenate([v0_ref[...], v1_ref[...], v2_ref[...], zpad], axis=0)
    lo_half = lax.broadcasted_iota(jnp.int32, (1, PW), 1) < HEAD_DIM
    ones = jnp.ones((ncols, PW), bf16)
    kvs = range(ATTN_KV_HEADS)

    def stacked_q(kv):
        parts = []
        for p in range(pairs):
            col = (kv * pairs + p) * PW
            qp = q_ref[:, col:col + PW]
            parts += [jnp.where(lo_half, qp, 0.0), jnp.where(lo_half, 0.0, qp)]
        return jnp.concatenate(parts, axis=0).astype(bf16)

    s = [lax.dot_general(stacked_q(kv), kcat[:, kv * PW:(kv + 1) * PW], NT_DIMS,
                         preferred_element_type=f32) + sb_ref[kv] + bias for kv in kvs]
    p = [jnp.exp(s[kv] - jnp.max(s[kv], axis=-1, keepdims=True)).astype(bf16) for kv in kvs]
    oa = [jnp.dot(p[kv], jnp.concatenate([vcat[:, kv * PW:(kv + 1) * PW], ones], axis=1),
                  preferred_element_type=f32) for kv in kvs]
    for kv in kvs:
        on = oa[kv][:, :PW] / oa[kv][:, PW:]
        for pi in range(pairs):
            r0 = 2 * pi * CHUNK
            blk = jnp.where(lo_half, on[r0:r0 + CHUNK], on[r0 + CHUNK:r0 + 2 * CHUNK])
            col = (kv * pairs + pi) * PW
            o_ref[:, col:col + PW] = blk.astype(o_ref.dtype)


def _attention(qkv, kext, vext, sinks, npc, nchunks):
    T = qkv.shape[0]
    qd = ATTN_HEADS * HEAD_DIM
    kvd = kext.shape[1]

    def kv_spec(j):
        def imap(c):
            return (jnp.where(c < npc, c, npc + 2 + 3 * (c - npc)) + j, 0)
        return pl.BlockSpec((CHUNK, kvd), imap)

    nkeys = 3 * CHUNK
    col = jnp.arange(nkeys + CHUNK)[None, None, :]
    sink_rows = jnp.repeat(sinks.astype(f32).reshape(ATTN_KV_HEADS, ATTN_GROUP), CHUNK, axis=1)[:, :, None]
    sink_bias = jnp.where(col < nkeys, 0.0, jnp.where(col == nkeys, sink_rows, -jnp.inf)).astype(f32)

    return pl.pallas_call(
        functools.partial(_attn_kernel, npc=npc),
        out_shape=jax.ShapeDtypeStruct((T, qd), bf16),
        grid=(nchunks,),
        in_specs=[pl.BlockSpec(sink_bias.shape, lambda c: (0, 0, 0)),
                  pl.BlockSpec((CHUNK, qd), lambda c: (c, 0))]
                 + [kv_spec(j) for j in range(3)] + [kv_spec(j) for j in range(3)],
        out_specs=pl.BlockSpec((CHUNK, qd), lambda c: (c, 0)),
        compiler_params=_params(("parallel",)),
        name="attn_core",
    )(sink_bias, qkv, kext, kext, kext, vext, vext, vext)


def _ret_kernel(lg_ref, q_ref, k_ref, v_ref, g_ref, s0_ref, gnw_ref, y_ref, sout_ref, S_ref, *, from_state):
    h = pl.program_id(0)
    c = pl.program_id(1)
    L = q_ref.shape[0]
    lg = lg_ref[h]

    if from_state:
        S_ref[...] = s0_ref[0, 0]
    else:
        @pl.when(c == 0)
        def _():
            S_ref[...] = jnp.zeros_like(S_ref)

    q = q_ref[...]
    k = k_ref[...]
    v = v_ref[...]
    row = lax.broadcasted_iota(jnp.int32, (L, L), 0)
    col = lax.broadcasted_iota(jnp.int32, (L, L), 1)
    diff = (row - col).astype(f32)
    decay = jnp.where(diff >= 0, jnp.exp(lg * jnp.maximum(diff, 0.0)), 0.0)
    idx = lax.broadcasted_iota(jnp.int32, (L, 1), 0).astype(f32)
    xi = jnp.exp(lg * (idx + 1.0))
    zeta = jnp.exp(lg * (L - 1.0 - idx))

    S = S_ref[...]
    scores = lax.dot_general(q, k, NT_DIMS, preferred_element_type=f32) * decay
    o = (jnp.dot(scores.astype(bf16), v, preferred_element_type=f32)
         + jnp.dot((q.astype(f32) * xi).astype(bf16), S.astype(bf16), preferred_element_type=f32))
    kz_t = (k.astype(f32) * zeta).T.astype(bf16)
    S_new = jnp.exp(lg * L) * S + jnp.dot(kz_t, v, preferred_element_type=f32)
    S_ref[...] = S_new
    sout_ref[0, 0] = S_new

    mu = jnp.mean(o, axis=-1, keepdims=True)
    d = o - mu
    var = jnp.mean(d * d, axis=-1, keepdims=True)
    on = d * lax.rsqrt(var + RET_GN_EPS) * gnw_ref[...]
    y_ref[...] = (jax.nn.silu(g_ref[...].astype(f32)) * on).astype(y_ref.dtype)


def _retention(proj, state, gn_w, lg, row0, nrows, L, from_state, name):
    nsteps = nrows // L
    b0 = row0 // L
    assert nsteps * L == nrows and b0 * L == row0
    nseq = nsteps if from_state else 1
    qb = RET_HEADS
    vb = 2 * RET_HEADS * RET_DK // RET_DV
    gb = vb + RET_HEADS
    state_map = (lambda h, c: (c, h, 0, 0)) if from_state else (lambda h, c: (0, h, 0, 0))

    return pl.pallas_call(
        functools.partial(_ret_kernel, from_state=from_state),
        out_shape=(jax.ShapeDtypeStruct((nrows, RET_HEADS * RET_DV), bf16),
                   jax.ShapeDtypeStruct((nseq, RET_HEADS, RET_DK, RET_DV), f32)),
        grid=(RET_HEADS, nsteps),
        in_specs=[pl.BlockSpec(memory_space=pltpu.SMEM),
                  pl.BlockSpec((L, RET_DK), lambda h, c: (b0 + c, h)),
                  pl.BlockSpec((L, RET_DK), lambda h, c: (b0 + c, qb + h)),
                  pl.BlockSpec((L, RET_DV), lambda h, c: (b0 + c, vb + h)),
                  pl.BlockSpec((L, RET_DV), lambda h, c: (b0 + c, gb + h)),
                  pl.BlockSpec((1, 1, RET_DK, RET_DV), state_map),
                  pl.BlockSpec((1, RET_DV), lambda h, c: (0, h))],
        out_specs=(pl.BlockSpec((L, RET_DV), lambda h, c: (c, h)),
                   pl.BlockSpec((1, 1, RET_DK, RET_DV), state_map)),
        scratch_shapes=[pltpu.VMEM((RET_DK, RET_DV), f32)],
        compiler_params=_params(("parallel", "arbitrary")),
        name=name,
    )(lg, proj, proj, proj, proj, state, gn_w)


SUBLANES = 8


def _lerpmm_kernel(x_ref, xprev_ref, g_ref, start_ref, mu_ref, w_ref, o_ref, hlast_ref,
                   h_ref, xx_ref, l_ref, *, tiles_per_d, n_big, npc):
    i = pl.program_id(0)
    j = pl.program_id(1)
    tm = x_ref.shape[0]
    cpb = tm // CHUNK

    @pl.when(j == 0)
    def _():
        g = g_ref[...]
        h = _rms(x_ref[...], g)
        h_ref[...] = h
        xx_ref[...] = pltpu.roll(h, 1, 0) - h
        prev = _rms(xprev_ref[...], g)[SUBLANES - 1:SUBLANES, :]
        xx_ref[0:1, :] = prev - h[0:1, :]
        for ci in range(cpb):
            gc = i * cpb + ci
            r0 = ci * CHUNK

            @pl.when(jnp.logical_or(gc == 0, gc >= npc))
            def _(ci=ci, r0=r0):
                xx_ref[r0:r0 + 1, :] = start_ref[ci] - h_ref[r0:r0 + 1, :]

            hlast_ref[ci] = h[r0 + CHUNK - 1:r0 + CHUNK, :]

    @pl.when(jnp.logical_or(j % tiles_per_d == 0, j >= n_big))
    def _():
        l_ref[...] = (h_ref[...] + xx_ref[...] * mu_ref[0]).astype(bf16)

    o_ref[...] = jnp.dot(l_ref[...], w_ref[...], preferred_element_type=f32)


def _lerp_matmul(x, g, starts, mu_tiles, wcat, tn, tiles_per_d, n_big, npc):
    T, D = x.shape
    N = wcat.shape[1]
    tm = _pick(T, (512, 256, 128, 64))
    cpb = tm // CHUNK
    return pl.pallas_call(
        functools.partial(_lerpmm_kernel, tiles_per_d=tiles_per_d, n_big=n_big, npc=npc),
        out_shape=(jax.ShapeDtypeStruct((T, N), f32),
                   jax.ShapeDtypeStruct((T // CHUNK, 1, D), f32)),
        grid=(T // tm, N // tn),
        in_specs=[pl.BlockSpec((tm, D), lambda i, j: (i, 0)),
                  pl.BlockSpec((SUBLANES, D), lambda i, j: (jnp.maximum(i * (tm // SUBLANES) - 1, 0), 0)),
                  pl.BlockSpec((1, D), lambda i, j: (0, 0)),
                  pl.BlockSpec((cpb, 1, D), lambda i, j: (i, 0, 0)),
                  pl.BlockSpec((1, 1, D), lambda i, j: (j, 0, 0)),
                  pl.BlockSpec((D, tn), lambda i, j: (0, j))],
        out_specs=(pl.BlockSpec((tm, tn), lambda i, j: (i, j)),
                   pl.BlockSpec((cpb, 1, D), lambda i, j: (i, 0, 0))),
        scratch_shapes=[pltpu.VMEM((tm, D), f32), pltpu.VMEM((tm, D), f32), pltpu.VMEM((tm, D), bf16)],
        compiler_params=_params(("parallel", "arbitrary")),
        name="rwkv_proj",
    )(x, x, g, starts, mu_tiles, wcat)


def _softplus(z):
    return jnp.maximum(z, 0.0) + jnp.log(1.0 + jnp.exp(-jnp.abs(z)))


SEG_TILE = 256


def _head_sum(x, split):
    rows, D = x.shape
    r = lax.broadcasted_iota(jnp.int32, (SEG_TILE, SEG_TILE), 0) // RWKV_HS
    c = lax.broadcasted_iota(jnp.int32, (SEG_TILE, SEG_TILE), 1) // RWKV_HS
    ones = (r == c).astype(bf16)
    out = []
    for j in range(D // SEG_TILE):
        blk = x[:, j * SEG_TILE:(j + 1) * SEG_TILE]
        hi = blk.astype(bf16)
        s = jnp.dot(hi, ones, preferred_element_type=f32)
        if split:
            lo = (blk - hi.astype(f32)).astype(bf16)
            s = s + jnp.dot(lo, ones, preferred_element_type=f32)
        out.append(s)
    return jnp.concatenate(out, axis=1)


def _chunk_cumsum(x):
    rows = x.shape[0]
    row = lax.broadcasted_iota(jnp.int32, (rows, rows), 0)
    col = lax.broadcasted_iota(jnp.int32, (rows, rows), 1)
    tri = jnp.logical_and((row // CHUNK) == (col // CHUNK), row >= col).astype(bf16)
    hi = x.astype(bf16)
    rest = x - hi.astype(f32)
    mid = rest.astype(bf16)
    lo = (rest - mid.astype(f32)).astype(bf16)
    dot = functools.partial(jnp.dot, preferred_element_type=f32)
    return dot(tri, hi) + dot(tri, mid) + dot(tri, lo)


def _rwkv_prep_kernel(p_ref, r_ref, k_ref, v_ref, w2_ref, a2_ref, g2_ref, w0_ref, a0_ref,
                      kk_ref, ka_ref, rk_ref,
                      at_ref, rt_ref, bt_ref, kt_ref, bh_ref, kh_ref, v16_ref, gl_ref, g_ref, bonus_ref):
    P = LORA_PAD
    tm = r_ref.shape[0]
    pw = jnp.tanh(p_ref[:, :P]).astype(bf16)
    pa = p_ref[:, P:2 * P].astype(bf16)
    pg = jax.nn.sigmoid(p_ref[:, 2 * P:]).astype(bf16)
    wl = w0_ref[...] + jnp.dot(pw, w2_ref[...], preferred_element_type=f32)
    lw = -jnp.exp(-_softplus(-wl) - 0.5)
    a = jax.nn.sigmoid(a0_ref[...] + jnp.dot(pa, a2_ref[...], preferred_element_type=f32))
    g_ref[...] = jnp.dot(pg, g2_ref[...], preferred_element_type=f32)

    cum = _chunk_cumsum(lw)
    tot = jnp.concatenate(
        [jnp.broadcast_to(cum[(ci + 1) * CHUNK - 1:(ci + 1) * CHUNK, :], (CHUNK, cum.shape[1]))
         for ci in range(tm // CHUNK)], axis=0)

    r, k, v = r_ref[...], k_ref[...], v_ref[...]
    kk = k * kk_ref[...]
    kk = kk / jnp.maximum(jnp.sqrt(_head_sum(kk * kk, True)), 1e-12)
    kmod = k * (1.0 + (a - 1.0) * ka_ref[...])
    beta = kk * a
    e_neg = jnp.exp(-cum)
    e_end = jnp.exp(tot - cum)
    at_ref[...] = (-kk * jnp.exp(cum - lw)).astype(bf16)
    rt_ref[...] = (r * jnp.exp(cum)).astype(bf16)
    bt_ref[...] = (beta * e_neg).astype(bf16)
    kt_ref[...] = (kmod * e_neg).astype(bf16)
    bh_ref[...] = (beta * e_end).astype(bf16)
    kh_ref[...] = (kmod * e_end).astype(bf16)
    v16_ref[...] = v.astype(bf16)
    bonus_ref[...] = _head_sum(r * kmod * rk_ref[...], True) * v
    for ci in range(tm // CHUNK):
        gl_ref[ci] = jnp.exp(tot[ci * CHUNK:ci * CHUNK + 1, :])


def _rwkv_prep(proj, lora_block, w2p, a2p, g2p, w0, a0, k_k, k_a, r_k):
    T = proj.shape[0]
    D = w2p.shape[1]
    P = LORA_PAD
    tm = _pick(T, (128, 64))
    wspec = pl.BlockSpec((P, D), lambda i: (0, 0))
    vspec = pl.BlockSpec((1, D), lambda i: (0, 0))
    ospec = pl.BlockSpec((tm, D), lambda i: (i, 0))
    o16 = jax.ShapeDtypeStruct((T, D), bf16)
    o32 = jax.ShapeDtypeStruct((T, D), f32)
    return pl.pallas_call(
        _rwkv_prep_kernel,
        out_shape=(o16,) * 7 + (jax.ShapeDtypeStruct((T // CHUNK, 1, D), f32), o32, o32),
        grid=(T // tm,),
        in_specs=[pl.BlockSpec((tm, 3 * P), lambda i: (i, lora_block)),
                  pl.BlockSpec((tm, D), lambda i: (i, 0)),
                  pl.BlockSpec((tm, D), lambda i: (i, 1)),
                  pl.BlockSpec((tm, D), lambda i: (i, 2)),
                  wspec, wspec, wspec, vspec, vspec, vspec, vspec, vspec],
        out_specs=(ospec,) * 7 + (pl.BlockSpec((tm // CHUNK, 1, D), lambda i: (i, 0, 0)), ospec, ospec),
        compiler_params=_params(("parallel",)),
        name="rwkv_prep",
    )(proj, proj, proj, proj, w2p, a2p, g2p, w0, a0, k_k, k_a, r_k)


def _rwkv_heads_chunk(heads):
    L, N = heads[0][6].shape
    nh = len(heads)
    dot = functools.partial(jnp.dot, preferred_element_type=f32)
    row = lax.broadcasted_iota(jnp.int32, (L, L), 0)
    col = lax.broadcasted_iota(jnp.int32, (L, L), 1)
    incl = row >= col
    strict = row > col
    lhs = [jnp.concatenate([h[0], h[1]], axis=0) for h in heads]
    rhs = [jnp.concatenate([h[2], h[3]], axis=0) for h in heads]
    G = [lax.dot_general(lhs[i], rhs[i], NT_DIMS, preferred_element_type=f32) for i in range(nh)]
    LS = [lax.dot_general(lhs[i], heads[i][7].astype(bf16), NT_DIMS, preferred_element_type=f32)
          for i in range(nh)]
    n = [jnp.where(strict, g[:L, :L], 0.0) for g in G]
    a_ak = [jnp.where(strict, g[:L, L:], 0.0).astype(bf16) for g in G]
    a_r = [jnp.concatenate([jnp.where(incl, g[L:, :L], 0.0), jnp.where(incl, g[L:, L:], 0.0)],
                           axis=1).astype(bf16) for g in G]
    x = [LS[i][:L] + dot(a_ak[i], heads[i][6]) for i in range(nh)]
    steps = max(1, (L - 1).bit_length())
    for s in range(steps):
        if s < steps - 1:
            z = [dot(n[i].astype(bf16), jnp.concatenate([x[i], n[i]], axis=1).astype(bf16))
                 for i in range(nh)]
            x = [x[i] + z[i][:, :N] for i in range(nh)]
            n = [z[i][:, N:] for i in range(nh)]
        else:
            x = [x[i] + dot(n[i].astype(bf16), x[i].astype(bf16)) for i in range(nh)]
    uv = [jnp.concatenate([x[i], heads[i][6].astype(f32)], axis=0) for i in range(nh)]
    o = [LS[i][L:] + dot(a_r[i], uv[i].astype(bf16)) for i in range(nh)]
    S_new = [heads[i][7] * heads[i][8]
             + dot(uv[i].T.astype(bf16), jnp.concatenate([heads[i][4], heads[i][5]], axis=0))
             for i in range(nh)]
    return list(zip(o, S_new))


def _rwkv_core_kernel(at_ref, rt_ref, bt_ref, kt_ref, bh_ref, kh_ref, v_ref, gl_ref, s0_ref,
                      o_ref, sout_ref, S_ref, *, npc, nhead):
    c = pl.program_id(1)
    N = RWKV_HS

    @pl.when(c == 0)
    def _():
        S_ref[...] = jnp.zeros_like(S_ref)

    @pl.when(c >= npc)
    def _():
        S_ref[...] = s0_ref[0]

    gl = gl_ref[0]
    heads = []
    for hh in range(nhead):
        sl = slice(hh * N, (hh + 1) * N)
        heads.append((at_ref[:, sl], rt_ref[:, sl], bt_ref[:, sl], kt_ref[:, sl], bh_ref[:, sl],
                      kh_ref[:, sl], v_ref[:, sl], S_ref[hh], gl[:, sl]))
    res = _rwkv_heads_chunk(heads)
    for hh in range(nhead):
        S_ref[hh] = res[hh][1]
    o_ref[...] = jnp.concatenate([o for o, _ in res], axis=1)

    @pl.when(jnp.logical_or(c == npc - 1, c >= npc))
    def _():
        sout_ref[0] = S_ref[...]


RWKV_HEADS_PER_STEP = 32


def _rwkv_core(ops, v16, gl, state, npc, nchunks):
    T, D = v16.shape
    N = RWKV_HS
    nh = D // N
    nb = state.shape[0]
    G = RWKV_HEADS_PER_STEP
    W = G * N
    tok = pl.BlockSpec((CHUNK, W), lambda p, c: (c, p))

    def sin_map(p, c):
        return (jnp.maximum(c - npc, 0), p, 0, 0)

    def sout_map(p, c):
        return (jnp.maximum(c - npc + 1, 0), p, 0, 0)

    return pl.pallas_call(
        functools.partial(_rwkv_core_kernel, npc=npc, nhead=G),
        out_shape=(jax.ShapeDtypeStruct((T, D), f32),
                   jax.ShapeDtypeStruct((nb + 1, nh, N, N), f32)),
        grid=(nh // G, nchunks),
        in_specs=[tok] * 7 + [pl.BlockSpec((1, 1, W), lambda p, c: (c, 0, p)),
                              pl.BlockSpec((1, G, N, N), sin_map)],
        out_specs=(tok, pl.BlockSpec((1, G, N, N), sout_map)),
        scratch_shapes=[pltpu.VMEM((G, N, N), f32)],
        compiler_params=_params(("parallel", "arbitrary")),
        name="rwkv_core",
    )(*ops, v16, gl, state)


def _rwkv_out_kernel(o_ref, bonus_ref, g_ref, lnw_ref, lnb_ref, w_ref, res_ref, out_ref, y_ref):
    j = pl.program_id(1)

    @pl.when(j == 0)
    def _():
        o = o_ref[...]
        mu = _head_sum(o, False) * (1.0 / RWKV_HS)
        d = o - mu
        var = _head_sum(d * d, False) * (1.0 / RWKV_HS)
        on = d * lax.rsqrt(var + RWKV_GN_EPS) * lnw_ref[...] + lnb_ref[...]
        y_ref[...] = ((on + bonus_ref[...]) * g_ref[...]).astype(bf16)

    out_ref[...] = res_ref[...] + jnp.dot(y_ref[...], w_ref[...], preferred_element_type=f32)


def _rwkv_out(o, bonus, g, ln_w, ln_b, w, res):
    T, D = o.shape
    N = w.shape[1]
    tm = _pick(T, (256, 128, 64))
    tn = _pick(N, (1024, 512, 256))
    tspec = pl.BlockSpec((tm, D), lambda i, j: (i, 0))
    vspec = pl.BlockSpec((1, D), lambda i, j: (0, 0))
    return pl.pallas_call(
        _rwkv_out_kernel,
        out_shape=jax.ShapeDtypeStruct((T, N), f32),
        grid=(T // tm, N // tn),
        in_specs=[tspec, tspec, tspec, vspec, vspec,
                  pl.BlockSpec((D, tn), lambda i, j: (0, j)),
                  pl.BlockSpec((tm, tn), lambda i, j: (i, j))],
        out_specs=pl.BlockSpec((tm, tn), lambda i, j: (i, j)),
        scratch_shapes=[pltpu.VMEM((tm, D), bf16)],
        compiler_params=_params(("parallel", "arbitrary")),
        name="rwkv_out",
    )(o, bonus, g, ln_w, ln_b, w, res)


def _positions(n_prompt, n_b, n_s):
    return jnp.concatenate([jnp.arange(n_prompt), jnp.tile(PAST_LEN + jnp.arange(n_s), n_b)]).astype(f32)


def _attn_tables(pos):
    half = ROT_DIM // 2
    inv = ROPE_THETA ** (-jnp.arange(half, dtype=f32) / half)
    ang = pos[:, None] * inv[None, :]
    cos, sin = jnp.cos(ang), jnp.sin(ang)
    T = pos.shape[0]
    zeros = jnp.zeros((T, half), f32)
    rest = HEAD_DIM - ROT_DIM
    c = jnp.concatenate([cos, cos, jnp.ones((T, rest), f32)], axis=1)
    s1 = jnp.concatenate([zeros, sin, jnp.zeros((T, rest), f32)], axis=1)
    s2 = jnp.concatenate([-sin, zeros, jnp.zeros((T, rest), f32)], axis=1)
    rep = 256 // HEAD_DIM
    return tuple(jnp.tile(t, (1, rep)) for t in (c, s1, s2))


def _ret_tables(pos):
    half = RET_DK // 2
    inv = RET_THETA ** (-jnp.arange(half, dtype=f32) / half)
    ang = pos[:, None] * inv[None, :]
    return jnp.cos(ang), jnp.sin(ang)


def _pad_cols(w, n):
    return jnp.pad(w, ((0, 0), (0, n - w.shape[1])))


def _pad_rows(w, n):
    return jnp.pad(w, ((0, n - w.shape[0]), (0, 0)))


def kernel(x_prompt, x_sample, cache_attn_k, cache_attn_v, state_ret, state_rwkv, state_rwkv_shift,
           norm_mix, norm_mlp, norm_final,
           attn_w_qkv, attn_sinks, attn_w_o,
           ret_w_in, ret_gn_w, ret_w_o,
           rwkv_mu, rwkv_w_rkv, rwkv_w_o, rwkv_w0, rwkv_w1, rwkv_w2, rwkv_a0, rwkv_a1, rwkv_a2,
           rwkv_g1, rwkv_g2, rwkv_k_k, rwkv_k_a, rwkv_r_k, rwkv_ln_w, rwkv_ln_b,
           mlp_w_up, mlp_w_down):
    bp, tp, D = x_prompt.shape
    nb, ns, _ = x_sample.shape
    assert bp == 1 and ns == CHUNK and tp % CHUNK == 0
    depth = norm_mix.shape[0]
    npc = tp // CHUNK
    nchunks = npc + nb
    T = tp + nb * ns
    qd = ATTN_HEADS * HEAD_DIM
    kvd = ATTN_KV_HEADS * HEAD_DIM

    x = jnp.concatenate([x_prompt.reshape(tp, D), x_sample.reshape(nb * ns, D)], axis=0)
    pos = _positions(tp, nb, ns)
    attn_tabs = _attn_tables(pos)
    ret_tabs = _ret_tables(pos)
    lg = jnp.log1p(-jnp.exp2(-5.0 - jnp.arange(RET_HEADS, dtype=f32)))
    tm_proj = _pick(T, (512, 256, 128, 64))
    tm_ret = _pick(T, (1024, 512, 256, 128, 64))
    w_up16 = mlp_w_up.astype(bf16)
    w_down16 = mlp_w_down.astype(bf16)

    kp_l, vp_l, ks_l, vs_l, rp_l, rs_l, wp_l, ws_l, shp_l, shs_l = ([] for _ in range(10))
    for i in range(depth):
        j, kind = divmod(i, 3)
        g_mix = norm_mix[i][None, :]
        if kind == 0:
            nsub = (qd + 2 * kvd) // (2 * EPI_COLS)
            routes = [[("q", 0, s * EPI_COLS) for s in range(nsub)],
                      [("q", 0, s * EPI_COLS) for s in range(nsub, qd // EPI_COLS)]
                      + [("k", 1, 0), ("plain", 1, EPI_COLS)]]
            q, kv = _norm_matmul(
                x, g_mix, attn_w_qkv[j].astype(bf16), attn_tabs, routes, _epi_attn,
                [jax.ShapeDtypeStruct((T, qd), bf16), jax.ShapeDtypeStruct((T, 2 * kvd), f32)],
                [pl.BlockSpec((tm_proj, qd), lambda i, jj: (i, 0)),
                 pl.BlockSpec((tm_proj, 2 * kvd), lambda i, jj: (i, 0))],
                tm_proj, "attn_qkv")
            k_new = kv[:, :kvd]
            v_new = kv[:, kvd:]

            def ext(new, cache):
                samp = jnp.concatenate([cache.reshape(nb, WINDOW, kvd), new[tp:].reshape(nb, ns, kvd)], axis=1)
                rows = jnp.concatenate([jnp.zeros((WINDOW, kvd), f32), new[:tp],
                                        samp.reshape(nb * (WINDOW + ns), kvd)], axis=0)
                r4 = rows.astype(bf16).reshape(rows.shape[0], ATTN_KV_HEADS, 1, HEAD_DIM)
                dup = jnp.broadcast_to(r4, (rows.shape[0], ATTN_KV_HEADS, 2, HEAD_DIM))
                return dup.reshape(rows.shape[0], 2 * kvd), samp

            kext, k_samp = ext(k_new, cache_attn_k[j])
            vext, v_samp = ext(v_new, cache_attn_v[j])
            o = _attention(q, kext, vext, attn_sinks[j], npc, nchunks)
            x = _matmul_residual(o, attn_w_o[j].astype(bf16), x, "attn_out")
            kp_l.append(k_new[tp - WINDOW:tp].reshape(1, WINDOW, ATTN_KV_HEADS, HEAD_DIM))
            vp_l.append(v_new[tp - WINDOW:tp].reshape(1, WINDOW, ATTN_KV_HEADS, HEAD_DIM))
            ks_l.append(k_samp[:, -WINDOW:].reshape(nb, WINDOW, ATTN_KV_HEADS, HEAD_DIM))
            vs_l.append(v_samp[:, -WINDOW:].reshape(nb, WINDOW, ATTN_KV_HEADS, HEAD_DIM))
        elif kind == 1:
            nq = RET_HEADS * RET_DK // 1024
            nv = RET_HEADS * RET_DV // 1024
            tile = lambda mode: [(mode, 0, s * EPI_COLS) for s in range(1024 // EPI_COLS)]
            routes = [tile("q")] * nq + [tile("k")] * nq + [tile("plain")] * (2 * nv)
            (proj,) = _norm_matmul(
                x, g_mix, ret_w_in[j].astype(bf16), ret_tabs, routes, _epi_ret,
                [jax.ShapeDtypeStruct((T, ret_w_in.shape[2]), bf16)],
                [pl.BlockSpec((tm_ret, 1024), lambda i, jj: (i, jj))], tm_ret, "ret_proj")
            gnw = ret_gn_w[j][None, :]
            lp = _pick(tp, (256, 128, 64))
            y_p, s_p = _retention(proj, state_ret[j], gnw, lg, 0, tp, lp, False, "ret_prompt")
            y_s, s_s = _retention(proj, state_ret[j], gnw, lg, tp, nb * ns, ns, True, "ret_sample")
            x = _matmul_residual2(y_p, y_s, ret_w_o[j].astype(bf16), x, "ret_out")
            rp_l.append(s_p)
            rs_l.append(s_s)
        else:
            starts = jnp.concatenate([jnp.zeros((npc, D), f32), state_rwkv_shift[j]], axis=0)[:, None, :]
            P = LORA_PAD
            tn = 512
            wcat = jnp.concatenate(
                [rwkv_w_rkv[j][0], rwkv_w_rkv[j][1], rwkv_w_rkv[j][2],
                 _pad_cols(rwkv_w1[j], P), _pad_cols(rwkv_a1[j], P), _pad_cols(rwkv_g1[j], P)],
                axis=1).astype(bf16)
            tiles_per_d = D // tn
            mu = rwkv_mu[j]
            mu_tiles = jnp.concatenate(
                [jnp.repeat(mu[jnp.array([0, 2, 3])], tiles_per_d, axis=0), mu[jnp.array([1, 4, 5])]],
                axis=0)[:, None, :]
            proj, h_last = _lerp_matmul(x, g_mix, starts, mu_tiles, wcat, tn, tiles_per_d,
                                        3 * tiles_per_d, npc)
            prep = _rwkv_prep(
                proj, 3 * D // (3 * P),
                _pad_rows(rwkv_w2[j], P).astype(bf16), _pad_rows(rwkv_a2[j], P).astype(bf16),
                _pad_rows(rwkv_g2[j], P).astype(bf16), rwkv_w0[j][None, :], rwkv_a0[j][None, :],
                rwkv_k_k[j][None, :], rwkv_k_a[j][None, :], rwkv_r_k[j].reshape(1, D))
            ops, v16, gl, g, bonus = prep[:6], prep[6], prep[7], prep[8], prep[9]
            o, s_all = _rwkv_core(ops, v16, gl, state_rwkv[j], npc, nchunks)
            x = _rwkv_out(o, bonus, g, rwkv_ln_w[j][None, :], rwkv_ln_b[j][None, :],
                          rwkv_w_o[j].astype(bf16), x)
            wp_l.append(s_all[:1])
            ws_l.append(s_all[1:])
            shp_l.append(h_last[npc - 1])
            shs_l.append(h_last[npc:, 0])
        x = _mlp(x, norm_mlp[i][None, :], w_up16, w_down16, i, norm_final[None, :], i == depth - 1, "mlp")

    y_prompt = x[:tp].reshape(1, tp, D)
    y_sample = x[tp:].reshape(nb, ns, D)
    return (y_prompt, y_sample,
            jnp.stack(kp_l), jnp.stack(vp_l), jnp.stack(ks_l), jnp.stack(vs_l),
            jnp.stack(rp_l), jnp.stack(rs_l),
            jnp.stack(wp_l), jnp.stack(ws_l), jnp.stack(shp_l), jnp.stack(shs_l))
```

```python
import functools

import jax
import jax.numpy as jnp
from jax import lax
from jax.experimental import pallas as pl
from jax.experimental.pallas import tpu as pltpu

f32 = jnp.float32
bf16 = jnp.bfloat16

CHUNK = 64
NORM_EPS = 1e-5
PAST_LEN = 4096

ATTN_HEADS = 32
ATTN_KV_HEADS = 4
ATTN_GROUP = ATTN_HEADS // ATTN_KV_HEADS
HEAD_DIM = 64
WINDOW = 128
ROT_DIM = HEAD_DIM // 4
ROPE_THETA = 500000.0

RET_HEADS = 8
RET_DK = 256
RET_DV = 512
RET_THETA = 10000.0
RET_GN_EPS = 1e-5

RWKV_HS = 64
RWKV_GN_EPS = 64e-5
LORA_PAD = 512

VMEM_LIMIT = 52 * 1024 * 1024

NT_DIMS = (((1,), (1,)), ((), ()))


def _pick(n, cands):
    for c in cands:
        if n % c == 0:
            return c
    raise ValueError(f"no tile for {n} in {cands}")


def _params(sem):
    return pltpu.CompilerParams(dimension_semantics=sem, vmem_limit_bytes=VMEM_LIMIT)


def _rms(x, g):
    return x * lax.rsqrt(jnp.mean(x * x, axis=-1, keepdims=True) + NORM_EPS) * g


def _epi_attn(blk, mode, tabs):
    if mode == "plain":
        return blk
    c_ref, s1_ref, s2_ref = tabs
    out = (blk * c_ref[...] + pltpu.roll(blk, 8, 1) * s1_ref[...]
           + pltpu.roll(blk, 256 - 8, 1) * s2_ref[...])
    return out * (HEAD_DIM ** -0.5) if mode == "q" else out


def _epi_ret(blk, mode, tabs):
    if mode == "plain":
        return blk
    cos_ref, sin_ref = tabs
    c, s = cos_ref[...], sin_ref[...]
    x1, x2 = blk[:, :128], blk[:, 128:]
    out = jnp.concatenate([x1 * c - x2 * s, x2 * c + x1 * s], axis=1)
    return out * (RET_DK ** -0.5) if mode == "k" else out


EPI_COLS = 256


def _normmm_kernel(*refs, routes, ntab, nout, epi):
    x_ref, g_ref, w_ref = refs[:3]
    tabs = refs[3:3 + ntab]
    outs = refs[3 + ntab:3 + ntab + nout]
    xn_ref = refs[3 + ntab + nout]
    j = pl.program_id(1)

    @pl.when(j == 0)
    def _():
        xn_ref[...] = _rms(x_ref[...], g_ref[...]).astype(bf16)

    groups = {}
    for jj, tile_routes in enumerate(routes):
        groups.setdefault(tuple(tile_routes), []).append(jj)
    for tile_routes, jjs in groups.items():
        cond = functools.reduce(jnp.logical_or, [j == jj for jj in jjs])

        @pl.when(cond)
        def _(tile_routes=tile_routes):
            for s, (mode, oi, col) in enumerate(tile_routes):
                acc = jnp.dot(xn_ref[...], w_ref[:, s * EPI_COLS:(s + 1) * EPI_COLS],
                              preferred_element_type=f32)
                outs[oi][:, col:col + EPI_COLS] = epi(acc, mode, tabs).astype(outs[oi].dtype)


def _norm_matmul(x, g, w, tabs, routes, epi, out_shapes, out_specs, tm, name):
    T, D = x.shape
    N = w.shape[1]
    ntile = len(routes)
    tn = N // ntile
    assert tn * ntile == N and tn == EPI_COLS * len(routes[0])
    tab_specs = [pl.BlockSpec((tm, t.shape[1]), lambda i, j: (i, 0)) for t in tabs]
    return pl.pallas_call(
        functools.partial(_normmm_kernel, routes=routes, ntab=len(tabs), nout=len(out_shapes), epi=epi),
        out_shape=tuple(out_shapes),
        grid=(T // tm, ntile),
        in_specs=[pl.BlockSpec((tm, D), lambda i, j: (i, 0)),
                  pl.BlockSpec((1, D), lambda i, j: (0, 0)),
                  pl.BlockSpec((D, tn), lambda i, j: (0, j))] + tab_specs,
        out_specs=tuple(out_specs),
        scratch_shapes=[pltpu.VMEM((tm, D), bf16)],
        compiler_params=_params(("parallel", "arbitrary")),
        name=name,
    )(x, g, w, *tabs)


def _mmres_kernel(a_ref, w_ref, r_ref, o_ref):
    o_ref[...] = r_ref[...] + jnp.dot(a_ref[...], w_ref[...], preferred_element_type=f32)


def _resident(shape):
    return pl.BlockSpec(shape, lambda *_: (0,) * len(shape), pipeline_mode=pl.Buffered(1))


def _matmul_residual(a, w, res, name):
    T, K = a.shape
    N = w.shape[1]
    tm = _pick(T, (512, 256, 128, 64))
    return pl.pallas_call(
        _mmres_kernel,
        out_shape=jax.ShapeDtypeStruct((T, N), f32),
        grid=(T // tm,),
        in_specs=[pl.BlockSpec((tm, K), lambda i: (i, 0)),
                  _resident((K, N)),
                  pl.BlockSpec((tm, N), lambda i: (i, 0))],
        out_specs=pl.BlockSpec((tm, N), lambda i: (i, 0)),
        compiler_params=_params(("parallel",)),
        name=name,
    )(a, w, res)


def _mmres2_kernel(ap_ref, as_ref, w_ref, r_ref, o_ref, *, np_blocks):
    i = pl.program_id(0)

    @pl.when(i < np_blocks)
    def _():
        o_ref[...] = r_ref[...] + jnp.dot(ap_ref[...], w_ref[...], preferred_element_type=f32)

    @pl.when(i >= np_blocks)
    def _():
        o_ref[...] = r_ref[...] + jnp.dot(as_ref[...], w_ref[...], preferred_element_type=f32)


def _matmul_residual2(a_p, a_s, w, res, name):
    tp, K = a_p.shape
    ts = a_s.shape[0]
    N = w.shape[1]
    tm = _pick(ts, (256, 128, 64))
    assert tp % tm == 0
    np_blocks = tp // tm
    return pl.pallas_call(
        functools.partial(_mmres2_kernel, np_blocks=np_blocks),
        out_shape=jax.ShapeDtypeStruct((tp + ts, N), f32),
        grid=((tp + ts) // tm,),
        in_specs=[pl.BlockSpec((tm, K), lambda i: (jnp.minimum(i, np_blocks - 1), 0)),
                  pl.BlockSpec((tm, K), lambda i: (jnp.maximum(i - np_blocks, 0), 0)),
                  _resident((K, N)),
                  pl.BlockSpec((tm, N), lambda i: (i, 0))],
        out_specs=pl.BlockSpec((tm, N), lambda i: (i, 0)),
        compiler_params=_params(("parallel",)),
        name=name,
    )(a_p, a_s, w, res)


def _mlp_kernel(x_ref, g_ref, wu_ref, wd_ref, gf_ref, o_ref, xn_ref, *, nk, final):
    k = pl.program_id(1)

    @pl.when(k == 0)
    def _():
        x = x_ref[...]
        xn_ref[...] = _rms(x, g_ref[...]).astype(bf16)
        o_ref[...] = x

    h = jnp.dot(xn_ref[...], wu_ref[...], preferred_element_type=f32)
    h = jnp.square(jnp.maximum(h, 0.0)).astype(bf16)
    o_ref[...] += jnp.dot(h, wd_ref[...], preferred_element_type=f32)

    if final:
        @pl.when(k == nk - 1)
        def _():
            o_ref[...] = _rms(o_ref[...], gf_ref[...])


def _mlp(x, g, w_up, w_down, layer, g_final, final, name):
    T, D = x.shape
    F = w_up.shape[2]
    tm = _pick(T, (512, 256, 128, 64))
    tf = _pick(F, (1024, 512))
    nk = F // tf
    return pl.pallas_call(
        functools.partial(_mlp_kernel, nk=nk, final=final),
        out_shape=jax.ShapeDtypeStruct((T, D), f32),
        grid=(T // tm, nk),
        in_specs=[pl.BlockSpec((tm, D), lambda i, k: (i, 0)),
                  pl.BlockSpec((1, D), lambda i, k: (0, 0)),
                  pl.BlockSpec((None, D, tf), lambda i, k: (layer, 0, k)),
                  pl.BlockSpec((None, tf, D), lambda i, k: (layer, k, 0)),
                  pl.BlockSpec((1, D), lambda i, k: (0, 0))],
        out_specs=pl.BlockSpec((tm, D), lambda i, k: (i, 0)),
        scratch_shapes=[pltpu.VMEM((tm, D), bf16)],
        compiler_params=_params(("parallel", "arbitrary")),
        name=name,
    )(x, g, w_up, w_down, g_final)


def _attn_kernel(sb_ref, q_ref, k0_ref, k1_ref, k2_ref, v0_ref, v1_ref, v2_ref, o_ref, *, npc):
    c = pl.program_id(0)
    nkeys = 3 * CHUNK
    ncols = nkeys + CHUNK
    PW = 2 * HEAD_DIM
    pairs = ATTN_GROUP // 2
    col_blk = lax.broadcasted_iota(jnp.int32, (1, ncols), 1) // CHUNK
    valid = jnp.logical_or(jnp.logical_or(c >= npc, c + col_blk >= 2), col_blk >= 3)
    bias = jnp.where(valid, 0.0, -jnp.inf).astype(f32)
    zpad = jnp.zeros((CHUNK, ATTN_KV_HEADS * PW), bf16)
    kcat = jnp.concatenate([k0_ref[...], k1_ref[...], k2_ref[...], zpad], axis=0)
    vcat = jnp.concatenate([v0_ref[...], v1_ref[...], v2_ref[...], zpad], axis=0)
    lo_half = lax.broadcasted_iota(jnp.int32, (1, PW), 1) < HEAD_DIM
    ones = jnp.ones((ncols, PW), bf16)
    kvs = range(ATTN_KV_HEADS)

    def stacked_q(kv):
        parts = []
        for p in range(pairs):
            col = (kv * pairs + p) * PW
            qp = q_ref[:, col:col + PW]
            parts += [jnp.where(lo_half, qp, 0.0), jnp.where(lo_half, 0.0, qp)]
        return jnp.concatenate(parts, axis=0).astype(bf16)

    s = [lax.dot_general(stacked_q(kv), kcat[:, kv * PW:(kv + 1) * PW], NT_DIMS,
                         preferred_element_type=f32) + sb_ref[kv] + bias for kv in kvs]
    p = [jnp.exp(s[kv] - jnp.max(s[kv], axis=-1, keepdims=True)).astype(bf16) for kv in kvs]
    oa = [jnp.dot(p[kv], jnp.concatenate([vcat[:, kv * PW:(kv + 1) * PW], ones], axis=1),
                  preferred_element_type=f32) for kv in kvs]
    for kv in kvs:
        on = oa[kv][:, :PW] / oa[kv][:, PW:]
        for pi in range(pairs):
            r0 = 2 * pi * CHUNK
            blk = jnp.where(lo_half, on[r0:r0 + CHUNK], on[r0 + CHUNK:r0 + 2 * CHUNK])
            col = (kv * pairs + pi) * PW
            o_ref[:, col:col + PW] = blk.astype(o_ref.dtype)


def _attention(qkv, kext, vext, sinks, npc, nchunks):
    T = qkv.shape[0]
    qd = ATTN_HEADS * HEAD_DIM
    kvd = kext.shape[1]

    def kv_spec(j):
        def imap(c):
            return (jnp.where(c < npc, c, npc + 2 + 3 * (c - npc)) + j, 0)
        return pl.BlockSpec((CHUNK, kvd), imap)

    nkeys = 3 * CHUNK
    col = jnp.arange(nkeys + CHUNK)[None, None, :]
    sink_rows = jnp.repeat(sinks.astype(f32).reshape(ATTN_KV_HEADS, ATTN_GROUP), CHUNK, axis=1)[:, :, None]
    sink_bias = jnp.where(col < nkeys, 0.0, jnp.where(col == nkeys, sink_rows, -jnp.inf)).astype(f32)

    return pl.pallas_call(
        functools.partial(_attn_kernel, npc=npc),
        out_shape=jax.ShapeDtypeStruct((T, qd), bf16),
        grid=(nchunks,),
        in_specs=[pl.BlockSpec(sink_bias.shape, lambda c: (0, 0, 0)),
                  pl.BlockSpec((CHUNK, qd), lambda c: (c, 0))]
                 + [kv_spec(j) for j in range(3)] + [kv_spec(j) for j in range(3)],
        out_specs=pl.BlockSpec((CHUNK, qd), lambda c: (c, 0)),
        compiler_params=_params(("parallel",)),
        name="attn_core",
    )(sink_bias, qkv, kext, kext, kext, vext, vext, vext)


RET_HEADS_PER_STEP = 4


def _ret_kernel(lg_ref, q_ref, k_ref, v_ref, g_ref, s0_ref, gnw_ref, y_ref, sout_ref, S_ref, *, from_state):
    hg = pl.program_id(0)
    c = pl.program_id(1)
    L = q_ref.shape[0]
    nh = RET_HEADS_PER_STEP
    hs = range(nh)
    dot = functools.partial(jnp.dot, preferred_element_type=f32)

    if from_state:
        S_ref[...] = s0_ref[0]
    else:
        @pl.when(c == 0)
        def _():
            S_ref[...] = jnp.zeros_like(S_ref)

    lg = [lg_ref[hg * nh + h] for h in hs]
    q = [q_ref[:, h * RET_DK:(h + 1) * RET_DK] for h in hs]
    k = [k_ref[:, h * RET_DK:(h + 1) * RET_DK] for h in hs]
    v = [v_ref[:, h * RET_DV:(h + 1) * RET_DV] for h in hs]
    row = lax.broadcasted_iota(jnp.int32, (L, L), 0)
    col = lax.broadcasted_iota(jnp.int32, (L, L), 1)
    diff = (row - col).astype(f32)
    idx = lax.broadcasted_iota(jnp.int32, (L, 1), 0).astype(f32)
    decay = [jnp.where(diff >= 0, jnp.exp(lg[h] * jnp.maximum(diff, 0.0)), 0.0) for h in hs]
    xi = [jnp.exp(lg[h] * (idx + 1.0)) for h in hs]
    zeta = [jnp.exp(lg[h] * (L - 1.0 - idx)) for h in hs]

    S = [S_ref[h] for h in hs]
    scores = [lax.dot_general(q[h], k[h], NT_DIMS, preferred_element_type=f32) * decay[h] for h in hs]
    o = [dot(scores[h].astype(bf16), v[h])
         + dot((q[h].astype(f32) * xi[h]).astype(bf16), S[h].astype(bf16)) for h in hs]
    S_new = [jnp.exp(lg[h] * L) * S[h] + dot((k[h].astype(f32) * zeta[h]).T.astype(bf16), v[h])
             for h in hs]
    for h in hs:
        S_ref[h] = S_new[h]
        sout_ref[0, h] = S_new[h]

    cen = [o[h] - jnp.mean(o[h], axis=-1, keepdims=True) for h in hs]
    on = [cen[h] * lax.rsqrt(jnp.mean(cen[h] * cen[h], axis=-1, keepdims=True) + RET_GN_EPS) for h in hs]
    for h in hs:
        sl = slice(h * RET_DV, (h + 1) * RET_DV)
        y_ref[:, sl] = (jax.nn.silu(g_ref[:, sl].astype(f32)) * on[h] * gnw_ref[:, sl]).astype(y_ref.dtype)


def _retention(proj, state, gn_w, lg, row0, nrows, L, from_state, name):
    nsteps = nrows // L
    b0 = row0 // L
    assert nsteps * L == nrows and b0 * L == row0
    nseq = nsteps if from_state else 1
    G = RET_HEADS_PER_STEP
    kw, vw = G * RET_DK, G * RET_DV
    kb = RET_HEADS * RET_DK // kw
    vb = 2 * RET_HEADS * RET_DK // vw
    gb = vb + RET_HEADS * RET_DV // vw
    state_map = (lambda h, c: (c, h, 0, 0)) if from_state else (lambda h, c: (0, h, 0, 0))

    return pl.pallas_call(
        functools.partial(_ret_kernel, from_state=from_state),
        out_shape=(jax.ShapeDtypeStruct((nrows, RET_HEADS * RET_DV), bf16),
                   jax.ShapeDtypeStruct((nseq, RET_HEADS, RET_DK, RET_DV), f32)),
        grid=(RET_HEADS // G, nsteps),
        in_specs=[pl.BlockSpec(memory_space=pltpu.SMEM),
                  pl.BlockSpec((L, kw), lambda h, c: (b0 + c, h)),
                  pl.BlockSpec((L, kw), lambda h, c: (b0 + c, kb + h)),
                  pl.BlockSpec((L, vw), lambda h, c: (b0 + c, vb + h)),
                  pl.BlockSpec((L, vw), lambda h, c: (b0 + c, gb + h)),
                  pl.BlockSpec((1, G, RET_DK, RET_DV), state_map),
                  pl.BlockSpec((1, vw), lambda h, c: (0, h))],
        out_specs=(pl.BlockSpec((L, vw), lambda h, c: (c, h)),
                   pl.BlockSpec((1, G, RET_DK, RET_DV), state_map)),
        scratch_shapes=[pltpu.VMEM((G, RET_DK, RET_DV), f32)],
        compiler_params=_params(("parallel", "arbitrary")),
        name=name,
    )(lg, proj, proj, proj, proj, state, gn_w)


SUBLANES = 8


def _lerpmm_kernel(x_ref, xprev_ref, g_ref, start_ref, mu_ref, w_ref, o_ref, hlast_ref,
                   h_ref, xx_ref, l_ref, *, tiles_per_d, n_big, npc):
    i = pl.program_id(0)
    j = pl.program_id(1)
    tm = x_ref.shape[0]
    cpb = tm // CHUNK

    @pl.when(j == 0)
    def _():
        g = g_ref[...]
        h = _rms(x_ref[...], g)
        h_ref[...] = h
        xx_ref[...] = pltpu.roll(h, 1, 0) - h
        prev = _rms(xprev_ref[...], g)[SUBLANES - 1:SUBLANES, :]
        xx_ref[0:1, :] = prev - h[0:1, :]
        for ci in range(cpb):
            gc = i * cpb + ci
            r0 = ci * CHUNK

            @pl.when(jnp.logical_or(gc == 0, gc >= npc))
            def _(ci=ci, r0=r0):
                xx_ref[r0:r0 + 1, :] = start_ref[ci] - h_ref[r0:r0 + 1, :]

            hlast_ref[ci] = h[r0 + CHUNK - 1:r0 + CHUNK, :]

    @pl.when(jnp.logical_or(j % tiles_per_d == 0, j >= n_big))
    def _():
        l_ref[...] = (h_ref[...] + xx_ref[...] * mu_ref[0]).astype(bf16)

    o_ref[...] = jnp.dot(l_ref[...], w_ref[...], preferred_element_type=f32)


def _lerp_matmul(x, g, starts, mu_tiles, wcat, tn, tiles_per_d, n_big, npc):
    T, D = x.shape
    N = wcat.shape[1]
    tm = _pick(T, (512, 256, 128, 64))
    cpb = tm // CHUNK
    return pl.pallas_call(
        functools.partial(_lerpmm_kernel, tiles_per_d=tiles_per_d, n_big=n_big, npc=npc),
        out_shape=(jax.ShapeDtypeStruct((T, N), f32),
                   jax.ShapeDtypeStruct((T // CHUNK, 1, D), f32)),
        grid=(T // tm, N // tn),
        in_specs=[pl.BlockSpec((tm, D), lambda i, j: (i, 0)),
                  pl.BlockSpec((SUBLANES, D), lambda i, j: (jnp.maximum(i * (tm // SUBLANES) - 1, 0), 0)),
                  pl.BlockSpec((1, D), lambda i, j: (0, 0)),
                  pl.BlockSpec((cpb, 1, D), lambda i, j: (i, 0, 0)),
                  pl.BlockSpec((1, 1, D), lambda i, j: (j, 0, 0)),
                  pl.BlockSpec((D, tn), lambda i, j: (0, j))],
        out_specs=(pl.BlockSpec((tm, tn), lambda i, j: (i, j)),
                   pl.BlockSpec((cpb, 1, D), lambda i, j: (i, 0, 0))),
        scratch_shapes=[pltpu.VMEM((tm, D), f32), pltpu.VMEM((tm, D), f32), pltpu.VMEM((tm, D), bf16)],
        compiler_params=_params(("parallel", "arbitrary")),
        name="rwkv_proj",
    )(x, x, g, starts, mu_tiles, wcat)


def _softplus(z):
    return jnp.maximum(z, 0.0) + jnp.log(1.0 + jnp.exp(-jnp.abs(z)))


SEG_TILE = 256


def _head_sum(x, split):
    rows, D = x.shape
    r = lax.broadcasted_iota(jnp.int32, (SEG_TILE, SEG_TILE), 0) // RWKV_HS
    c = lax.broadcasted_iota(jnp.int32, (SEG_TILE, SEG_TILE), 1) // RWKV_HS
    ones = (r == c).astype(bf16)
    out = []
    for j in range(D // SEG_TILE):
        blk = x[:, j * SEG_TILE:(j + 1) * SEG_TILE]
        hi = blk.astype(bf16)
        s = jnp.dot(hi, ones, preferred_element_type=f32)
        if split:
            lo = (blk - hi.astype(f32)).astype(bf16)
            s = s + jnp.dot(lo, ones, preferred_element_type=f32)
        out.append(s)
    return jnp.concatenate(out, axis=1)


def _chunk_cumsum(x):
    rows = x.shape[0]
    row = lax.broadcasted_iota(jnp.int32, (rows, rows), 0)
    col = lax.broadcasted_iota(jnp.int32, (rows, rows), 1)
    tri = jnp.logical_and((row // CHUNK) == (col // CHUNK), row >= col).astype(bf16)
    hi = x.astype(bf16)
    rest = x - hi.astype(f32)
    mid = rest.astype(bf16)
    lo = (rest - mid.astype(f32)).astype(bf16)
    dot = functools.partial(jnp.dot, preferred_element_type=f32)
    return dot(tri, hi) + dot(tri, mid) + dot(tri, lo)


def _rwkv_prep_kernel(p_ref, r_ref, k_ref, v_ref, w2_ref, a2_ref, g2_ref, w0_ref, a0_ref,
                      kk_ref, ka_ref, rk_ref,
                      at_ref, rt_ref, bt_ref, kt_ref, bh_ref, kh_ref, v16_ref, gl_ref, g_ref, bonus_ref):
    P = LORA_PAD
    tm = r_ref.shape[0]
    pw = jnp.tanh(p_ref[:, :P]).astype(bf16)
    pa = p_ref[:, P:2 * P].astype(bf16)
    pg = jax.nn.sigmoid(p_ref[:, 2 * P:]).astype(bf16)
    wl = w0_ref[...] + jnp.dot(pw, w2_ref[...], preferred_element_type=f32)
    lw = -jnp.exp(-_softplus(-wl) - 0.5)
    a = jax.nn.sigmoid(a0_ref[...] + jnp.dot(pa, a2_ref[...], preferred_element_type=f32))
    g_ref[...] = jnp.dot(pg, g2_ref[...], preferred_element_type=f32)

    cum = _chunk_cumsum(lw)
    tot = jnp.concatenate(
        [jnp.broadcast_to(cum[(ci + 1) * CHUNK - 1:(ci + 1) * CHUNK, :], (CHUNK, cum.shape[1]))
         for ci in range(tm // CHUNK)], axis=0)

    r, k, v = r_ref[...], k_ref[...], v_ref[...]
    kk = k * kk_ref[...]
    kk = kk / jnp.maximum(jnp.sqrt(_head_sum(kk * kk, True)), 1e-12)
    kmod = k * (1.0 + (a - 1.0) * ka_ref[...])
    beta = kk * a
    e_neg = jnp.exp(-cum)
    e_end = jnp.exp(tot - cum)
    at_ref[...] = (-kk * jnp.exp(cum - lw)).astype(bf16)
    rt_ref[...] = (r * jnp.exp(cum)).astype(bf16)
    bt_ref[...] = (beta * e_neg).astype(bf16)
    kt_ref[...] = (kmod * e_neg).astype(bf16)
    bh_ref[...] = (beta * e_end).astype(bf16)
    kh_ref[...] = (kmod * e_end).astype(bf16)
    v16_ref[...] = v.astype(bf16)
    bonus_ref[...] = _head_sum(r * kmod * rk_ref[...], True) * v
    for ci in range(tm // CHUNK):
        gl_ref[ci] = jnp.exp(tot[ci * CHUNK:ci * CHUNK + 1, :])


def _rwkv_prep(proj, lora_block, w2p, a2p, g2p, w0, a0, k_k, k_a, r_k):
    T = proj.shape[0]
    D = w2p.shape[1]
    P = LORA_PAD
    tm = _pick(T, (128, 64))
    wspec = pl.BlockSpec((P, D), lambda i: (0, 0))
    vspec = pl.BlockSpec((1, D), lambda i: (0, 0))
    ospec = pl.BlockSpec((tm, D), lambda i: (i, 0))
    o16 = jax.ShapeDtypeStruct((T, D), bf16)
    o32 = jax.ShapeDtypeStruct((T, D), f32)
    return pl.pallas_call(
        _rwkv_prep_kernel,
        out_shape=(o16,) * 7 + (jax.ShapeDtypeStruct((T // CHUNK, 1, D), f32), o32, o32),
        grid=(T // tm,),
        in_specs=[pl.BlockSpec((tm, 3 * P), lambda i: (i, lora_block)),
                  pl.BlockSpec((tm, D), lambda i: (i, 0)),
                  pl.BlockSpec((tm, D), lambda i: (i, 1)),
                  pl.BlockSpec((tm, D), lambda i: (i, 2)),
                  wspec, wspec, wspec, vspec, vspec, vspec, vspec, vspec],
        out_specs=(ospec,) * 7 + (pl.BlockSpec((tm // CHUNK, 1, D), lambda i: (i, 0, 0)), ospec, ospec),
        compiler_params=_params(("parallel",)),
        name="rwkv_prep",
    )(proj, proj, proj, proj, w2p, a2p, g2p, w0, a0, k_k, k_a, r_k)


def _rwkv_heads_chunk(heads):
    L, N = heads[0][6].shape
    nh = len(heads)
    dot = functools.partial(jnp.dot, preferred_element_type=f32)
    row = lax.broadcasted_iota(jnp.int32, (L, L), 0)
    col = lax.broadcasted_iota(jnp.int32, (L, L), 1)
    incl = row >= col
    strict = row > col
    lhs = [jnp.concatenate([h[0], h[1]], axis=0) for h in heads]
    rhs = [jnp.concatenate([h[2], h[3]], axis=0) for h in heads]
    G = [lax.dot_general(lhs[i], rhs[i], NT_DIMS, preferred_element_type=f32) for i in range(nh)]
    LS = [lax.dot_general(lhs[i], heads[i][7].astype(bf16), NT_DIMS, preferred_element_type=f32)
          for i in range(nh)]
    n = [jnp.where(strict, g[:L, :L], 0.0) for g in G]
    a_ak = [jnp.where(strict, g[:L, L:], 0.0).astype(bf16) for g in G]
    a_r = [jnp.concatenate([jnp.where(incl, g[L:, :L], 0.0), jnp.where(incl, g[L:, L:], 0.0)],
                           axis=1).astype(bf16) for g in G]
    x = [LS[i][:L] + dot(a_ak[i], heads[i][6]) for i in range(nh)]
    steps = max(1, (L - 1).bit_length())
    for s in range(steps):
        if s < steps - 1:
            z = [dot(n[i].astype(bf16), jnp.concatenate([x[i], n[i]], axis=1).astype(bf16))
                 for i in range(nh)]
            x = [x[i] + z[i][:, :N] for i in range(nh)]
            n = [z[i][:, N:] for i in range(nh)]
        else:
            x = [x[i] + dot(n[i].astype(bf16), x[i].astype(bf16)) for i in range(nh)]
    uv = [jnp.concatenate([x[i], heads[i][6].astype(f32)], axis=0) for i in range(nh)]
    o = [LS[i][L:] + dot(a_r[i], uv[i].astype(bf16)) for i in range(nh)]
    S_new = [heads[i][7] * heads[i][8]
             + dot(uv[i].T.astype(bf16), jnp.concatenate([heads[i][4], heads[i][5]], axis=0))
             for i in range(nh)]
    return list(zip(o, S_new))


def _rwkv_core_kernel(at_ref, rt_ref, bt_ref, kt_ref, bh_ref, kh_ref, v_ref, gl_ref, s0_ref,
                      bonus_ref, g_ref, lnw_ref, lnb_ref, y_ref, sout_ref, S_ref, *, npc, nhead):
    c = pl.program_id(1)
    N = RWKV_HS

    @pl.when(c == 0)
    def _():
        S_ref[...] = jnp.zeros_like(S_ref)

    @pl.when(c >= npc)
    def _():
        S_ref[...] = s0_ref[0]

    gl = gl_ref[0]
    heads = []
    for hh in range(nhead):
        sl = slice(hh * N, (hh + 1) * N)
        heads.append((at_ref[:, sl], rt_ref[:, sl], bt_ref[:, sl], kt_ref[:, sl], bh_ref[:, sl],
                      kh_ref[:, sl], v_ref[:, sl], S_ref[hh], gl[:, sl]))
    res = _rwkv_heads_chunk(heads)
    for hh in range(nhead):
        S_ref[hh] = res[hh][1]
    outs = [o for o, _ in res]
    cen = [o - jnp.mean(o, axis=-1, keepdims=True) for o in outs]
    nrm = [d * lax.rsqrt(jnp.mean(d * d, axis=-1, keepdims=True) + RWKV_GN_EPS) for d in cen]
    on = jnp.concatenate(nrm, axis=1) * lnw_ref[...] + lnb_ref[...]
    y_ref[...] = ((on + bonus_ref[...]) * g_ref[...]).astype(y_ref.dtype)

    @pl.when(jnp.logical_or(c == npc - 1, c >= npc))
    def _():
        sout_ref[0] = S_ref[...]


RWKV_HEADS_PER_STEP = 32


def _rwkv_core(ops, v16, gl, state, bonus, g, ln_w, ln_b, npc, nchunks):
    T, D = v16.shape
    N = RWKV_HS
    nh = D // N
    nb = state.shape[0]
    G = RWKV_HEADS_PER_STEP
    W = G * N
    tok = pl.BlockSpec((CHUNK, W), lambda p, c: (c, p))
    pspec = pl.BlockSpec((1, W), lambda p, c: (0, p))

    def sin_map(p, c):
        return (jnp.maximum(c - npc, 0), p, 0, 0)

    def sout_map(p, c):
        return (jnp.maximum(c - npc + 1, 0), p, 0, 0)

    return pl.pallas_call(
        functools.partial(_rwkv_core_kernel, npc=npc, nhead=G),
        out_shape=(jax.ShapeDtypeStruct((T, D), bf16),
                   jax.ShapeDtypeStruct((nb + 1, nh, N, N), f32)),
        grid=(nh // G, nchunks),
        in_specs=[tok] * 7 + [pl.BlockSpec((1, 1, W), lambda p, c: (c, 0, p)),
                              pl.BlockSpec((1, G, N, N), sin_map),
                              tok, tok, pspec, pspec],
        out_specs=(tok, pl.BlockSpec((1, G, N, N), sout_map)),
        scratch_shapes=[pltpu.VMEM((G, N, N), f32)],
        compiler_params=_params(("parallel", "arbitrary")),
        name="rwkv_core",
    )(*ops, v16, gl, state, bonus, g, ln_w, ln_b)


def _positions(n_prompt, n_b, n_s):
    return jnp.concatenate([jnp.arange(n_prompt), jnp.tile(PAST_LEN + jnp.arange(n_s), n_b)]).astype(f32)


def _attn_tables(pos):
    half = ROT_DIM // 2
    inv = ROPE_THETA ** (-jnp.arange(half, dtype=f32) / half)
    ang = pos[:, None] * inv[None, :]
    cos, sin = jnp.cos(ang), jnp.sin(ang)
    T = pos.shape[0]
    zeros = jnp.zeros((T, half), f32)
    rest = HEAD_DIM - ROT_DIM
    c = jnp.concatenate([cos, cos, jnp.ones((T, rest), f32)], axis=1)
    s1 = jnp.concatenate([zeros, sin, jnp.zeros((T, rest), f32)], axis=1)
    s2 = jnp.concatenate([-sin, zeros, jnp.zeros((T, rest), f32)], axis=1)
    rep = 256 // HEAD_DIM
    return tuple(jnp.tile(t, (1, rep)) for t in (c, s1, s2))


def _ret_tables(pos):
    half = RET_DK // 2
    inv = RET_THETA ** (-jnp.arange(half, dtype=f32) / half)
    ang = pos[:, None] * inv[None, :]
    return jnp.cos(ang), jnp.sin(ang)


def _pad_cols(w, n):
    return jnp.pad(w, ((0, 0), (0, n - w.shape[1])))


def _pad_rows(w, n):
    return jnp.pad(w, ((0, n - w.shape[0]), (0, 0)))


def kernel(x_prompt, x_sample, cache_attn_k, cache_attn_v, state_ret, state_rwkv, state_rwkv_shift,
           norm_mix, norm_mlp, norm_final,
           attn_w_qkv, attn_sinks, attn_w_o,
           ret_w_in, ret_gn_w, ret_w_o,
           rwkv_mu, rwkv_w_rkv, rwkv_w_o, rwkv_w0, rwkv_w1, rwkv_w2, rwkv_a0, rwkv_a1, rwkv_a2,
           rwkv_g1, rwkv_g2, rwkv_k_k, rwkv_k_a, rwkv_r_k, rwkv_ln_w, rwkv_ln_b,
           mlp_w_up, mlp_w_down):
    bp, tp, D = x_prompt.shape
    nb, ns, _ = x_sample.shape
    assert bp == 1 and ns == CHUNK and tp % CHUNK == 0
    depth = norm_mix.shape[0]
    npc = tp // CHUNK
    nchunks = npc + nb
    T = tp + nb * ns
    qd = ATTN_HEADS * HEAD_DIM
    kvd = ATTN_KV_HEADS * HEAD_DIM

    x = jnp.concatenate([x_prompt.reshape(tp, D), x_sample.reshape(nb * ns, D)], axis=0)
    pos = _positions(tp, nb, ns)
    attn_tabs = _attn_tables(pos)
    ret_tabs = _ret_tables(pos)
    lg = jnp.log1p(-jnp.exp2(-5.0 - jnp.arange(RET_HEADS, dtype=f32)))
    tm_proj = _pick(T, (512, 256, 128, 64))
    tm_ret = _pick(T, (1024, 512, 256, 128, 64))
    w_up16 = mlp_w_up.astype(bf16)
    w_down16 = mlp_w_down.astype(bf16)

    kp_l, vp_l, ks_l, vs_l, rp_l, rs_l, wp_l, ws_l, shp_l, shs_l = ([] for _ in range(10))
    for i in range(depth):
        j, kind = divmod(i, 3)
        g_mix = norm_mix[i][None, :]
        if kind == 0:
            nsub = (qd + 2 * kvd) // (2 * EPI_COLS)
            routes = [[("q", 0, s * EPI_COLS) for s in range(nsub)],
                      [("q", 0, s * EPI_COLS) for s in range(nsub, qd // EPI_COLS)]
                      + [("k", 1, 0), ("plain", 1, EPI_COLS)]]
            q, kv = _norm_matmul(
                x, g_mix, attn_w_qkv[j].astype(bf16), attn_tabs, routes, _epi_attn,
                [jax.ShapeDtypeStruct((T, qd), bf16), jax.ShapeDtypeStruct((T, 2 * kvd), f32)],
                [pl.BlockSpec((tm_proj, qd), lambda i, jj: (i, 0)),
                 pl.BlockSpec((tm_proj, 2 * kvd), lambda i, jj: (i, 0))],
                tm_proj, "attn_qkv")
            k_new = kv[:, :kvd]
            v_new = kv[:, kvd:]

            def ext(new, cache):
                samp = jnp.concatenate([cache.reshape(nb, WINDOW, kvd), new[tp:].reshape(nb, ns, kvd)], axis=1)
                rows = jnp.concatenate([jnp.zeros((WINDOW, kvd), f32), new[:tp],
                                        samp.reshape(nb * (WINDOW + ns), kvd)], axis=0)
                r4 = rows.astype(bf16).reshape(rows.shape[0], ATTN_KV_HEADS, 1, HEAD_DIM)
                dup = jnp.broadcast_to(r4, (rows.shape[0], ATTN_KV_HEADS, 2, HEAD_DIM))
                return dup.reshape(rows.shape[0], 2 * kvd), samp

            kext, k_samp = ext(k_new, cache_attn_k[j])
            vext, v_samp = ext(v_new, cache_attn_v[j])
            o = _attention(q, kext, vext, attn_sinks[j], npc, nchunks)
            x = _matmul_residual(o, attn_w_o[j].astype(bf16), x, "attn_out")
            kp_l.append(k_new[tp - WINDOW:tp].reshape(1, WINDOW, ATTN_KV_HEADS, HEAD_DIM))
            vp_l.append(v_new[tp - WINDOW:tp].reshape(1, WINDOW, ATTN_KV_HEADS, HEAD_DIM))
            ks_l.append(k_samp[:, -WINDOW:].reshape(nb, WINDOW, ATTN_KV_HEADS, HEAD_DIM))
            vs_l.append(v_samp[:, -WINDOW:].reshape(nb, WINDOW, ATTN_KV_HEADS, HEAD_DIM))
        elif kind == 1:
            nq = RET_HEADS * RET_DK // 1024
            nv = RET_HEADS * RET_DV // 1024
            tile = lambda mode: [(mode, 0, s * EPI_COLS) for s in range(1024 // EPI_COLS)]
            routes = [tile("q")] * nq + [tile("k")] * nq + [tile("plain")] * (2 * nv)
            (proj,) = _norm_matmul(
                x, g_mix, ret_w_in[j].astype(bf16), ret_tabs, routes, _epi_ret,
                [jax.ShapeDtypeStruct((T, ret_w_in.shape[2]), bf16)],
                [pl.BlockSpec((tm_ret, 1024), lambda i, jj: (i, jj))], tm_ret, "ret_proj")
            gnw = ret_gn_w[j][None, :]
            lp = _pick(tp, (256, 128, 64))
            y_p, s_p = _retention(proj, state_ret[j], gnw, lg, 0, tp, lp, False, "ret_prompt")
            y_s, s_s = _retention(proj, state_ret[j], gnw, lg, tp, nb * ns, ns, True, "ret_sample")
            x = _matmul_residual2(y_p, y_s, ret_w_o[j].astype(bf16), x, "ret_out")
            rp_l.append(s_p)
            rs_l.append(s_s)
        else:
            starts = jnp.concatenate([jnp.zeros((npc, D), f32), state_rwkv_shift[j]], axis=0)[:, None, :]
            P = LORA_PAD
            tn = 512
            wcat = jnp.concatenate(
                [rwkv_w_rkv[j][0], rwkv_w_rkv[j][1], rwkv_w_rkv[j][2],
                 _pad_cols(rwkv_w1[j], P), _pad_cols(rwkv_a1[j], P), _pad_cols(rwkv_g1[j], P)],
                axis=1).astype(bf16)
            tiles_per_d = D // tn
            mu = rwkv_mu[j]
            mu_tiles = jnp.concatenate(
                [jnp.repeat(mu[jnp.array([0, 2, 3])], tiles_per_d, axis=0), mu[jnp.array([1, 4, 5])]],
                axis=0)[:, None, :]
            proj, h_last = _lerp_matmul(x, g_mix, starts, mu_tiles, wcat, tn, tiles_per_d,
                                        3 * tiles_per_d, npc)
            prep = _rwkv_prep(
                proj, 3 * D // (3 * P),
                _pad_rows(rwkv_w2[j], P).astype(bf16), _pad_rows(rwkv_a2[j], P).astype(bf16),
                _pad_rows(rwkv_g2[j], P).astype(bf16), rwkv_w0[j][None, :], rwkv_a0[j][None, :],
                rwkv_k_k[j][None, :], rwkv_k_a[j][None, :], rwkv_r_k[j].reshape(1, D))
            ops, v16, gl, g, bonus = prep[:6], prep[6], prep[7], prep[8], prep[9]
            y, s_all = _rwkv_core(ops, v16, gl, state_rwkv[j], bonus, g,
                                  rwkv_ln_w[j][None, :], rwkv_ln_b[j][None, :], npc, nchunks)
            x = _matmul_residual(y, rwkv_w_o[j].astype(bf16), x, "rwkv_out")
            wp_l.append(s_all[:1])
            ws_l.append(s_all[1:])
            shp_l.append(h_last[npc - 1])
            shs_l.append(h_last[npc:, 0])
        x = _mlp(x, norm_mlp[i][None, :], w_up16, w_down16, i, norm_final[None, :], i == depth - 1, "mlp")

    y_prompt = x[:tp].reshape(1, tp, D)
    y_sample = x[tp:].reshape(nb, ns, D)
    return (y_prompt, y_sample,
            jnp.stack(kp_l), jnp.stack(vp_l), jnp.stack(ks_l), jnp.stack(vs_l),
            jnp.stack(rp_l), jnp.stack(rs_l),
            jnp.stack(wp_l), jnp.stack(ws_l), jnp.stack(shp_l), jnp.stack(shs_l))
```

```python
import functools

import jax
import jax.numpy as jnp
from jax import lax
from jax.experimental import pallas as pl
from jax.experimental.pallas import tpu as pltpu

f32 = jnp.float32
bf16 = jnp.bfloat16

CHUNK = 64
NORM_EPS = 1e-5
PAST_LEN = 4096

ATTN_HEADS = 32
ATTN_KV_HEADS = 4
ATTN_GROUP = ATTN_HEADS // ATTN_KV_HEADS
HEAD_DIM = 64
WINDOW = 128
ROT_DIM = HEAD_DIM // 4
ROPE_THETA = 500000.0

RET_HEADS = 8
RET_DK = 256
RET_DV = 512
RET_THETA = 10000.0
RET_GN_EPS = 1e-5

RWKV_HS = 64
RWKV_GN_EPS = 64e-5
LORA_PAD = 512

VMEM_LIMIT = 52 * 1024 * 1024

NT_DIMS = (((1,), (1,)), ((), ()))


def _pick(n, cands):
    for c in cands:
        if n % c == 0:
            return c
    raise ValueError(f"no tile for {n} in {cands}")


def _params(sem):
    return pltpu.CompilerParams(dimension_semantics=sem, vmem_limit_bytes=VMEM_LIMIT)


def _rms(x, g):
    return x * lax.rsqrt(jnp.mean(x * x, axis=-1, keepdims=True) + NORM_EPS) * g


def _epi_attn(blk, mode, tabs):
    if mode == "plain":
        return blk
    c_ref, s1_ref, s2_ref = tabs
    out = (blk * c_ref[...] + pltpu.roll(blk, 8, 1) * s1_ref[...]
           + pltpu.roll(blk, 256 - 8, 1) * s2_ref[...])
    return out * (HEAD_DIM ** -0.5) if mode == "q" else out


def _epi_ret(blk, mode, tabs):
    if mode == "plain":
        return blk
    cos_ref, sin_ref = tabs
    c, s = cos_ref[...], sin_ref[...]
    x1, x2 = blk[:, :128], blk[:, 128:]
    out = jnp.concatenate([x1 * c - x2 * s, x2 * c + x1 * s], axis=1)
    return out * (RET_DK ** -0.5) if mode == "k" else out


EPI_COLS = 256


def _normmm_kernel(*refs, routes, ntab, nout, epi):
    x_ref, g_ref, w_ref = refs[:3]
    tabs = refs[3:3 + ntab]
    outs = refs[3 + ntab:3 + ntab + nout]
    xn_ref = refs[3 + ntab + nout]
    j = pl.program_id(1)

    @pl.when(j == 0)
    def _():
        xn_ref[...] = _rms(x_ref[...], g_ref[...]).astype(bf16)

    groups = {}
    for jj, tile_routes in enumerate(routes):
        groups.setdefault(tuple(tile_routes), []).append(jj)
    for tile_routes, jjs in groups.items():
        cond = functools.reduce(jnp.logical_or, [j == jj for jj in jjs])

        @pl.when(cond)
        def _(tile_routes=tile_routes):
            for s, (mode, oi, col) in enumerate(tile_routes):
                acc = jnp.dot(xn_ref[...], w_ref[:, s * EPI_COLS:(s + 1) * EPI_COLS],
                              preferred_element_type=f32)
                outs[oi][:, col:col + EPI_COLS] = epi(acc, mode, tabs).astype(outs[oi].dtype)


def _norm_matmul(x, g, w, tabs, routes, epi, out_shapes, out_specs, tm, name):
    T, D = x.shape
    N = w.shape[1]
    ntile = len(routes)
    tn = N // ntile
    assert tn * ntile == N and tn == EPI_COLS * len(routes[0])
    tab_specs = [pl.BlockSpec((tm, t.shape[1]), lambda i, j: (i, 0)) for t in tabs]
    return pl.pallas_call(
        functools.partial(_normmm_kernel, routes=routes, ntab=len(tabs), nout=len(out_shapes), epi=epi),
        out_shape=tuple(out_shapes),
        grid=(T // tm, ntile),
        in_specs=[pl.BlockSpec((tm, D), lambda i, j: (i, 0)),
                  pl.BlockSpec((1, D), lambda i, j: (0, 0)),
                  pl.BlockSpec((D, tn), lambda i, j: (0, j))] + tab_specs,
        out_specs=tuple(out_specs),
        scratch_shapes=[pltpu.VMEM((tm, D), bf16)],
        compiler_params=_params(("parallel", "arbitrary")),
        name=name,
    )(x, g, w, *tabs)


def _mmres_kernel(a_ref, w_ref, r_ref, o_ref):
    o_ref[...] = r_ref[...] + jnp.dot(a_ref[...], w_ref[...], preferred_element_type=f32)


def _resident(shape):
    return pl.BlockSpec(shape, lambda *_: (0,) * len(shape), pipeline_mode=pl.Buffered(1))


def _matmul_residual(a, w, res, name):
    T, K = a.shape
    N = w.shape[1]
    tm = _pick(T, (512, 256, 128, 64))
    return pl.pallas_call(
        _mmres_kernel,
        out_shape=jax.ShapeDtypeStruct((T, N), f32),
        grid=(T // tm,),
        in_specs=[pl.BlockSpec((tm, K), lambda i: (i, 0)),
                  _resident((K, N)),
                  pl.BlockSpec((tm, N), lambda i: (i, 0))],
        out_specs=pl.BlockSpec((tm, N), lambda i: (i, 0)),
        compiler_params=_params(("parallel",)),
        name=name,
    )(a, w, res)


def _mmres2_kernel(ap_ref, as_ref, w_ref, r_ref, o_ref, *, np_blocks):
    i = pl.program_id(0)

    @pl.when(i < np_blocks)
    def _():
        o_ref[...] = r_ref[...] + jnp.dot(ap_ref[...], w_ref[...], preferred_element_type=f32)

    @pl.when(i >= np_blocks)
    def _():
        o_ref[...] = r_ref[...] + jnp.dot(as_ref[...], w_ref[...], preferred_element_type=f32)


def _matmul_residual2(a_p, a_s, w, res, name):
    tp, K = a_p.shape
    ts = a_s.shape[0]
    N = w.shape[1]
    tm = _pick(ts, (256, 128, 64))
    assert tp % tm == 0
    np_blocks = tp // tm
    return pl.pallas_call(
        functools.partial(_mmres2_kernel, np_blocks=np_blocks),
        out_shape=jax.ShapeDtypeStruct((tp + ts, N), f32),
        grid=((tp + ts) // tm,),
        in_specs=[pl.BlockSpec((tm, K), lambda i: (jnp.minimum(i, np_blocks - 1), 0)),
                  pl.BlockSpec((tm, K), lambda i: (jnp.maximum(i - np_blocks, 0), 0)),
                  _resident((K, N)),
                  pl.BlockSpec((tm, N), lambda i: (i, 0))],
        out_specs=pl.BlockSpec((tm, N), lambda i: (i, 0)),
        compiler_params=_params(("parallel",)),
        name=name,
    )(a_p, a_s, w, res)


def _mlp_kernel(x_ref, g_ref, wu_ref, wd_ref, gf_ref, *rest, nk, split_at):
    if split_at is None:
        acc_ref, xn_ref = rest
    else:
        op_ref, os_ref, xn_ref, acc_ref = rest
    i = pl.program_id(0)
    k = pl.program_id(1)

    @pl.when(k == 0)
    def _():
        x = x_ref[...]
        xn_ref[...] = _rms(x, g_ref[...]).astype(bf16)
        acc_ref[...] = x

    h = jnp.dot(xn_ref[...], wu_ref[...], preferred_element_type=f32)
    h = jnp.square(jnp.maximum(h, 0.0)).astype(bf16)
    acc_ref[...] += jnp.dot(h, wd_ref[...], preferred_element_type=f32)

    if split_at is not None:
        @pl.when(jnp.logical_and(k == nk - 1, i < split_at))
        def _():
            op_ref[...] = _rms(acc_ref[...], gf_ref[...])

        @pl.when(jnp.logical_and(k == nk - 1, i >= split_at))
        def _():
            os_ref[...] = _rms(acc_ref[...], gf_ref[...])


def _mlp(x, g, w_up, w_down, layer, g_final, split_rows, name):
    T, D = x.shape
    F = w_up.shape[2]
    tm = _pick(T if split_rows is None else T - split_rows, (512, 256, 128, 64))
    tf = _pick(F, (1024, 512))
    nk = F // tf
    if split_rows is None:
        split_at = None
        out_shape = jax.ShapeDtypeStruct((T, D), f32)
        out_specs = pl.BlockSpec((tm, D), lambda i, k: (i, 0))
        scratch = [pltpu.VMEM((tm, D), bf16)]
    else:
        assert split_rows % tm == 0
        split_at = split_rows // tm
        out_shape = (jax.ShapeDtypeStruct((split_rows, D), f32), jax.ShapeDtypeStruct((T - split_rows, D), f32))
        out_specs = (pl.BlockSpec((tm, D), lambda i, k: (jnp.minimum(i, split_at - 1), 0)),
                     pl.BlockSpec((tm, D), lambda i, k: (jnp.maximum(i - split_at, 0), 0)))
        scratch = [pltpu.VMEM((tm, D), bf16), pltpu.VMEM((tm, D), f32)]
    return pl.pallas_call(
        functools.partial(_mlp_kernel, nk=nk, split_at=split_at),
        out_shape=out_shape,
        grid=(T // tm, nk),
        in_specs=[pl.BlockSpec((tm, D), lambda i, k: (i, 0)),
                  pl.BlockSpec((1, D), lambda i, k: (0, 0)),
                  pl.BlockSpec((None, D, tf), lambda i, k: (layer, 0, k)),
                  pl.BlockSpec((None, tf, D), lambda i, k: (layer, k, 0)),
                  pl.BlockSpec((1, D), lambda i, k: (0, 0))],
        out_specs=out_specs,
        scratch_shapes=scratch,
        compiler_params=_params(("arbitrary", "arbitrary")),
        name=name,
    )(x, g, w_up, w_down, g_final)


def _attn_kernel(sb_ref, q_ref, k0_ref, k1_ref, k2_ref, v0_ref, v1_ref, v2_ref, o_ref, *, npc):
    c = pl.program_id(0)
    nkeys = 3 * CHUNK
    ncols = nkeys + CHUNK
    PW = 2 * HEAD_DIM
    pairs = ATTN_GROUP // 2
    col_blk = lax.broadcasted_iota(jnp.int32, (1, ncols), 1) // CHUNK
    valid = jnp.logical_or(jnp.logical_or(c >= npc, c + col_blk >= 2), col_blk >= 3)
    bias = jnp.where(valid, 0.0, -jnp.inf).astype(f32)
    zpad = jnp.zeros((CHUNK, ATTN_KV_HEADS * PW), bf16)
    kcat = jnp.concatenate([k0_ref[...], k1_ref[...], k2_ref[...], zpad], axis=0)
    vcat = jnp.concatenate([v0_ref[...], v1_ref[...], v2_ref[...], zpad], axis=0)
    lo_half = lax.broadcasted_iota(jnp.int32, (1, PW), 1) < HEAD_DIM
    ones = jnp.ones((ncols, PW), bf16)
    kvs = range(ATTN_KV_HEADS)

    def stacked_q(kv):
        parts = []
        for p in range(pairs):
            col = (kv * pairs + p) * PW
            qp = q_ref[:, col:col + PW]
            parts += [jnp.where(lo_half, qp, 0.0), jnp.where(lo_half, 0.0, qp)]
        return jnp.concatenate(parts, axis=0).astype(bf16)

    s = [lax.dot_general(stacked_q(kv), kcat[:, kv * PW:(kv + 1) * PW], NT_DIMS,
                         preferred_element_type=f32) + sb_ref[kv] + bias for kv in kvs]
    p = [jnp.exp(s[kv] - jnp.max(s[kv], axis=-1, keepdims=True)).astype(bf16) for kv in kvs]
    oa = [jnp.dot(p[kv], jnp.concatenate([vcat[:, kv * PW:(kv + 1) * PW], ones], axis=1),
                  preferred_element_type=f32) for kv in kvs]
    for kv in kvs:
        on = oa[kv][:, :PW] / oa[kv][:, PW:]
        for pi in range(pairs):
            r0 = 2 * pi * CHUNK
            blk = jnp.where(lo_half, on[r0:r0 + CHUNK], on[r0 + CHUNK:r0 + 2 * CHUNK])
            col = (kv * pairs + pi) * PW
            o_ref[:, col:col + PW] = blk.astype(o_ref.dtype)


def _attention(qkv, kext, vext, sinks, npc, nchunks):
    T = qkv.shape[0]
    qd = ATTN_HEADS * HEAD_DIM
    kvd = kext.shape[1]

    def kv_spec(j):
        def imap(c):
            return (jnp.where(c < npc, c, npc + 2 + 3 * (c - npc)) + j, 0)
        return pl.BlockSpec((CHUNK, kvd), imap)

    nkeys = 3 * CHUNK
    col = jnp.arange(nkeys + CHUNK)[None, None, :]
    sink_rows = jnp.repeat(sinks.astype(f32).reshape(ATTN_KV_HEADS, ATTN_GROUP), CHUNK, axis=1)[:, :, None]
    sink_bias = jnp.where(col < nkeys, 0.0, jnp.where(col == nkeys, sink_rows, -jnp.inf)).astype(f32)

    return pl.pallas_call(
        functools.partial(_attn_kernel, npc=npc),
        out_shape=jax.ShapeDtypeStruct((T, qd), bf16),
        grid=(nchunks,),
        in_specs=[pl.BlockSpec(sink_bias.shape, lambda c: (0, 0, 0)),
                  pl.BlockSpec((CHUNK, qd), lambda c: (c, 0))]
                 + [kv_spec(j) for j in range(3)] + [kv_spec(j) for j in range(3)],
        out_specs=pl.BlockSpec((CHUNK, qd), lambda c: (c, 0)),
        compiler_params=_params(("parallel",)),
        name="attn_core",
    )(sink_bias, qkv, kext, kext, kext, vext, vext, vext)


RET_HEADS_PER_STEP = 4


def _ret_kernel(lg_ref, q_ref, k_ref, v_ref, g_ref, s0_ref, gnw_ref, y_ref, sout_ref, S_ref, *, from_state):
    hg = pl.program_id(0)
    c = pl.program_id(1)
    L = q_ref.shape[0]
    nh = RET_HEADS_PER_STEP
    hs = range(nh)
    dot = functools.partial(jnp.dot, preferred_element_type=f32)

    if from_state:
        S_ref[...] = s0_ref[0]
    else:
        @pl.when(c == 0)
        def _():
            S_ref[...] = jnp.zeros_like(S_ref)

    lg = [lg_ref[hg * nh + h] for h in hs]
    q = [q_ref[:, h * RET_DK:(h + 1) * RET_DK] for h in hs]
    k = [k_ref[:, h * RET_DK:(h + 1) * RET_DK] for h in hs]
    v = [v_ref[:, h * RET_DV:(h + 1) * RET_DV] for h in hs]
    row = lax.broadcasted_iota(jnp.int32, (L, L), 0)
    col = lax.broadcasted_iota(jnp.int32, (L, L), 1)
    diff = (row - col).astype(f32)
    idx = lax.broadcasted_iota(jnp.int32, (L, 1), 0).astype(f32)
    decay = [jnp.where(diff >= 0, jnp.exp(lg[h] * jnp.maximum(diff, 0.0)), 0.0) for h in hs]
    xi = [jnp.exp(lg[h] * (idx + 1.0)) for h in hs]
    zeta = [jnp.exp(lg[h] * (L - 1.0 - idx)) for h in hs]

    S = [S_ref[h] for h in hs]
    scores = [lax.dot_general(q[h], k[h], NT_DIMS, preferred_element_type=f32) * decay[h] for h in hs]
    o = [dot(scores[h].astype(bf16), v[h])
         + dot((q[h].astype(f32) * xi[h]).astype(bf16), S[h].astype(bf16)) for h in hs]
    S_new = [jnp.exp(lg[h] * L) * S[h] + dot((k[h].astype(f32) * zeta[h]).T.astype(bf16), v[h])
             for h in hs]
    for h in hs:
        S_ref[h] = S_new[h]
        sout_ref[0, h] = S_new[h]

    cen = [o[h] - jnp.mean(o[h], axis=-1, keepdims=True) for h in hs]
    on = [cen[h] * lax.rsqrt(jnp.mean(cen[h] * cen[h], axis=-1, keepdims=True) + RET_GN_EPS) for h in hs]
    for h in hs:
        sl = slice(h * RET_DV, (h + 1) * RET_DV)
        y_ref[:, sl] = (jax.nn.silu(g_ref[:, sl].astype(f32)) * on[h] * gnw_ref[:, sl]).astype(y_ref.dtype)


def _retention(proj, state, gn_w, lg, row0, nrows, L, from_state, name):
    nsteps = nrows // L
    b0 = row0 // L
    assert nsteps * L == nrows and b0 * L == row0
    nseq = nsteps if from_state else 1
    G = RET_HEADS_PER_STEP
    kw, vw = G * RET_DK, G * RET_DV
    kb = RET_HEADS * RET_DK // kw
    vb = 2 * RET_HEADS * RET_DK // vw
    gb = vb + RET_HEADS * RET_DV // vw
    state_map = (lambda h, c: (c, h, 0, 0)) if from_state else (lambda h, c: (0, h, 0, 0))

    return pl.pallas_call(
        functools.partial(_ret_kernel, from_state=from_state),
        out_shape=(jax.ShapeDtypeStruct((nrows, RET_HEADS * RET_DV), bf16),
                   jax.ShapeDtypeStruct((nseq, RET_HEADS, RET_DK, RET_DV), f32)),
        grid=(RET_HEADS // G, nsteps),
        in_specs=[pl.BlockSpec(memory_space=pltpu.SMEM),
                  pl.BlockSpec((L, kw), lambda h, c: (b0 + c, h)),
                  pl.BlockSpec((L, kw), lambda h, c: (b0 + c, kb + h)),
                  pl.BlockSpec((L, vw), lambda h, c: (b0 + c, vb + h)),
                  pl.BlockSpec((L, vw), lambda h, c: (b0 + c, gb + h)),
                  pl.BlockSpec((1, G, RET_DK, RET_DV), state_map),
                  pl.BlockSpec((1, vw), lambda h, c: (0, h))],
        out_specs=(pl.BlockSpec((L, vw), lambda h, c: (c, h)),
                   pl.BlockSpec((1, G, RET_DK, RET_DV), state_map)),
        scratch_shapes=[pltpu.VMEM((G, RET_DK, RET_DV), f32)],
        compiler_params=_params(("parallel", "arbitrary")),
        name=name,
    )(lg, proj, proj, proj, proj, state, gn_w)


SUBLANES = 8


def _lerpmm_kernel(x_ref, xprev_ref, g_ref, start_ref, mu_ref, w_ref, o_ref, hlast_ref,
                   h_ref, xx_ref, l_ref, *, tiles_per_d, n_big, npc):
    i = pl.program_id(0)
    j = pl.program_id(1)
    tm = x_ref.shape[0]
    cpb = tm // CHUNK

    @pl.when(j == 0)
    def _():
        g = g_ref[...]
        h = _rms(x_ref[...], g)
        h_ref[...] = h
        xx_ref[...] = pltpu.roll(h, 1, 0) - h
        prev = _rms(xprev_ref[...], g)[SUBLANES - 1:SUBLANES, :]
        xx_ref[0:1, :] = prev - h[0:1, :]
        for ci in range(cpb):
            gc = i * cpb + ci
            r0 = ci * CHUNK

            @pl.when(jnp.logical_or(gc == 0, gc >= npc))
            def _(ci=ci, r0=r0):
                xx_ref[r0:r0 + 1, :] = start_ref[ci] - h_ref[r0:r0 + 1, :]

            hlast_ref[ci] = h[r0 + CHUNK - 1:r0 + CHUNK, :]

    @pl.when(jnp.logical_or(j % tiles_per_d == 0, j >= n_big))
    def _():
        l_ref[...] = (h_ref[...] + xx_ref[...] * mu_ref[0]).astype(bf16)

    o_ref[...] = jnp.dot(l_ref[...], w_ref[...], preferred_element_type=f32).astype(o_ref.dtype)


def _lerp_matmul(x, g, starts, mu_tiles, wcat, tn, tiles_per_d, n_big, npc):
    T, D = x.shape
    N = wcat.shape[1]
    tm = _pick(T, (512, 256, 128, 64))
    cpb = tm // CHUNK
    return pl.pallas_call(
        functools.partial(_lerpmm_kernel, tiles_per_d=tiles_per_d, n_big=n_big, npc=npc),
        out_shape=(jax.ShapeDtypeStruct((T, N), bf16),
                   jax.ShapeDtypeStruct((T // CHUNK, 1, D), f32)),
        grid=(T // tm, N // tn),
        in_specs=[pl.BlockSpec((tm, D), lambda i, j: (i, 0)),
                  pl.BlockSpec((SUBLANES, D), lambda i, j: (jnp.maximum(i * (tm // SUBLANES) - 1, 0), 0)),
                  pl.BlockSpec((1, D), lambda i, j: (0, 0)),
                  pl.BlockSpec((cpb, 1, D), lambda i, j: (i, 0, 0)),
                  pl.BlockSpec((1, 1, D), lambda i, j: (j, 0, 0)),
                  pl.BlockSpec((D, tn), lambda i, j: (0, j))],
        out_specs=(pl.BlockSpec((tm, tn), lambda i, j: (i, j)),
                   pl.BlockSpec((cpb, 1, D), lambda i, j: (i, 0, 0))),
        scratch_shapes=[pltpu.VMEM((tm, D), f32), pltpu.VMEM((tm, D), f32), pltpu.VMEM((tm, D), bf16)],
        compiler_params=_params(("parallel", "arbitrary")),
        name="rwkv_proj",
    )(x, x, g, starts, mu_tiles, wcat)


def _softplus(z):
    return jnp.maximum(z, 0.0) + jnp.log(1.0 + jnp.exp(-jnp.abs(z)))


SEG_TILE = 256


def _head_sum(x, split):
    rows, D = x.shape
    r = lax.broadcasted_iota(jnp.int32, (SEG_TILE, SEG_TILE), 0) // RWKV_HS
    c = lax.broadcasted_iota(jnp.int32, (SEG_TILE, SEG_TILE), 1) // RWKV_HS
    ones = (r == c).astype(bf16)
    out = []
    for j in range(D // SEG_TILE):
        blk = x[:, j * SEG_TILE:(j + 1) * SEG_TILE]
        hi = blk.astype(bf16)
        s = jnp.dot(hi, ones, preferred_element_type=f32)
        if split:
            lo = (blk - hi.astype(f32)).astype(bf16)
            s = s + jnp.dot(lo, ones, preferred_element_type=f32)
        out.append(s)
    return jnp.concatenate(out, axis=1)


def _chunk_cumsum(x):
    rows = x.shape[0]
    row = lax.broadcasted_iota(jnp.int32, (rows, rows), 0)
    col = lax.broadcasted_iota(jnp.int32, (rows, rows), 1)
    tri = jnp.logical_and((row // CHUNK) == (col // CHUNK), row >= col).astype(bf16)
    hi = x.astype(bf16)
    rest = x - hi.astype(f32)
    mid = rest.astype(bf16)
    lo = (rest - mid.astype(f32)).astype(bf16)
    dot = functools.partial(jnp.dot, preferred_element_type=f32)
    return dot(tri, hi) + dot(tri, mid) + dot(tri, lo)


def _rwkv_prep_kernel(p_ref, r_ref, k_ref, v_ref, w2_ref, a2_ref, g2_ref, w0_ref, a0_ref,
                      kk_ref, ka_ref, rk_ref, lnw_ref, lnb_ref,
                      at_ref, rt_ref, bt_ref, kt_ref, bh_ref, kh_ref, gl_ref, ga_ref, gb_ref):
    P = LORA_PAD
    tm = r_ref.shape[0]
    pw = jnp.tanh(p_ref[:, :P].astype(f32)).astype(bf16)
    pa = p_ref[:, P:2 * P]
    pg = jax.nn.sigmoid(p_ref[:, 2 * P:].astype(f32)).astype(bf16)
    wl = w0_ref[...] + jnp.dot(pw, w2_ref[...], preferred_element_type=f32)
    lw = -jnp.exp(-_softplus(-wl) - 0.5)
    a = jax.nn.sigmoid(a0_ref[...] + jnp.dot(pa, a2_ref[...], preferred_element_type=f32))
    g = jnp.dot(pg, g2_ref[...], preferred_element_type=f32)

    cum = _chunk_cumsum(lw)
    tot = jnp.concatenate(
        [jnp.broadcast_to(cum[(ci + 1) * CHUNK - 1:(ci + 1) * CHUNK, :], (CHUNK, cum.shape[1]))
         for ci in range(tm // CHUNK)], axis=0)

    r, k, v = r_ref[...].astype(f32), k_ref[...].astype(f32), v_ref[...].astype(f32)
    kk = k * kk_ref[...]
    kk = kk / jnp.maximum(jnp.sqrt(_head_sum(kk * kk, True)), 1e-12)
    kmod = k * (1.0 + (a - 1.0) * ka_ref[...])
    beta = kk * a
    e_neg = jnp.exp(-cum)
    e_end = jnp.exp(tot - cum)
    at_ref[...] = (-kk * jnp.exp(cum - lw)).astype(bf16)
    rt_ref[...] = (r * jnp.exp(cum)).astype(bf16)
    bt_ref[...] = (beta * e_neg).astype(bf16)
    kt_ref[...] = (kmod * e_neg).astype(bf16)
    bh_ref[...] = (beta * e_end).astype(bf16)
    kh_ref[...] = (kmod * e_end).astype(bf16)
    bonus = _head_sum(r * kmod * rk_ref[...], True) * v
    ga_ref[...] = (lnw_ref[...] * g).astype(bf16)
    gb_ref[...] = ((lnb_ref[...] + bonus) * g).astype(bf16)
    for ci in range(tm // CHUNK):
        gl_ref[ci] = jnp.exp(tot[ci * CHUNK:ci * CHUNK + 1, :])


def _rwkv_prep(proj, lora_block, w2p, a2p, g2p, w0, a0, k_k, k_a, r_k, ln_w, ln_b):
    T = proj.shape[0]
    D = w2p.shape[1]
    P = LORA_PAD
    tm = _pick(T, (128, 64))
    wspec = _resident((P, D))
    vspec = pl.BlockSpec((1, D), lambda i: (0, 0))
    ospec = pl.BlockSpec((tm, D), lambda i: (i, 0))
    o16 = jax.ShapeDtypeStruct((T, D), bf16)
    return pl.pallas_call(
        _rwkv_prep_kernel,
        out_shape=(o16,) * 6 + (jax.ShapeDtypeStruct((T // CHUNK, 1, D), f32), o16, o16),
        grid=(T // tm,),
        in_specs=[pl.BlockSpec((tm, 3 * P), lambda i: (i, lora_block)),
                  pl.BlockSpec((tm, D), lambda i: (i, 0)),
                  pl.BlockSpec((tm, D), lambda i: (i, 1)),
                  pl.BlockSpec((tm, D), lambda i: (i, 2)),
                  wspec, wspec, wspec] + [vspec] * 7,
        out_specs=(ospec,) * 6 + (pl.BlockSpec((tm // CHUNK, 1, D), lambda i: (i, 0, 0)), ospec, ospec),
        compiler_params=_params(("parallel",)),
        name="rwkv_prep",
    )(proj, proj, proj, proj, w2p, a2p, g2p, w0, a0, k_k, k_a, r_k, ln_w, ln_b)


def _rwkv_heads_chunk(heads):
    L, N = heads[0][6].shape
    nh = len(heads)
    dot = functools.partial(jnp.dot, preferred_element_type=f32)
    row = lax.broadcasted_iota(jnp.int32, (L, L), 0)
    col = lax.broadcasted_iota(jnp.int32, (L, L), 1)
    incl = row >= col
    strict = row > col
    lhs = [jnp.concatenate([h[0], h[1]], axis=0) for h in heads]
    rhs = [jnp.concatenate([h[2], h[3]], axis=0) for h in heads]
    G = [lax.dot_general(lhs[i], rhs[i], NT_DIMS, preferred_element_type=f32) for i in range(nh)]
    LS = [lax.dot_general(lhs[i], heads[i][7].astype(bf16), NT_DIMS, preferred_element_type=f32)
          for i in range(nh)]
    n = [jnp.where(strict, g[:L, :L], 0.0) for g in G]
    a_ak = [jnp.where(strict, g[:L, L:], 0.0).astype(bf16) for g in G]
    a_r = [jnp.concatenate([jnp.where(incl, g[L:, :L], 0.0), jnp.where(incl, g[L:, L:], 0.0)],
                           axis=1).astype(bf16) for g in G]
    x = [LS[i][:L] + dot(a_ak[i], heads[i][6]) for i in range(nh)]
    steps = max(1, (L - 1).bit_length())
    for s in range(steps):
        if s < steps - 1:
            z = [dot(n[i].astype(bf16), jnp.concatenate([x[i], n[i]], axis=1).astype(bf16))
                 for i in range(nh)]
            x = [x[i] + z[i][:, :N] for i in range(nh)]
            n = [z[i][:, N:] for i in range(nh)]
        else:
            x = [x[i] + dot(n[i].astype(bf16), x[i].astype(bf16)) for i in range(nh)]
    uv = [jnp.concatenate([x[i], heads[i][6].astype(f32)], axis=0) for i in range(nh)]
    o = [LS[i][L:] + dot(a_r[i], uv[i].astype(bf16)) for i in range(nh)]
    S_new = [heads[i][7] * heads[i][8]
             + dot(uv[i].T.astype(bf16), jnp.concatenate([heads[i][4], heads[i][5]], axis=0))
             for i in range(nh)]
    return list(zip(o, S_new))


def _rwkv_core_kernel(at_ref, rt_ref, bt_ref, kt_ref, bh_ref, kh_ref, v_ref, gl_ref, s0_ref,
                      ga_ref, gb_ref, y_ref, sout_ref, S_ref, *, npc, nhead):
    c = pl.program_id(1)
    N = RWKV_HS

    @pl.when(c == 0)
    def _():
        S_ref[...] = jnp.zeros_like(S_ref)

    @pl.when(c >= npc)
    def _():
        S_ref[...] = s0_ref[0]

    gl = gl_ref[0]
    heads = []
    for hh in range(nhead):
        sl = slice(hh * N, (hh + 1) * N)
        heads.append((at_ref[:, sl], rt_ref[:, sl], bt_ref[:, sl], kt_ref[:, sl], bh_ref[:, sl],
                      kh_ref[:, sl], v_ref[:, sl], S_ref[hh], gl[:, sl]))
    res = _rwkv_heads_chunk(heads)
    for hh in range(nhead):
        S_ref[hh] = res[hh][1]
    outs = [o for o, _ in res]
    cen = [o - jnp.mean(o, axis=-1, keepdims=True) for o in outs]
    nrm = [d * lax.rsqrt(jnp.mean(d * d, axis=-1, keepdims=True) + RWKV_GN_EPS) for d in cen]
    y_ref[...] = (jnp.concatenate(nrm, axis=1) * ga_ref[...].astype(f32)
                  + gb_ref[...].astype(f32)).astype(y_ref.dtype)

    @pl.when(jnp.logical_or(c == npc - 1, c >= npc))
    def _():
        sout_ref[0] = S_ref[...]


RWKV_HEADS_PER_STEP = 32


def _rwkv_core(ops, proj, v_block, gl, state, ga, gb, npc, nchunks):
    T, D = ga.shape
    N = RWKV_HS
    nh = D // N
    nb = state.shape[0]
    G = RWKV_HEADS_PER_STEP
    W = G * N
    tok = pl.BlockSpec((CHUNK, W), lambda p, c: (c, p))
    vtok = pl.BlockSpec((CHUNK, W), lambda p, c: (c, v_block * (D // W) + p))

    def sin_map(p, c):
        return (jnp.maximum(c - npc, 0), p, 0, 0)

    def sout_map(p, c):
        return (jnp.maximum(c - npc + 1, 0), p, 0, 0)

    return pl.pallas_call(
        functools.partial(_rwkv_core_kernel, npc=npc, nhead=G),
        out_shape=(jax.ShapeDtypeStruct((T, D), bf16),
                   jax.ShapeDtypeStruct((nb + 1, nh, N, N), f32)),
        grid=(nh // G, nchunks),
        in_specs=[tok] * 6 + [vtok, pl.BlockSpec((1, 1, W), lambda p, c: (c, 0, p)),
                              pl.BlockSpec((1, G, N, N), sin_map), tok, tok],
        out_specs=(tok, pl.BlockSpec((1, G, N, N), sout_map)),
        scratch_shapes=[pltpu.VMEM((G, N, N), f32)],
        compiler_params=_params(("parallel", "arbitrary")),
        name="rwkv_core",
    )(*ops, proj, gl, state, ga, gb)


def _positions(n_prompt, n_b, n_s):
    return jnp.concatenate([jnp.arange(n_prompt), jnp.tile(PAST_LEN + jnp.arange(n_s), n_b)]).astype(f32)


def _attn_tables(pos):
    half = ROT_DIM // 2
    inv = ROPE_THETA ** (-jnp.arange(half, dtype=f32) / half)
    ang = pos[:, None] * inv[None, :]
    cos, sin = jnp.cos(ang), jnp.sin(ang)
    T = pos.shape[0]
    zeros = jnp.zeros((T, half), f32)
    rest = HEAD_DIM - ROT_DIM
    c = jnp.concatenate([cos, cos, jnp.ones((T, rest), f32)], axis=1)
    s1 = jnp.concatenate([zeros, sin, jnp.zeros((T, rest), f32)], axis=1)
    s2 = jnp.concatenate([-sin, zeros, jnp.zeros((T, rest), f32)], axis=1)
    rep = 256 // HEAD_DIM
    return tuple(jnp.tile(t, (1, rep)) for t in (c, s1, s2))


def _ret_tables(pos):
    half = RET_DK // 2
    inv = RET_THETA ** (-jnp.arange(half, dtype=f32) / half)
    ang = pos[:, None] * inv[None, :]
    return jnp.cos(ang), jnp.sin(ang)


def _pad_cols(w, n):
    return jnp.pad(w, ((0, 0), (0, n - w.shape[1])))


def _pad_rows(w, n):
    return jnp.pad(w, ((0, n - w.shape[0]), (0, 0)))


def kernel(x_prompt, x_sample, cache_attn_k, cache_attn_v, state_ret, state_rwkv, state_rwkv_shift,
           norm_mix, norm_mlp, norm_final,
           attn_w_qkv, attn_sinks, attn_w_o,
           ret_w_in, ret_gn_w, ret_w_o,
           rwkv_mu, rwkv_w_rkv, rwkv_w_o, rwkv_w0, rwkv_w1, rwkv_w2, rwkv_a0, rwkv_a1, rwkv_a2,
           rwkv_g1, rwkv_g2, rwkv_k_k, rwkv_k_a, rwkv_r_k, rwkv_ln_w, rwkv_ln_b,
           mlp_w_up, mlp_w_down):
    bp, tp, D = x_prompt.shape
    nb, ns, _ = x_sample.shape
    assert bp == 1 and ns == CHUNK and tp % CHUNK == 0
    depth = norm_mix.shape[0]
    npc = tp // CHUNK
    nchunks = npc + nb
    T = tp + nb * ns
    qd = ATTN_HEADS * HEAD_DIM
    kvd = ATTN_KV_HEADS * HEAD_DIM

    x = jnp.concatenate([x_prompt.reshape(tp, D), x_sample.reshape(nb * ns, D)], axis=0)
    pos = _positions(tp, nb, ns)
    attn_tabs = _attn_tables(pos)
    ret_tabs = _ret_tables(pos)
    lg = jnp.log1p(-jnp.exp2(-5.0 - jnp.arange(RET_HEADS, dtype=f32)))
    tm_proj = _pick(T, (512, 256, 128, 64))
    tm_ret = _pick(T, (1024, 512, 256, 128, 64))
    w_up16 = mlp_w_up.astype(bf16)
    w_down16 = mlp_w_down.astype(bf16)

    kp_l, vp_l, ks_l, vs_l, rp_l, rs_l, wp_l, ws_l, shp_l, shs_l = ([] for _ in range(10))
    for i in range(depth):
        j, kind = divmod(i, 3)
        g_mix = norm_mix[i][None, :]
        if kind == 0:
            nsub = (qd + 2 * kvd) // (2 * EPI_COLS)
            routes = [[("q", 0, s * EPI_COLS) for s in range(nsub)],
                      [("q", 0, s * EPI_COLS) for s in range(nsub, qd // EPI_COLS)]
                      + [("k", 1, 0), ("plain", 1, EPI_COLS)]]
            q, kv = _norm_matmul(
                x, g_mix, attn_w_qkv[j].astype(bf16), attn_tabs, routes, _epi_attn,
                [jax.ShapeDtypeStruct((T, qd), bf16), jax.ShapeDtypeStruct((T, 2 * kvd), f32)],
                [pl.BlockSpec((tm_proj, qd), lambda i, jj: (i, 0)),
                 pl.BlockSpec((tm_proj, 2 * kvd), lambda i, jj: (i, 0))],
                tm_proj, "attn_qkv")
            k_new = kv[:, :kvd]
            v_new = kv[:, kvd:]

            def ext(new, cache):
                samp = jnp.concatenate([cache.reshape(nb, WINDOW, kvd), new[tp:].reshape(nb, ns, kvd)], axis=1)
                rows = jnp.concatenate([jnp.zeros((WINDOW, kvd), f32), new[:tp],
                                        samp.reshape(nb * (WINDOW + ns), kvd)], axis=0)
                r4 = rows.astype(bf16).reshape(rows.shape[0], ATTN_KV_HEADS, 1, HEAD_DIM)
                dup = jnp.broadcast_to(r4, (rows.shape[0], ATTN_KV_HEADS, 2, HEAD_DIM))
                return dup.reshape(rows.shape[0], 2 * kvd), samp

            kext, k_samp = ext(k_new, cache_attn_k[j])
            vext, v_samp = ext(v_new, cache_attn_v[j])
            o = _attention(q, kext, vext, attn_sinks[j], npc, nchunks)
            x = _matmul_residual(o, attn_w_o[j].astype(bf16), x, "attn_out")
            kp_l.append(k_new[tp - WINDOW:tp].reshape(1, WINDOW, ATTN_KV_HEADS, HEAD_DIM))
            vp_l.append(v_new[tp - WINDOW:tp].reshape(1, WINDOW, ATTN_KV_HEADS, HEAD_DIM))
            ks_l.append(k_samp[:, -WINDOW:].reshape(nb, WINDOW, ATTN_KV_HEADS, HEAD_DIM))
            vs_l.append(v_samp[:, -WINDOW:].reshape(nb, WINDOW, ATTN_KV_HEADS, HEAD_DIM))
        elif kind == 1:
            nq = RET_HEADS * RET_DK // 1024
            nv = RET_HEADS * RET_DV // 1024
            tile = lambda mode: [(mode, 0, s * EPI_COLS) for s in range(1024 // EPI_COLS)]
            routes = [tile("q")] * nq + [tile("k")] * nq + [tile("plain")] * (2 * nv)
            (proj,) = _norm_matmul(
                x, g_mix, ret_w_in[j].astype(bf16), ret_tabs, routes, _epi_ret,
                [jax.ShapeDtypeStruct((T, ret_w_in.shape[2]), bf16)],
                [pl.BlockSpec((tm_ret, 1024), lambda i, jj: (i, jj))], tm_ret, "ret_proj")
            gnw = ret_gn_w[j][None, :]
            lp = _pick(tp, (256, 128, 64))
            y_p, s_p = _retention(proj, state_ret[j], gnw, lg, 0, tp, lp, False, "ret_prompt")
            y_s, s_s = _retention(proj, state_ret[j], gnw, lg, tp, nb * ns, ns, True, "ret_sample")
            x = _matmul_residual2(y_p, y_s, ret_w_o[j].astype(bf16), x, "ret_out")
            rp_l.append(s_p)
            rs_l.append(s_s)
        else:
            starts = jnp.concatenate([jnp.zeros((npc, D), f32), state_rwkv_shift[j]], axis=0)[:, None, :]
            P = LORA_PAD
            tn = 512
            wcat = jnp.concatenate(
                [rwkv_w_rkv[j][0], rwkv_w_rkv[j][1], rwkv_w_rkv[j][2],
                 _pad_cols(rwkv_w1[j], P), _pad_cols(rwkv_a1[j], P), _pad_cols(rwkv_g1[j], P)],
                axis=1).astype(bf16)
            tiles_per_d = D // tn
            mu = rwkv_mu[j]
            mu_tiles = jnp.concatenate(
                [jnp.repeat(mu[jnp.array([0, 2, 3])], tiles_per_d, axis=0), mu[jnp.array([1, 4, 5])]],
                axis=0)[:, None, :]
            proj, h_last = _lerp_matmul(x, g_mix, starts, mu_tiles, wcat, tn, tiles_per_d,
                                        3 * tiles_per_d, npc)
            prep = _rwkv_prep(
                proj, 3 * D // (3 * P),
                _pad_rows(rwkv_w2[j], P).astype(bf16), _pad_rows(rwkv_a2[j], P).astype(bf16),
                _pad_rows(rwkv_g2[j], P).astype(bf16), rwkv_w0[j][None, :], rwkv_a0[j][None, :],
                rwkv_k_k[j][None, :], rwkv_k_a[j][None, :], rwkv_r_k[j].reshape(1, D),
                rwkv_ln_w[j][None, :], rwkv_ln_b[j][None, :])
            ops, gl, ga, gb = prep[:6], prep[6], prep[7], prep[8]
            y, s_all = _rwkv_core(ops, proj, 2, gl, state_rwkv[j], ga, gb, npc, nchunks)
            x = _matmul_residual(y, rwkv_w_o[j].astype(bf16), x, "rwkv_out")
            wp_l.append(s_all[:1])
            ws_l.append(s_all[1:])
            shp_l.append(h_last[npc - 1])
            shs_l.append(h_last[npc:, 0])
        x = _mlp(x, norm_mlp[i][None, :], w_up16, w_down16, i, norm_final[None, :],
                 tp if i == depth - 1 else None, "mlp")

    y_prompt = x[0].reshape(1, tp, D)
    y_sample = x[1].reshape(nb, ns, D)
    return (y_prompt, y_sample,
            jnp.stack(kp_l), jnp.stack(vp_l), jnp.stack(ks_l), jnp.stack(vs_l),
            jnp.stack(rp_l), jnp.stack(rs_l),
            jnp.stack(wp_l), jnp.stack(ws_l), jnp.stack(shp_l), jnp.stack(shs_l))
```

```python
import functools

import jax
import jax.numpy as jnp
from jax import lax
from jax.experimental import pallas as pl
from jax.experimental.pallas import tpu as pltpu

f32 = jnp.float32
bf16 = jnp.bfloat16

CHUNK = 64
NORM_EPS = 1e-5
PAST_LEN = 4096

ATTN_HEADS = 32
ATTN_KV_HEADS = 4
ATTN_GROUP = ATTN_HEADS // ATTN_KV_HEADS
HEAD_DIM = 64
WINDOW = 128
ROT_DIM = HEAD_DIM // 4
ROPE_THETA = 500000.0

RET_HEADS = 8
RET_DK = 256
RET_DV = 512
RET_THETA = 10000.0
RET_GN_EPS = 1e-5

RWKV_HS = 64
RWKV_GN_EPS = 64e-5
LORA_PAD = 512

VMEM_LIMIT = 52 * 1024 * 1024

NT_DIMS = (((1,), (1,)), ((), ()))


def _pick(n, cands):
    for c in cands:
        if n % c == 0:
            return c
    raise ValueError(f"no tile for {n} in {cands}")


def _params(sem):
    return pltpu.CompilerParams(dimension_semantics=sem, vmem_limit_bytes=VMEM_LIMIT)


def _rms(x, g):
    return x * lax.rsqrt(jnp.mean(x * x, axis=-1, keepdims=True) + NORM_EPS) * g


def _epi_attn(blk, mode, tabs):
    if mode == "plain":
        return blk
    c_ref, s1_ref, s2_ref = tabs
    out = (blk * c_ref[...] + pltpu.roll(blk, 8, 1) * s1_ref[...]
           + pltpu.roll(blk, 256 - 8, 1) * s2_ref[...])
    return out * (HEAD_DIM ** -0.5) if mode == "q" else out


def _epi_ret(blk, mode, tabs):
    if mode == "plain":
        return blk
    cos_ref, sin_ref = tabs
    c, s = cos_ref[...], sin_ref[...]
    x1, x2 = blk[:, :128], blk[:, 128:]
    out = jnp.concatenate([x1 * c - x2 * s, x2 * c + x1 * s], axis=1)
    return out * (RET_DK ** -0.5) if mode == "k" else out


EPI_COLS = 256


def _row_split_specs(tm, cols, split_at):
    first = pl.BlockSpec((tm, cols), lambda i, *_: (jnp.minimum(i, split_at - 1), 0))
    second = pl.BlockSpec((tm, cols), lambda i, *_: (jnp.maximum(i - split_at, 0), 0))
    return [first, second]


def _normmm_kernel(*refs, routes, ntab, nout, epi, split_at):
    nx = 1 if split_at is None else 2
    x_refs = refs[:nx]
    g_ref, w_ref = refs[nx:nx + 2]
    tabs = refs[nx + 2:nx + 2 + ntab]
    outs = refs[nx + 2 + ntab:nx + 2 + ntab + nout]
    xn_ref = refs[nx + 2 + ntab + nout]
    i = pl.program_id(0)
    j = pl.program_id(1)

    if split_at is None:
        @pl.when(j == 0)
        def _():
            xn_ref[...] = _rms(x_refs[0][...], g_ref[...]).astype(bf16)
    else:
        @pl.when(jnp.logical_and(j == 0, i < split_at))
        def _():
            xn_ref[...] = _rms(x_refs[0][...], g_ref[...]).astype(bf16)

        @pl.when(jnp.logical_and(j == 0, i >= split_at))
        def _():
            xn_ref[...] = _rms(x_refs[1][...], g_ref[...]).astype(bf16)

    groups = {}
    for jj, tile_routes in enumerate(routes):
        groups.setdefault(tuple(tile_routes), []).append(jj)
    for tile_routes, jjs in groups.items():
        cond = functools.reduce(jnp.logical_or, [j == jj for jj in jjs])

        @pl.when(cond)
        def _(tile_routes=tile_routes):
            for s, (mode, oi, col) in enumerate(tile_routes):
                acc = jnp.dot(xn_ref[...], w_ref[:, s * EPI_COLS:(s + 1) * EPI_COLS],
                              preferred_element_type=f32)
                outs[oi][:, col:col + EPI_COLS] = epi(acc, mode, tabs).astype(outs[oi].dtype)


def _norm_matmul(x, g, w, tabs, routes, epi, out_shapes, out_specs, tm, name):
    xs = x if isinstance(x, tuple) else (x,)
    T = sum(p.shape[0] for p in xs)
    D = xs[0].shape[1]
    N = w.shape[1]
    ntile = len(routes)
    tn = N // ntile
    assert tn * ntile == N and tn == EPI_COLS * len(routes[0])
    if len(xs) == 1:
        split_at = None
        x_specs = [pl.BlockSpec((tm, D), lambda i, j: (i, 0))]
    else:
        assert xs[0].shape[0] % tm == 0 and xs[1].shape[0] % tm == 0
        split_at = xs[0].shape[0] // tm
        x_specs = _row_split_specs(tm, D, split_at)
    tab_specs = [pl.BlockSpec((tm, t.shape[1]), lambda i, j: (i, 0)) for t in tabs]
    return pl.pallas_call(
        functools.partial(_normmm_kernel, routes=routes, ntab=len(tabs), nout=len(out_shapes), epi=epi,
                          split_at=split_at),
        out_shape=tuple(out_shapes),
        grid=(T // tm, ntile),
        in_specs=x_specs + [pl.BlockSpec((1, D), lambda i, j: (0, 0)),
                            pl.BlockSpec((D, tn), lambda i, j: (0, j))] + tab_specs,
        out_specs=tuple(out_specs),
        scratch_shapes=[pltpu.VMEM((tm, D), bf16)],
        compiler_params=_params(("parallel", "arbitrary")),
        name=name,
    )(*xs, g, w, *tabs)


def _mmres_kernel(a_ref, w_ref, *rest, split_at):
    o_ref = rest[-1]
    prod = jnp.dot(a_ref[...], w_ref[...], preferred_element_type=f32)
    if split_at is None:
        o_ref[...] = rest[0][...] + prod
    else:
        i = pl.program_id(0)

        @pl.when(i < split_at)
        def _():
            o_ref[...] = rest[0][...] + prod

        @pl.when(i >= split_at)
        def _():
            o_ref[...] = rest[1][...] + prod


def _resident(shape):
    return pl.BlockSpec(shape, lambda *_: (0,) * len(shape), pipeline_mode=pl.Buffered(1))


def _matmul_residual(a, w, res, name):
    T, K = a.shape
    N = w.shape[1]
    rs = res if isinstance(res, tuple) else (res,)
    if len(rs) == 1:
        tm = _pick(T, (512, 256, 128, 64))
        split_at = None
        r_specs = [pl.BlockSpec((tm, N), lambda i: (i, 0))]
    else:
        tm = _pick(rs[1].shape[0], (512, 256, 128, 64))
        assert rs[0].shape[0] % tm == 0
        split_at = rs[0].shape[0] // tm
        r_specs = _row_split_specs(tm, N, split_at)
    return pl.pallas_call(
        functools.partial(_mmres_kernel, split_at=split_at),
        out_shape=jax.ShapeDtypeStruct((T, N), f32),
        grid=(T // tm,),
        in_specs=[pl.BlockSpec((tm, K), lambda i: (i, 0)), _resident((K, N))] + r_specs,
        out_specs=pl.BlockSpec((tm, N), lambda i: (i, 0)),
        compiler_params=_params(("parallel",)),
        name=name,
    )(a, w, *rs)


def _mmres2_kernel(ap_ref, as_ref, w_ref, r_ref, o_ref, *, np_blocks):
    i = pl.program_id(0)

    @pl.when(i < np_blocks)
    def _():
        o_ref[...] = r_ref[...] + jnp.dot(ap_ref[...], w_ref[...], preferred_element_type=f32)

    @pl.when(i >= np_blocks)
    def _():
        o_ref[...] = r_ref[...] + jnp.dot(as_ref[...], w_ref[...], preferred_element_type=f32)


def _matmul_residual2(a_p, a_s, w, res, name):
    tp, K = a_p.shape
    ts = a_s.shape[0]
    N = w.shape[1]
    tm = _pick(ts, (256, 128, 64))
    assert tp % tm == 0
    np_blocks = tp // tm
    return pl.pallas_call(
        functools.partial(_mmres2_kernel, np_blocks=np_blocks),
        out_shape=jax.ShapeDtypeStruct((tp + ts, N), f32),
        grid=((tp + ts) // tm,),
        in_specs=[pl.BlockSpec((tm, K), lambda i: (jnp.minimum(i, np_blocks - 1), 0)),
                  pl.BlockSpec((tm, K), lambda i: (jnp.maximum(i - np_blocks, 0), 0)),
                  _resident((K, N)),
                  pl.BlockSpec((tm, N), lambda i: (i, 0))],
        out_specs=pl.BlockSpec((tm, N), lambda i: (i, 0)),
        compiler_params=_params(("parallel",)),
        name=name,
    )(a_p, a_s, w, res)


def _mlp_kernel(x_ref, g_ref, wu_ref, wd_ref, gf_ref, *rest, nk, split_at):
    if split_at is None:
        acc_ref, xn_ref = rest
    else:
        op_ref, os_ref, xn_ref, acc_ref = rest
    i = pl.program_id(0)
    k = pl.program_id(1)

    @pl.when(k == 0)
    def _():
        x = x_ref[...]
        xn_ref[...] = _rms(x, g_ref[...]).astype(bf16)
        acc_ref[...] = x

    h = jnp.dot(xn_ref[...], wu_ref[...], preferred_element_type=f32)
    h = jnp.square(jnp.maximum(h, 0.0)).astype(bf16)
    acc_ref[...] += jnp.dot(h, wd_ref[...], preferred_element_type=f32)

    if split_at is not None:
        @pl.when(jnp.logical_and(k == nk - 1, i < split_at))
        def _():
            op_ref[...] = _rms(acc_ref[...], gf_ref[...])

        @pl.when(jnp.logical_and(k == nk - 1, i >= split_at))
        def _():
            os_ref[...] = _rms(acc_ref[...], gf_ref[...])


def _mlp(x, g, w_up, w_down, layer, g_final, split_rows, name):
    T, D = x.shape
    F = w_up.shape[2]
    tf = _pick(F, (1024, 512))
    nk = F // tf
    if split_rows is None:
        tm = _pick(T, (512, 256, 128, 64))
        x_spec = pl.BlockSpec((tm, D), lambda i, k: (i, 0))
        split_at = None
        out_shape = jax.ShapeDtypeStruct((T, D), f32)
        out_specs = pl.BlockSpec((tm, D), lambda i, k: (i, 0))
        scratch = [pltpu.VMEM((tm, D), bf16)]
    else:
        tm = _pick(T - split_rows, (512, 256, 128, 64))
        x_spec = pl.BlockSpec((tm, D), lambda i, k: (i, 0))
        assert split_rows % tm == 0
        split_at = split_rows // tm
        out_shape = (jax.ShapeDtypeStruct((split_rows, D), f32), jax.ShapeDtypeStruct((T - split_rows, D), f32))
        out_specs = (pl.BlockSpec((tm, D), lambda i, k: (jnp.minimum(i, split_at - 1), 0)),
                     pl.BlockSpec((tm, D), lambda i, k: (jnp.maximum(i - split_at, 0), 0)))
        scratch = [pltpu.VMEM((tm, D), bf16), pltpu.VMEM((tm, D), f32)]
    return pl.pallas_call(
        functools.partial(_mlp_kernel, nk=nk, split_at=split_at),
        out_shape=out_shape,
        grid=(T // tm, nk),
        in_specs=[x_spec,
                  pl.BlockSpec((1, D), lambda i, k: (0, 0)),
                  pl.BlockSpec((None, D, tf), lambda i, k: (layer, 0, k)),
                  pl.BlockSpec((None, tf, D), lambda i, k: (layer, k, 0)),
                  pl.BlockSpec((1, D), lambda i, k: (0, 0))],
        out_specs=out_specs,
        scratch_shapes=scratch,
        compiler_params=_params(("arbitrary", "arbitrary")),
        name=name,
    )(x, g, w_up, w_down, g_final)


def _attn_kernel(sb_ref, q_ref, k0_ref, k1_ref, k2_ref, v0_ref, v1_ref, v2_ref, o_ref, *, npc):
    c = pl.program_id(0)
    nkeys = 3 * CHUNK
    ncols = nkeys + CHUNK
    PW = 2 * HEAD_DIM
    pairs = ATTN_GROUP // 2
    col_blk = lax.broadcasted_iota(jnp.int32, (1, ncols), 1) // CHUNK
    valid = jnp.logical_or(jnp.logical_or(c >= npc, c + col_blk >= 2), col_blk >= 3)
    bias = jnp.where(valid, 0.0, -jnp.inf).astype(f32)
    zpad = jnp.zeros((CHUNK, ATTN_KV_HEADS * PW), bf16)
    kcat = jnp.concatenate([k0_ref[...], k1_ref[...], k2_ref[...], zpad], axis=0)
    vcat = jnp.concatenate([v0_ref[...], v1_ref[...], v2_ref[...], zpad], axis=0)
    lo_half = lax.broadcasted_iota(jnp.int32, (1, PW), 1) < HEAD_DIM
    ones = jnp.ones((ncols, PW), bf16)
    kvs = range(ATTN_KV_HEADS)

    def stacked_q(kv):
        parts = []
        for p in range(pairs):
            col = (kv * pairs + p) * PW
            qp = q_ref[:, col:col + PW]
            parts += [jnp.where(lo_half, qp, 0.0), jnp.where(lo_half, 0.0, qp)]
        return jnp.concatenate(parts, axis=0).astype(bf16)

    s = [lax.dot_general(stacked_q(kv), kcat[:, kv * PW:(kv + 1) * PW], NT_DIMS,
                         preferred_element_type=f32) + sb_ref[kv] + bias for kv in kvs]
    p = [jnp.exp(s[kv] - jnp.max(s[kv], axis=-1, keepdims=True)).astype(bf16) for kv in kvs]
    oa = [jnp.dot(p[kv], jnp.concatenate([vcat[:, kv * PW:(kv + 1) * PW], ones], axis=1),
                  preferred_element_type=f32) for kv in kvs]
    for kv in kvs:
        on = oa[kv][:, :PW] / oa[kv][:, PW:]
        for pi in range(pairs):
            r0 = 2 * pi * CHUNK
            blk = jnp.where(lo_half, on[r0:r0 + CHUNK], on[r0 + CHUNK:r0 + 2 * CHUNK])
            col = (kv * pairs + pi) * PW
            o_ref[:, col:col + PW] = blk.astype(o_ref.dtype)


def _attention(qkv, kext, vext, sinks, npc, nchunks):
    T = qkv.shape[0]
    qd = ATTN_HEADS * HEAD_DIM
    kvd = kext.shape[1]

    def kv_spec(j):
        def imap(c):
            return (jnp.where(c < npc, c, npc + 2 + 3 * (c - npc)) + j, 0)
        return pl.BlockSpec((CHUNK, kvd), imap)

    nkeys = 3 * CHUNK
    col = jnp.arange(nkeys + CHUNK)[None, None, :]
    sink_rows = jnp.repeat(sinks.astype(f32).reshape(ATTN_KV_HEADS, ATTN_GROUP), CHUNK, axis=1)[:, :, None]
    sink_bias = jnp.where(col < nkeys, 0.0, jnp.where(col == nkeys, sink_rows, -jnp.inf)).astype(f32)

    return pl.pallas_call(
        functools.partial(_attn_kernel, npc=npc),
        out_shape=jax.ShapeDtypeStruct((T, qd), bf16),
        grid=(nchunks,),
        in_specs=[pl.BlockSpec(sink_bias.shape, lambda c: (0, 0, 0)),
                  pl.BlockSpec((CHUNK, qd), lambda c: (c, 0))]
                 + [kv_spec(j) for j in range(3)] + [kv_spec(j) for j in range(3)],
        out_specs=pl.BlockSpec((CHUNK, qd), lambda c: (c, 0)),
        compiler_params=_params(("parallel",)),
        name="attn_core",
    )(sink_bias, qkv, kext, kext, kext, vext, vext, vext)


RET_HEADS_PER_STEP = 4


def _ret_kernel(lg_ref, q_ref, k_ref, v_ref, g_ref, s0_ref, gnw_ref, y_ref, sout_ref, S_ref, *, from_state):
    hg = pl.program_id(0)
    c = pl.program_id(1)
    L = q_ref.shape[0]
    nh = RET_HEADS_PER_STEP
    hs = range(nh)
    dot = functools.partial(jnp.dot, preferred_element_type=f32)

    if from_state:
        S_ref[...] = s0_ref[0]
    else:
        @pl.when(c == 0)
        def _():
            S_ref[...] = jnp.zeros_like(S_ref)

    lg = [lg_ref[hg * nh + h] for h in hs]
    q = [q_ref[:, h * RET_DK:(h + 1) * RET_DK] for h in hs]
    k = [k_ref[:, h * RET_DK:(h + 1) * RET_DK] for h in hs]
    v = [v_ref[:, h * RET_DV:(h + 1) * RET_DV] for h in hs]
    row = lax.broadcasted_iota(jnp.int32, (L, L), 0)
    col = lax.broadcasted_iota(jnp.int32, (L, L), 1)
    diff = (row - col).astype(f32)
    idx = lax.broadcasted_iota(jnp.int32, (L, 1), 0).astype(f32)
    decay = [jnp.where(diff >= 0, jnp.exp(lg[h] * jnp.maximum(diff, 0.0)), 0.0) for h in hs]
    xi = [jnp.exp(lg[h] * (idx + 1.0)) for h in hs]
    zeta = [jnp.exp(lg[h] * (L - 1.0 - idx)) for h in hs]

    S = [S_ref[h] for h in hs]
    scores = [lax.dot_general(q[h], k[h], NT_DIMS, preferred_element_type=f32) * decay[h] for h in hs]
    o = [dot(scores[h].astype(bf16), v[h])
         + dot((q[h].astype(f32) * xi[h]).astype(bf16), S[h].astype(bf16)) for h in hs]
    S_new = [jnp.exp(lg[h] * L) * S[h] + dot((k[h].astype(f32) * zeta[h]).T.astype(bf16), v[h])
             for h in hs]
    for h in hs:
        S_ref[h] = S_new[h]
        sout_ref[0, h] = S_new[h]

    cen = [o[h] - jnp.mean(o[h], axis=-1, keepdims=True) for h in hs]
    on = [cen[h] * lax.rsqrt(jnp.mean(cen[h] * cen[h], axis=-1, keepdims=True) + RET_GN_EPS) for h in hs]
    for h in hs:
        sl = slice(h * RET_DV, (h + 1) * RET_DV)
        y_ref[:, sl] = (jax.nn.silu(g_ref[:, sl].astype(f32)) * on[h] * gnw_ref[:, sl]).astype(y_ref.dtype)


def _retention(proj, state, gn_w, lg, row0, nrows, L, from_state, name):
    nsteps = nrows // L
    b0 = row0 // L
    assert nsteps * L == nrows and b0 * L == row0
    nseq = nsteps if from_state else 1
    G = RET_HEADS_PER_STEP
    kw, vw = G * RET_DK, G * RET_DV
    kb = RET_HEADS * RET_DK // kw
    vb = 2 * RET_HEADS * RET_DK // vw
    gb = vb + RET_HEADS * RET_DV // vw
    state_map = (lambda h, c: (c, h, 0, 0)) if from_state else (lambda h, c: (0, h, 0, 0))

    return pl.pallas_call(
        functools.partial(_ret_kernel, from_state=from_state),
        out_shape=(jax.ShapeDtypeStruct((nrows, RET_HEADS * RET_DV), bf16),
                   jax.ShapeDtypeStruct((nseq, RET_HEADS, RET_DK, RET_DV), f32)),
        grid=(RET_HEADS // G, nsteps),
        in_specs=[pl.BlockSpec(memory_space=pltpu.SMEM),
                  pl.BlockSpec((L, kw), lambda h, c: (b0 + c, h)),
                  pl.BlockSpec((L, kw), lambda h, c: (b0 + c, kb + h)),
                  pl.BlockSpec((L, vw), lambda h, c: (b0 + c, vb + h)),
                  pl.BlockSpec((L, vw), lambda h, c: (b0 + c, gb + h)),
                  pl.BlockSpec((1, G, RET_DK, RET_DV), state_map),
                  pl.BlockSpec((1, vw), lambda h, c: (0, h))],
        out_specs=(pl.BlockSpec((L, vw), lambda h, c: (c, h)),
                   pl.BlockSpec((1, G, RET_DK, RET_DV), state_map)),
        scratch_shapes=[pltpu.VMEM((G, RET_DK, RET_DV), f32)],
        compiler_params=_params(("parallel", "arbitrary")),
        name=name,
    )(lg, proj, proj, proj, proj, state, gn_w)


SUBLANES = 8


def _lerpmm_kernel(x_ref, xprev_ref, g_ref, start_ref, mu_ref, w_ref, o_ref, hlast_ref,
                   h_ref, xx_ref, l_ref, *, tiles_per_d, n_big, npc):
    i = pl.program_id(0)
    j = pl.program_id(1)
    tm = x_ref.shape[0]
    cpb = tm // CHUNK

    @pl.when(j == 0)
    def _():
        g = g_ref[...]
        h = _rms(x_ref[...], g)
        h_ref[...] = h
        xx_ref[...] = pltpu.roll(h, 1, 0) - h
        prev = _rms(xprev_ref[...], g)[SUBLANES - 1:SUBLANES, :]
        xx_ref[0:1, :] = prev - h[0:1, :]
        for ci in range(cpb):
            gc = i * cpb + ci
            r0 = ci * CHUNK

            @pl.when(jnp.logical_or(gc == 0, gc >= npc))
            def _(ci=ci, r0=r0):
                xx_ref[r0:r0 + 1, :] = start_ref[ci] - h_ref[r0:r0 + 1, :]

            hlast_ref[ci] = h[r0 + CHUNK - 1:r0 + CHUNK, :]

    @pl.when(jnp.logical_or(j % tiles_per_d == 0, j >= n_big))
    def _():
        l_ref[...] = (h_ref[...] + xx_ref[...] * mu_ref[0]).astype(bf16)

    o_ref[...] = jnp.dot(l_ref[...], w_ref[...], preferred_element_type=f32).astype(o_ref.dtype)


def _lerp_matmul(x, g, starts, mu_tiles, wcat, tn, tiles_per_d, n_big, npc):
    T, D = x.shape
    N = wcat.shape[1]
    tm = _pick(T, (1024, 512, 256, 128, 64))
    cpb = tm // CHUNK
    return pl.pallas_call(
        functools.partial(_lerpmm_kernel, tiles_per_d=tiles_per_d, n_big=n_big, npc=npc),
        out_shape=(jax.ShapeDtypeStruct((T, N), bf16),
                   jax.ShapeDtypeStruct((T // CHUNK, 1, D), f32)),
        grid=(T // tm, N // tn),
        in_specs=[pl.BlockSpec((tm, D), lambda i, j: (i, 0)),
                  pl.BlockSpec((SUBLANES, D), lambda i, j: (jnp.maximum(i * (tm // SUBLANES) - 1, 0), 0)),
                  pl.BlockSpec((1, D), lambda i, j: (0, 0)),
                  pl.BlockSpec((cpb, 1, D), lambda i, j: (i, 0, 0)),
                  pl.BlockSpec((1, 1, D), lambda i, j: (j, 0, 0)),
                  pl.BlockSpec((D, tn), lambda i, j: (0, j))],
        out_specs=(pl.BlockSpec((tm, tn), lambda i, j: (i, j)),
                   pl.BlockSpec((cpb, 1, D), lambda i, j: (i, 0, 0))),
        scratch_shapes=[pltpu.VMEM((tm, D), f32), pltpu.VMEM((tm, D), f32), pltpu.VMEM((tm, D), bf16)],
        compiler_params=_params(("parallel", "arbitrary")),
        name="rwkv_proj",
    )(x, x, g, starts, mu_tiles, wcat)


def _softplus(z):
    return jnp.maximum(z, 0.0) + jnp.log(1.0 + jnp.exp(-jnp.abs(z)))


SEG_TILE = 256


def _head_sum(x, split):
    rows, D = x.shape
    r = lax.broadcasted_iota(jnp.int32, (SEG_TILE, SEG_TILE), 0) // RWKV_HS
    c = lax.broadcasted_iota(jnp.int32, (SEG_TILE, SEG_TILE), 1) // RWKV_HS
    ones = (r == c).astype(bf16)
    out = []
    for j in range(D // SEG_TILE):
        blk = x[:, j * SEG_TILE:(j + 1) * SEG_TILE]
        hi = blk.astype(bf16)
        s = jnp.dot(hi, ones, preferred_element_type=f32)
        if split:
            lo = (blk - hi.astype(f32)).astype(bf16)
            s = s + jnp.dot(lo, ones, preferred_element_type=f32)
        out.append(s)
    return jnp.concatenate(out, axis=1)


def _chunk_cumsum(x):
    rows = x.shape[0]
    row = lax.broadcasted_iota(jnp.int32, (rows, rows), 0)
    col = lax.broadcasted_iota(jnp.int32, (rows, rows), 1)
    tri = jnp.logical_and((row // CHUNK) == (col // CHUNK), row >= col).astype(bf16)
    hi = x.astype(bf16)
    rest = x - hi.astype(f32)
    mid = rest.astype(bf16)
    lo = (rest - mid.astype(f32)).astype(bf16)
    dot = functools.partial(jnp.dot, preferred_element_type=f32)
    return dot(tri, hi) + dot(tri, mid) + dot(tri, lo)


def _rwkv_prep_kernel(p_ref, r_ref, k_ref, v_ref, w2_ref, a2_ref, g2_ref, w0_ref, a0_ref,
                      kk_ref, ka_ref, rk_ref, lnw_ref, lnb_ref,
                      at_ref, rt_ref, bt_ref, kt_ref, bh_ref, kh_ref, gl_ref, ga_ref, gb_ref):
    P = LORA_PAD
    tm = r_ref.shape[0]
    pw = jnp.tanh(p_ref[:, :P].astype(f32)).astype(bf16)
    pa = p_ref[:, P:2 * P]
    pg = jax.nn.sigmoid(p_ref[:, 2 * P:].astype(f32)).astype(bf16)
    wl = w0_ref[...] + jnp.dot(pw, w2_ref[...], preferred_element_type=f32)
    lw = -jnp.exp(-_softplus(-wl) - 0.5)
    a = jax.nn.sigmoid(a0_ref[...] + jnp.dot(pa, a2_ref[...], preferred_element_type=f32))
    g = jnp.dot(pg, g2_ref[...], preferred_element_type=f32)

    cum = _chunk_cumsum(lw)
    tot = jnp.concatenate(
        [jnp.broadcast_to(cum[(ci + 1) * CHUNK - 1:(ci + 1) * CHUNK, :], (CHUNK, cum.shape[1]))
         for ci in range(tm // CHUNK)], axis=0)

    r, k, v = r_ref[...].astype(f32), k_ref[...].astype(f32), v_ref[...].astype(f32)
    kk = k * kk_ref[...]
    kk = kk / jnp.maximum(jnp.sqrt(_head_sum(kk * kk, True)), 1e-12)
    kmod = k * (1.0 + (a - 1.0) * ka_ref[...])
    beta = kk * a
    e_neg = jnp.exp(-cum)
    e_end = jnp.exp(tot - cum)
    at_ref[...] = (-kk * jnp.exp(cum - lw)).astype(bf16)
    rt_ref[...] = (r * jnp.exp(cum)).astype(bf16)
    bt_ref[...] = (beta * e_neg).astype(bf16)
    kt_ref[...] = (kmod * e_neg).astype(bf16)
    bh_ref[...] = (beta * e_end).astype(bf16)
    kh_ref[...] = (kmod * e_end).astype(bf16)
    bonus = _head_sum(r * kmod * rk_ref[...], True) * v
    ga_ref[...] = (lnw_ref[...] * g).astype(bf16)
    gb_ref[...] = ((lnb_ref[...] + bonus) * g).astype(bf16)
    for ci in range(tm // CHUNK):
        gl_ref[ci] = jnp.exp(tot[ci * CHUNK:ci * CHUNK + 1, :])


def _rwkv_prep(proj, lora_block, w2p, a2p, g2p, w0, a0, k_k, k_a, r_k, ln_w, ln_b):
    T = proj.shape[0]
    D = w2p.shape[1]
    P = LORA_PAD
    tm = _pick(T, (128, 64))
    wspec = _resident((P, D))
    vspec = pl.BlockSpec((1, D), lambda i: (0, 0))
    ospec = pl.BlockSpec((tm, D), lambda i: (i, 0))
    o16 = jax.ShapeDtypeStruct((T, D), bf16)
    return pl.pallas_call(
        _rwkv_prep_kernel,
        out_shape=(o16,) * 6 + (jax.ShapeDtypeStruct((T // CHUNK, 1, D), f32), o16, o16),
        grid=(T // tm,),
        in_specs=[pl.BlockSpec((tm, 3 * P), lambda i: (i, lora_block)),
                  pl.BlockSpec((tm, D), lambda i: (i, 0)),
                  pl.BlockSpec((tm, D), lambda i: (i, 1)),
                  pl.BlockSpec((tm, D), lambda i: (i, 2)),
                  wspec, wspec, wspec] + [vspec] * 7,
        out_specs=(ospec,) * 6 + (pl.BlockSpec((tm // CHUNK, 1, D), lambda i: (i, 0, 0)), ospec, ospec),
        compiler_params=_params(("parallel",)),
        name="rwkv_prep",
    )(proj, proj, proj, proj, w2p, a2p, g2p, w0, a0, k_k, k_a, r_k, ln_w, ln_b)


def _rwkv_heads_chunk(heads):
    L, N = heads[0][6].shape
    assert L == N
    nh = len(heads)
    rng = range(nh)
    dot = functools.partial(jnp.dot, preferred_element_type=f32)
    row = lax.broadcasted_iota(jnp.int32, (L, 2 * L), 0)
    lane = lax.broadcasted_iota(jnp.int32, (L, 2 * L), 1)
    lo = lane < L
    tok = jnp.where(lo, lane, lane - L)
    incl = row >= tok
    strict = row > tok
    lhs = [jnp.concatenate([h[0], h[1]], axis=0) for h in heads]
    rhs = [jnp.concatenate([h[3], h[2]], axis=0) for h in heads]
    G = [lax.dot_general(lhs[i], rhs[i], NT_DIMS, preferred_element_type=f32) for i in rng]
    LS = [lax.dot_general(lhs[i], heads[i][7].astype(bf16), NT_DIMS, preferred_element_type=f32)
          for i in rng]
    top = [jnp.where(strict, g[:L], 0.0) for g in G]
    a_r = [jnp.where(incl, g[L:], 0.0).astype(bf16) for g in G]
    v2 = [jnp.concatenate([h[6], h[6]], axis=0) for h in heads]
    zero = jnp.zeros((L, N), f32)
    x0 = [LS[i][:L] + dot(jnp.where(lo, top[i], 0.0).astype(bf16), v2[i]) for i in rng]
    p = [jnp.where(lo, jnp.concatenate([x0[i], zero], axis=1), top[i]) for i in rng]
    steps = max(1, (L - 1).bit_length())
    for s in range(steps):
        z = [dot(pltpu.roll(p[i], N, 1)[:, :N].astype(bf16), p[i].astype(bf16))
             for i in rng]
        if s < steps - 1:
            p = [z[i] + jnp.where(lo, p[i], 0.0) for i in rng]
        else:
            u = [p[i][:, :N] + z[i][:, :N] for i in rng]
    vu = [jnp.concatenate([heads[i][6].astype(f32), u[i]], axis=0) for i in rng]
    o = [LS[i][L:] + dot(a_r[i], vu[i].astype(bf16)) for i in rng]
    S_new = [heads[i][7] * heads[i][8]
             + dot(vu[i].T.astype(bf16), jnp.concatenate([heads[i][5], heads[i][4]], axis=0))
             for i in rng]
    return list(zip(o, S_new))


def _rwkv_core_kernel(at_ref, rt_ref, bt_ref, kt_ref, bh_ref, kh_ref, v_ref, gl_ref, s0_ref,
                      ga_ref, gb_ref, y_ref, sout_ref, S_ref, *, npc, nhead):
    c = pl.program_id(1)
    N = RWKV_HS

    @pl.when(c == 0)
    def _():
        S_ref[...] = jnp.zeros_like(S_ref)

    @pl.when(c >= npc)
    def _():
        S_ref[...] = s0_ref[0]

    gl = gl_ref[0]
    heads = []
    for hh in range(nhead):
        sl = slice(hh * N, (hh + 1) * N)
        heads.append((at_ref[:, sl], rt_ref[:, sl], bt_ref[:, sl], kt_ref[:, sl], bh_ref[:, sl],
                      kh_ref[:, sl], v_ref[:, sl], S_ref[hh], gl[:, sl]))
    res = _rwkv_heads_chunk(heads)
    for hh in range(nhead):
        S_ref[hh] = res[hh][1]
    outs = [o for o, _ in res]
    cen = [o - jnp.mean(o, axis=-1, keepdims=True) for o in outs]
    nrm = [d * lax.rsqrt(jnp.mean(d * d, axis=-1, keepdims=True) + RWKV_GN_EPS) for d in cen]
    y_ref[...] = (jnp.concatenate(nrm, axis=1) * ga_ref[...].astype(f32)
                  + gb_ref[...].astype(f32)).astype(y_ref.dtype)

    @pl.when(jnp.logical_or(c == npc - 1, c >= npc))
    def _():
        sout_ref[0] = S_ref[...]


RWKV_HEADS_PER_STEP = 32


def _rwkv_core(ops, proj, v_block, gl, state, ga, gb, npc, nchunks):
    T, D = ga.shape
    N = RWKV_HS
    nh = D // N
    nb = state.shape[0]
    G = RWKV_HEADS_PER_STEP
    W = G * N
    tok = pl.BlockSpec((CHUNK, W), lambda p, c: (c, p))
    vtok = pl.BlockSpec((CHUNK, W), lambda p, c: (c, v_block * (D // W) + p))

    def sin_map(p, c):
        return (jnp.maximum(c - npc, 0), p, 0, 0)

    def sout_map(p, c):
        return (jnp.maximum(c - npc + 1, 0), p, 0, 0)

    return pl.pallas_call(
        functools.partial(_rwkv_core_kernel, npc=npc, nhead=G),
        out_shape=(jax.ShapeDtypeStruct((T, D), bf16),
                   jax.ShapeDtypeStruct((nb + 1, nh, N, N), f32)),
        grid=(nh // G, nchunks),
        in_specs=[tok] * 6 + [vtok, pl.BlockSpec((1, 1, W), lambda p, c: (c, 0, p)),
                              pl.BlockSpec((1, G, N, N), sin_map), tok, tok],
        out_specs=(tok, pl.BlockSpec((1, G, N, N), sout_map)),
        scratch_shapes=[pltpu.VMEM((G, N, N), f32)],
        compiler_params=_params(("parallel", "arbitrary")),
        name="rwkv_core",
    )(*ops, proj, gl, state, ga, gb)


def _positions(n_prompt, n_b, n_s):
    return jnp.concatenate([jnp.arange(n_prompt), jnp.tile(PAST_LEN + jnp.arange(n_s), n_b)]).astype(f32)


def _attn_tables(pos):
    half = ROT_DIM // 2
    inv = ROPE_THETA ** (-jnp.arange(half, dtype=f32) / half)
    ang = pos[:, None] * inv[None, :]
    cos, sin = jnp.cos(ang), jnp.sin(ang)
    T = pos.shape[0]
    zeros = jnp.zeros((T, half), f32)
    rest = HEAD_DIM - ROT_DIM
    c = jnp.concatenate([cos, cos, jnp.ones((T, rest), f32)], axis=1)
    s1 = jnp.concatenate([zeros, sin, jnp.zeros((T, rest), f32)], axis=1)
    s2 = jnp.concatenate([-sin, zeros, jnp.zeros((T, rest), f32)], axis=1)
    rep = 256 // HEAD_DIM
    return tuple(jnp.tile(t, (1, rep)) for t in (c, s1, s2))


def _ret_tables(pos):
    half = RET_DK // 2
    inv = RET_THETA ** (-jnp.arange(half, dtype=f32) / half)
    ang = pos[:, None] * inv[None, :]
    return jnp.cos(ang), jnp.sin(ang)


def _pad_cols(w, n):
    return jnp.pad(w, ((0, 0), (0, n - w.shape[1])))


def _pad_rows(w, n):
    return jnp.pad(w, ((0, n - w.shape[0]), (0, 0)))


def kernel(x_prompt, x_sample, cache_attn_k, cache_attn_v, state_ret, state_rwkv, state_rwkv_shift,
           norm_mix, norm_mlp, norm_final,
           attn_w_qkv, attn_sinks, attn_w_o,
           ret_w_in, ret_gn_w, ret_w_o,
           rwkv_mu, rwkv_w_rkv, rwkv_w_o, rwkv_w0, rwkv_w1, rwkv_w2, rwkv_a0, rwkv_a1, rwkv_a2,
           rwkv_g1, rwkv_g2, rwkv_k_k, rwkv_k_a, rwkv_r_k, rwkv_ln_w, rwkv_ln_b,
           mlp_w_up, mlp_w_down):
    bp, tp, D = x_prompt.shape
    nb, ns, _ = x_sample.shape
    assert bp == 1 and ns == CHUNK and tp % CHUNK == 0
    depth = norm_mix.shape[0]
    npc = tp // CHUNK
    nchunks = npc + nb
    T = tp + nb * ns
    qd = ATTN_HEADS * HEAD_DIM
    kvd = ATTN_KV_HEADS * HEAD_DIM

    x = (x_prompt.reshape(tp, D), x_sample.reshape(nb * ns, D))
    pos = _positions(tp, nb, ns)
    attn_tabs = _attn_tables(pos)
    ret_tabs = _ret_tables(pos)
    lg = jnp.log1p(-jnp.exp2(-5.0 - jnp.arange(RET_HEADS, dtype=f32)))
    tm_proj = _pick(T, (512, 256, 128, 64))
    tm_ret = _pick(T, (1024, 512, 256, 128, 64))
    w_up16 = mlp_w_up.astype(bf16)
    w_down16 = mlp_w_down.astype(bf16)

    kp_l, vp_l, ks_l, vs_l, rp_l, rs_l, wp_l, ws_l, shp_l, shs_l = ([] for _ in range(10))
    for i in range(depth):
        j, kind = divmod(i, 3)
        g_mix = norm_mix[i][None, :]
        if kind == 0:
            nsub = (qd + 2 * kvd) // (2 * EPI_COLS)
            routes = [[("q", 0, s * EPI_COLS) for s in range(nsub)],
                      [("q", 0, s * EPI_COLS) for s in range(nsub, qd // EPI_COLS)]
                      + [("k", 1, 0), ("plain", 1, EPI_COLS)]]
            q, kv = _norm_matmul(
                x, g_mix, attn_w_qkv[j].astype(bf16), attn_tabs, routes, _epi_attn,
                [jax.ShapeDtypeStruct((T, qd), bf16), jax.ShapeDtypeStruct((T, 2 * kvd), f32)],
                [pl.BlockSpec((tm_proj, qd), lambda i, jj: (i, 0)),
                 pl.BlockSpec((tm_proj, 2 * kvd), lambda i, jj: (i, 0))],
                tm_proj, "attn_qkv")
            k_new = kv[:, :kvd]
            v_new = kv[:, kvd:]

            def ext(new, cache):
                samp = jnp.concatenate([cache.reshape(nb, WINDOW, kvd), new[tp:].reshape(nb, ns, kvd)], axis=1)
                rows = jnp.concatenate([jnp.zeros((WINDOW, kvd), f32), new[:tp],
                                        samp.reshape(nb * (WINDOW + ns), kvd)], axis=0)
                r4 = rows.astype(bf16).reshape(rows.shape[0], ATTN_KV_HEADS, 1, HEAD_DIM)
                dup = jnp.broadcast_to(r4, (rows.shape[0], ATTN_KV_HEADS, 2, HEAD_DIM))
                return dup.reshape(rows.shape[0], 2 * kvd), samp

            kext, k_samp = ext(k_new, cache_attn_k[j])
            vext, v_samp = ext(v_new, cache_attn_v[j])
            o = _attention(q, kext, vext, attn_sinks[j], npc, nchunks)
            x = _matmul_residual(o, attn_w_o[j].astype(bf16), x, "attn_out")
            kp_l.append(k_new[tp - WINDOW:tp].reshape(1, WINDOW, ATTN_KV_HEADS, HEAD_DIM))
            vp_l.append(v_new[tp - WINDOW:tp].reshape(1, WINDOW, ATTN_KV_HEADS, HEAD_DIM))
            ks_l.append(k_samp[:, -WINDOW:].reshape(nb, WINDOW, ATTN_KV_HEADS, HEAD_DIM))
            vs_l.append(v_samp[:, -WINDOW:].reshape(nb, WINDOW, ATTN_KV_HEADS, HEAD_DIM))
        elif kind == 1:
            nq = RET_HEADS * RET_DK // 1024
            nv = RET_HEADS * RET_DV // 1024
            tile = lambda mode: [(mode, 0, s * EPI_COLS) for s in range(1024 // EPI_COLS)]
            routes = [tile("q")] * nq + [tile("k")] * nq + [tile("plain")] * (2 * nv)
            (proj,) = _norm_matmul(
                x, g_mix, ret_w_in[j].astype(bf16), ret_tabs, routes, _epi_ret,
                [jax.ShapeDtypeStruct((T, ret_w_in.shape[2]), bf16)],
                [pl.BlockSpec((tm_ret, 1024), lambda i, jj: (i, jj))], tm_ret, "ret_proj")
            gnw = ret_gn_w[j][None, :]
            lp = _pick(tp, (256, 128, 64))
            y_p, s_p = _retention(proj, state_ret[j], gnw, lg, 0, tp, lp, False, "ret_prompt")
            y_s, s_s = _retention(proj, state_ret[j], gnw, lg, tp, nb * ns, ns, True, "ret_sample")
            x = _matmul_residual2(y_p, y_s, ret_w_o[j].astype(bf16), x, "ret_out")
            rp_l.append(s_p)
            rs_l.append(s_s)
        else:
            starts = jnp.concatenate([jnp.zeros((npc, D), f32), state_rwkv_shift[j]], axis=0)[:, None, :]
            P = LORA_PAD
            tn = 512
            wcat = jnp.concatenate(
                [rwkv_w_rkv[j][0], rwkv_w_rkv[j][1], rwkv_w_rkv[j][2],
                 _pad_cols(rwkv_w1[j], P), _pad_cols(rwkv_a1[j], P), _pad_cols(rwkv_g1[j], P)],
                axis=1).astype(bf16)
            tiles_per_d = D // tn
            mu = rwkv_mu[j]
            mu_tiles = jnp.concatenate(
                [jnp.repeat(mu[jnp.array([0, 2, 3])], tiles_per_d, axis=0), mu[jnp.array([1, 4, 5])]],
                axis=0)[:, None, :]
            proj, h_last = _lerp_matmul(x, g_mix, starts, mu_tiles, wcat, tn, tiles_per_d,
                                        3 * tiles_per_d, npc)
            prep = _rwkv_prep(
                proj, 3 * D // (3 * P),
                _pad_rows(rwkv_w2[j], P).astype(bf16), _pad_rows(rwkv_a2[j], P).astype(bf16),
                _pad_rows(rwkv_g2[j], P).astype(bf16), rwkv_w0[j][None, :], rwkv_a0[j][None, :],
                rwkv_k_k[j][None, :], rwkv_k_a[j][None, :], rwkv_r_k[j].reshape(1, D),
                rwkv_ln_w[j][None, :], rwkv_ln_b[j][None, :])
            ops, gl, ga, gb = prep[:6], prep[6], prep[7], prep[8]
            y, s_all = _rwkv_core(ops, proj, 2, gl, state_rwkv[j], ga, gb, npc, nchunks)
            x = _matmul_residual(y, rwkv_w_o[j].astype(bf16), x, "rwkv_out")
            wp_l.append(s_all[:1])
            ws_l.append(s_all[1:])
            shp_l.append(h_last[npc - 1])
            shs_l.append(h_last[npc:, 0])
        x = _mlp(x, norm_mlp[i][None, :], w_up16, w_down16, i, norm_final[None, :],
                 tp if i == depth - 1 else None, "mlp")

    y_prompt = x[0].reshape(1, tp, D)
    y_sample = x[1].reshape(nb, ns, D)
    return (y_prompt, y_sample,
            jnp.stack(kp_l), jnp.stack(vp_l), jnp.stack(ks_l), jnp.stack(vs_l),
            jnp.stack(rp_l), jnp.stack(rs_l),
            jnp.stack(wp_l), jnp.stack(ws_l), jnp.stack(shp_l), jnp.stack(shs_l))
```

```python
import functools

import jax
import jax.numpy as jnp
from jax import lax
from jax.experimental import pallas as pl
from jax.experimental.pallas import tpu as pltpu

f32 = jnp.float32
bf16 = jnp.bfloat16

CHUNK = 64
NORM_EPS = 1e-5
PAST_LEN = 4096

ATTN_HEADS = 32
ATTN_KV_HEADS = 4
ATTN_GROUP = ATTN_HEADS // ATTN_KV_HEADS
HEAD_DIM = 64
WINDOW = 128
ROT_DIM = HEAD_DIM // 4
ROPE_THETA = 500000.0

RET_HEADS = 8
RET_DK = 256
RET_DV = 512
RET_THETA = 10000.0
RET_GN_EPS = 1e-5

RWKV_HS = 64
RWKV_GN_EPS = 64e-5
LORA_PAD = 512

VMEM_LIMIT = 52 * 1024 * 1024

NT_DIMS = (((1,), (1,)), ((), ()))


def _pick(n, cands):
    for c in cands:
        if n % c == 0:
            return c
    raise ValueError(f"no tile for {n} in {cands}")


def _params(sem):
    return pltpu.CompilerParams(dimension_semantics=sem, vmem_limit_bytes=VMEM_LIMIT)


def _rms(x, g):
    return x * lax.rsqrt(jnp.mean(x * x, axis=-1, keepdims=True) + NORM_EPS) * g


def _epi_attn(blk, mode, tabs):
    if mode == "plain":
        return blk
    c_ref, s1_ref, s2_ref = tabs
    out = (blk * c_ref[...] + pltpu.roll(blk, 8, 1) * s1_ref[...]
           + pltpu.roll(blk, 256 - 8, 1) * s2_ref[...])
    return out * (HEAD_DIM ** -0.5) if mode == "q" else out


def _epi_ret(blk, mode, tabs):
    if mode == "plain":
        return blk
    cos_ref, sin_ref = tabs
    c, s = cos_ref[...], sin_ref[...]
    x1, x2 = blk[:, :128], blk[:, 128:]
    out = jnp.concatenate([x1 * c - x2 * s, x2 * c + x1 * s], axis=1)
    return out * (RET_DK ** -0.5) if mode == "k" else out


EPI_COLS = 256


def _row_split_specs(tm, cols, split_at):
    first = pl.BlockSpec((tm, cols), lambda i, *_: (jnp.minimum(i, split_at - 1), 0))
    second = pl.BlockSpec((tm, cols), lambda i, *_: (jnp.maximum(i - split_at, 0), 0))
    return [first, second]


def _normmm_kernel(*refs, routes, ntab, nout, epi, split_at):
    nx = 1 if split_at is None else 2
    x_refs = refs[:nx]
    g_ref, w_ref = refs[nx:nx + 2]
    tabs = refs[nx + 2:nx + 2 + ntab]
    outs = refs[nx + 2 + ntab:nx + 2 + ntab + nout]
    xn_ref = refs[nx + 2 + ntab + nout]
    i = pl.program_id(0)
    j = pl.program_id(1)

    if split_at is None:
        @pl.when(j == 0)
        def _():
            xn_ref[...] = _rms(x_refs[0][...], g_ref[...]).astype(bf16)
    else:
        @pl.when(jnp.logical_and(j == 0, i < split_at))
        def _():
            xn_ref[...] = _rms(x_refs[0][...], g_ref[...]).astype(bf16)

        @pl.when(jnp.logical_and(j == 0, i >= split_at))
        def _():
            xn_ref[...] = _rms(x_refs[1][...], g_ref[...]).astype(bf16)

    groups = {}
    for jj, tile_routes in enumerate(routes):
        groups.setdefault(tuple(tile_routes), []).append(jj)
    for tile_routes, jjs in groups.items():
        cond = functools.reduce(jnp.logical_or, [j == jj for jj in jjs])

        @pl.when(cond)
        def _(tile_routes=tile_routes):
            for s, (mode, oi, col) in enumerate(tile_routes):
                acc = jnp.dot(xn_ref[...], w_ref[:, s * EPI_COLS:(s + 1) * EPI_COLS],
                              preferred_element_type=f32)
                outs[oi][:, col:col + EPI_COLS] = epi(acc, mode, tabs).astype(outs[oi].dtype)


def _norm_matmul(x, g, w, tabs, routes, epi, out_shapes, out_specs, tm, name):
    xs = x if isinstance(x, tuple) else (x,)
    T = sum(p.shape[0] for p in xs)
    D = xs[0].shape[1]
    N = w.shape[1]
    ntile = len(routes)
    tn = N // ntile
    assert tn * ntile == N and tn == EPI_COLS * len(routes[0])
    if len(xs) == 1:
        split_at = None
        x_specs = [pl.BlockSpec((tm, D), lambda i, j: (i, 0))]
    else:
        assert xs[0].shape[0] % tm == 0 and xs[1].shape[0] % tm == 0
        split_at = xs[0].shape[0] // tm
        x_specs = _row_split_specs(tm, D, split_at)
    tab_specs = [pl.BlockSpec((tm, t.shape[1]), lambda i, j: (i, 0)) for t in tabs]
    return pl.pallas_call(
        functools.partial(_normmm_kernel, routes=routes, ntab=len(tabs), nout=len(out_shapes), epi=epi,
                          split_at=split_at),
        out_shape=tuple(out_shapes),
        grid=(T // tm, ntile),
        in_specs=x_specs + [pl.BlockSpec((1, D), lambda i, j: (0, 0)),
                            pl.BlockSpec((D, tn), lambda i, j: (0, j))] + tab_specs,
        out_specs=tuple(out_specs),
        scratch_shapes=[pltpu.VMEM((tm, D), bf16)],
        compiler_params=_params(("parallel", "arbitrary")),
        name=name,
    )(*xs, g, w, *tabs)


def _mmres_kernel(a_ref, w_ref, *rest, split_at):
    o_ref = rest[-1]
    prod = jnp.dot(a_ref[...], w_ref[...], preferred_element_type=f32)
    if split_at is None:
        o_ref[...] = rest[0][...] + prod
    else:
        i = pl.program_id(0)

        @pl.when(i < split_at)
        def _():
            o_ref[...] = rest[0][...] + prod

        @pl.when(i >= split_at)
        def _():
            o_ref[...] = rest[1][...] + prod


def _resident(shape):
    return pl.BlockSpec(shape, lambda *_: (0,) * len(shape), pipeline_mode=pl.Buffered(1))


def _matmul_residual(a, w, res, name):
    T, K = a.shape
    N = w.shape[1]
    rs = res if isinstance(res, tuple) else (res,)
    if len(rs) == 1:
        tm = _pick(T, (512, 256, 128, 64))
        split_at = None
        r_specs = [pl.BlockSpec((tm, N), lambda i: (i, 0))]
    else:
        tm = _pick(rs[1].shape[0], (512, 256, 128, 64))
        assert rs[0].shape[0] % tm == 0
        split_at = rs[0].shape[0] // tm
        r_specs = _row_split_specs(tm, N, split_at)
    return pl.pallas_call(
        functools.partial(_mmres_kernel, split_at=split_at),
        out_shape=jax.ShapeDtypeStruct((T, N), f32),
        grid=(T // tm,),
        in_specs=[pl.BlockSpec((tm, K), lambda i: (i, 0)), _resident((K, N))] + r_specs,
        out_specs=pl.BlockSpec((tm, N), lambda i: (i, 0)),
        compiler_params=_params(("parallel",)),
        name=name,
    )(a, w, *rs)


def _mmres2_kernel(ap_ref, as_ref, w_ref, r_ref, o_ref, *, np_blocks):
    i = pl.program_id(0)

    @pl.when(i < np_blocks)
    def _():
        o_ref[...] = r_ref[...] + jnp.dot(ap_ref[...], w_ref[...], preferred_element_type=f32)

    @pl.when(i >= np_blocks)
    def _():
        o_ref[...] = r_ref[...] + jnp.dot(as_ref[...], w_ref[...], preferred_element_type=f32)


def _matmul_residual2(a_p, a_s, w, res, name):
    tp, K = a_p.shape
    ts = a_s.shape[0]
    N = w.shape[1]
    tm = _pick(ts, (256, 128, 64))
    assert tp % tm == 0
    np_blocks = tp // tm
    return pl.pallas_call(
        functools.partial(_mmres2_kernel, np_blocks=np_blocks),
        out_shape=jax.ShapeDtypeStruct((tp + ts, N), f32),
        grid=((tp + ts) // tm,),
        in_specs=[pl.BlockSpec((tm, K), lambda i: (jnp.minimum(i, np_blocks - 1), 0)),
                  pl.BlockSpec((tm, K), lambda i: (jnp.maximum(i - np_blocks, 0), 0)),
                  _resident((K, N)),
                  pl.BlockSpec((tm, N), lambda i: (i, 0))],
        out_specs=pl.BlockSpec((tm, N), lambda i: (i, 0)),
        compiler_params=_params(("parallel",)),
        name=name,
    )(a_p, a_s, w, res)


def _mlp_kernel(x_ref, g_ref, wu_ref, wd_ref, gf_ref, *rest, nk, split_at):
    if split_at is None:
        acc_ref, xn_ref = rest
    else:
        op_ref, os_ref, xn_ref, acc_ref = rest
    i = pl.program_id(0)
    k = pl.program_id(1)

    @pl.when(k == 0)
    def _():
        x = x_ref[...]
        xn_ref[...] = _rms(x, g_ref[...]).astype(bf16)
        acc_ref[...] = x

    h = jnp.dot(xn_ref[...], wu_ref[...], preferred_element_type=f32)
    h = jnp.square(jnp.maximum(h, 0.0)).astype(bf16)
    acc_ref[...] += jnp.dot(h, wd_ref[...], preferred_element_type=f32)

    if split_at is not None:
        @pl.when(jnp.logical_and(k == nk - 1, i < split_at))
        def _():
            op_ref[...] = _rms(acc_ref[...], gf_ref[...])

        @pl.when(jnp.logical_and(k == nk - 1, i >= split_at))
        def _():
            os_ref[...] = _rms(acc_ref[...], gf_ref[...])


def _mlp(x, g, w_up, w_down, layer, g_final, split_rows, name):
    T, D = x.shape
    F = w_up.shape[2]
    tf = _pick(F, (1024, 512))
    nk = F // tf
    if split_rows is None:
        tm = _pick(T, (512, 256, 128, 64))
        x_spec = pl.BlockSpec((tm, D), lambda i, k: (i, 0))
        split_at = None
        out_shape = jax.ShapeDtypeStruct((T, D), f32)
        out_specs = pl.BlockSpec((tm, D), lambda i, k: (i, 0))
        scratch = [pltpu.VMEM((tm, D), bf16)]
    else:
        tm = _pick(T - split_rows, (512, 256, 128, 64))
        x_spec = pl.BlockSpec((tm, D), lambda i, k: (i, 0))
        assert split_rows % tm == 0
        split_at = split_rows // tm
        out_shape = (jax.ShapeDtypeStruct((split_rows, D), f32), jax.ShapeDtypeStruct((T - split_rows, D), f32))
        out_specs = (pl.BlockSpec((tm, D), lambda i, k: (jnp.minimum(i, split_at - 1), 0)),
                     pl.BlockSpec((tm, D), lambda i, k: (jnp.maximum(i - split_at, 0), 0)))
        scratch = [pltpu.VMEM((tm, D), bf16), pltpu.VMEM((tm, D), f32)]
    return pl.pallas_call(
        functools.partial(_mlp_kernel, nk=nk, split_at=split_at),
        out_shape=out_shape,
        grid=(T // tm, nk),
        in_specs=[x_spec,
                  pl.BlockSpec((1, D), lambda i, k: (0, 0)),
                  pl.BlockSpec((None, D, tf), lambda i, k: (layer, 0, k)),
                  pl.BlockSpec((None, tf, D), lambda i, k: (layer, k, 0)),
                  pl.BlockSpec((1, D), lambda i, k: (0, 0))],
        out_specs=out_specs,
        scratch_shapes=scratch,
        compiler_params=_params(("arbitrary", "arbitrary")),
        name=name,
    )(x, g, w_up, w_down, g_final)


def _attn_kernel(sb_ref, q_ref, k0_ref, k1_ref, k2_ref, v0_ref, v1_ref, v2_ref, o_ref, *, npc):
    c = pl.program_id(0)
    nkeys = 3 * CHUNK
    ncols = nkeys + CHUNK
    PW = 2 * HEAD_DIM
    pairs = ATTN_GROUP // 2
    col_blk = lax.broadcasted_iota(jnp.int32, (1, ncols), 1) // CHUNK
    valid = jnp.logical_or(jnp.logical_or(c >= npc, c + col_blk >= 2), col_blk >= 3)
    bias = jnp.where(valid, 0.0, -jnp.inf).astype(f32)
    zpad = jnp.zeros((CHUNK, ATTN_KV_HEADS * PW), bf16)
    kcat = jnp.concatenate([k0_ref[...], k1_ref[...], k2_ref[...], zpad], axis=0)
    vcat = jnp.concatenate([v0_ref[...], v1_ref[...], v2_ref[...], zpad], axis=0)
    lo_half = lax.broadcasted_iota(jnp.int32, (1, PW), 1) < HEAD_DIM
    ones = jnp.ones((ncols, PW), bf16)
    kvs = range(ATTN_KV_HEADS)

    def stacked_q(kv):
        parts = []
        for p in range(pairs):
            col = (kv * pairs + p) * PW
            qp = q_ref[:, col:col + PW]
            parts += [jnp.where(lo_half, qp, 0.0), jnp.where(lo_half, 0.0, qp)]
        return jnp.concatenate(parts, axis=0).astype(bf16)

    s = [lax.dot_general(stacked_q(kv), kcat[:, kv * PW:(kv + 1) * PW], NT_DIMS,
                         preferred_element_type=f32) + sb_ref[kv] + bias for kv in kvs]
    p = [jnp.exp(s[kv] - jnp.max(s[kv], axis=-1, keepdims=True)).astype(bf16) for kv in kvs]
    oa = [jnp.dot(p[kv], jnp.concatenate([vcat[:, kv * PW:(kv + 1) * PW], ones], axis=1),
                  preferred_element_type=f32) for kv in kvs]
    for kv in kvs:
        on = oa[kv][:, :PW] / oa[kv][:, PW:]
        for pi in range(pairs):
            r0 = 2 * pi * CHUNK
            blk = jnp.where(lo_half, on[r0:r0 + CHUNK], on[r0 + CHUNK:r0 + 2 * CHUNK])
            col = (kv * pairs + pi) * PW
            o_ref[:, col:col + PW] = blk.astype(o_ref.dtype)


def _attention(qkv, kext, vext, sinks, npc, nchunks):
    T = qkv.shape[0]
    qd = ATTN_HEADS * HEAD_DIM
    kvd = kext.shape[1]

    def kv_spec(j):
        def imap(c):
            return (jnp.where(c < npc, c, npc + 2 + 3 * (c - npc)) + j, 0)
        return pl.BlockSpec((CHUNK, kvd), imap)

    nkeys = 3 * CHUNK
    col = jnp.arange(nkeys + CHUNK)[None, None, :]
    sink_rows = jnp.repeat(sinks.astype(f32).reshape(ATTN_KV_HEADS, ATTN_GROUP), CHUNK, axis=1)[:, :, None]
    sink_bias = jnp.where(col < nkeys, 0.0, jnp.where(col == nkeys, sink_rows, -jnp.inf)).astype(f32)

    return pl.pallas_call(
        functools.partial(_attn_kernel, npc=npc),
        out_shape=jax.ShapeDtypeStruct((T, qd), bf16),
        grid=(nchunks,),
        in_specs=[pl.BlockSpec(sink_bias.shape, lambda c: (0, 0, 0)),
                  pl.BlockSpec((CHUNK, qd), lambda c: (c, 0))]
                 + [kv_spec(j) for j in range(3)] + [kv_spec(j) for j in range(3)],
        out_specs=pl.BlockSpec((CHUNK, qd), lambda c: (c, 0)),
        compiler_params=_params(("parallel",)),
        name="attn_core",
    )(sink_bias, qkv, kext, kext, kext, vext, vext, vext)


RET_HEADS_PER_STEP = 4


def _ret_kernel(lg_ref, q_ref, k_ref, v_ref, g_ref, s0_ref, gnw_ref, y_ref, sout_ref, S_ref, *, from_state):
    hg = pl.program_id(0)
    c = pl.program_id(1)
    L = q_ref.shape[0]
    nh = RET_HEADS_PER_STEP
    hs = range(nh)
    dot = functools.partial(jnp.dot, preferred_element_type=f32)

    if from_state:
        S_ref[...] = s0_ref[0]
    else:
        @pl.when(c == 0)
        def _():
            S_ref[...] = jnp.zeros_like(S_ref)

    lg = [lg_ref[hg * nh + h] for h in hs]
    q = [q_ref[:, h * RET_DK:(h + 1) * RET_DK] for h in hs]
    k = [k_ref[:, h * RET_DK:(h + 1) * RET_DK] for h in hs]
    v = [v_ref[:, h * RET_DV:(h + 1) * RET_DV] for h in hs]
    row = lax.broadcasted_iota(jnp.int32, (L, L), 0)
    col = lax.broadcasted_iota(jnp.int32, (L, L), 1)
    diff = (row - col).astype(f32)
    idx = lax.broadcasted_iota(jnp.int32, (L, 1), 0).astype(f32)
    decay = [jnp.where(diff >= 0, jnp.exp(lg[h] * jnp.maximum(diff, 0.0)), 0.0) for h in hs]
    xi = [jnp.exp(lg[h] * (idx + 1.0)) for h in hs]
    zeta = [jnp.exp(lg[h] * (L - 1.0 - idx)) for h in hs]

    S = [S_ref[h] for h in hs]
    scores = [lax.dot_general(q[h], k[h], NT_DIMS, preferred_element_type=f32) * decay[h] for h in hs]
    o = [dot(scores[h].astype(bf16), v[h])
         + dot((q[h].astype(f32) * xi[h]).astype(bf16), S[h].astype(bf16)) for h in hs]
    S_new = [jnp.exp(lg[h] * L) * S[h] + dot((k[h].astype(f32) * zeta[h]).T.astype(bf16), v[h])
             for h in hs]
    for h in hs:
        S_ref[h] = S_new[h]
        sout_ref[0, h] = S_new[h]

    cen = [o[h] - jnp.mean(o[h], axis=-1, keepdims=True) for h in hs]
    on = [cen[h] * lax.rsqrt(jnp.mean(cen[h] * cen[h], axis=-1, keepdims=True) + RET_GN_EPS) for h in hs]
    for h in hs:
        sl = slice(h * RET_DV, (h + 1) * RET_DV)
        y_ref[:, sl] = (jax.nn.silu(g_ref[:, sl].astype(f32)) * on[h] * gnw_ref[:, sl]).astype(y_ref.dtype)


def _retention(proj, state, gn_w, lg, row0, nrows, L, from_state, name):
    nsteps = nrows // L
    b0 = row0 // L
    assert nsteps * L == nrows and b0 * L == row0
    nseq = nsteps if from_state else 1
    G = RET_HEADS_PER_STEP
    kw, vw = G * RET_DK, G * RET_DV
    kb = RET_HEADS * RET_DK // kw
    vb = 2 * RET_HEADS * RET_DK // vw
    gb = vb + RET_HEADS * RET_DV // vw
    state_map = (lambda h, c: (c, h, 0, 0)) if from_state else (lambda h, c: (0, h, 0, 0))

    return pl.pallas_call(
        functools.partial(_ret_kernel, from_state=from_state),
        out_shape=(jax.ShapeDtypeStruct((nrows, RET_HEADS * RET_DV), bf16),
                   jax.ShapeDtypeStruct((nseq, RET_HEADS, RET_DK, RET_DV), f32)),
        grid=(RET_HEADS // G, nsteps),
        in_specs=[pl.BlockSpec(memory_space=pltpu.SMEM),
                  pl.BlockSpec((L, kw), lambda h, c: (b0 + c, h)),
                  pl.BlockSpec((L, kw), lambda h, c: (b0 + c, kb + h)),
                  pl.BlockSpec((L, vw), lambda h, c: (b0 + c, vb + h)),
                  pl.BlockSpec((L, vw), lambda h, c: (b0 + c, gb + h)),
                  pl.BlockSpec((1, G, RET_DK, RET_DV), state_map),
                  pl.BlockSpec((1, vw), lambda h, c: (0, h))],
        out_specs=(pl.BlockSpec((L, vw), lambda h, c: (c, h)),
                   pl.BlockSpec((1, G, RET_DK, RET_DV), state_map)),
        scratch_shapes=[pltpu.VMEM((G, RET_DK, RET_DV), f32)],
        compiler_params=_params(("parallel", "arbitrary")),
        name=name,
    )(lg, proj, proj, proj, proj, state, gn_w)


SUBLANES = 8


def _lerpmm_kernel(x_ref, xprev_ref, g_ref, start_ref, mu_ref, w_ref, o_ref, hlast_ref,
                   h_ref, xx_ref, l_ref, *, tiles_per_d, n_big, npc):
    i = pl.program_id(0)
    j = pl.program_id(1)
    tm = x_ref.shape[0]
    cpb = tm // CHUNK

    @pl.when(j == 0)
    def _():
        g = g_ref[...]
        h = _rms(x_ref[...], g)
        h_ref[...] = h
        xx_ref[...] = pltpu.roll(h, 1, 0) - h
        prev = _rms(xprev_ref[...], g)[SUBLANES - 1:SUBLANES, :]
        xx_ref[0:1, :] = prev - h[0:1, :]
        for ci in range(cpb):
            gc = i * cpb + ci
            r0 = ci * CHUNK

            @pl.when(jnp.logical_or(gc == 0, gc >= npc))
            def _(ci=ci, r0=r0):
                xx_ref[r0:r0 + 1, :] = start_ref[ci] - h_ref[r0:r0 + 1, :]

            hlast_ref[ci] = h[r0 + CHUNK - 1:r0 + CHUNK, :]

    @pl.when(jnp.logical_or(j % tiles_per_d == 0, j >= n_big))
    def _():
        l_ref[...] = (h_ref[...] + xx_ref[...] * mu_ref[0]).astype(bf16)

    o_ref[...] = jnp.dot(l_ref[...], w_ref[...], preferred_element_type=f32).astype(o_ref.dtype)


def _lerp_matmul(x, g, starts, mu_tiles, wcat, tn, tiles_per_d, n_big, npc):
    T, D = x.shape
    N = wcat.shape[1]
    tm = _pick(T, (1024, 512, 256, 128, 64))
    cpb = tm // CHUNK
    return pl.pallas_call(
        functools.partial(_lerpmm_kernel, tiles_per_d=tiles_per_d, n_big=n_big, npc=npc),
        out_shape=(jax.ShapeDtypeStruct((T, N), bf16),
                   jax.ShapeDtypeStruct((T // CHUNK, 1, D), f32)),
        grid=(T // tm, N // tn),
        in_specs=[pl.BlockSpec((tm, D), lambda i, j: (i, 0)),
                  pl.BlockSpec((SUBLANES, D), lambda i, j: (jnp.maximum(i * (tm // SUBLANES) - 1, 0), 0)),
                  pl.BlockSpec((1, D), lambda i, j: (0, 0)),
                  pl.BlockSpec((cpb, 1, D), lambda i, j: (i, 0, 0)),
                  pl.BlockSpec((1, 1, D), lambda i, j: (j, 0, 0)),
                  pl.BlockSpec((D, tn), lambda i, j: (0, j))],
        out_specs=(pl.BlockSpec((tm, tn), lambda i, j: (i, j)),
                   pl.BlockSpec((cpb, 1, D), lambda i, j: (i, 0, 0))),
        scratch_shapes=[pltpu.VMEM((tm, D), f32), pltpu.VMEM((tm, D), f32), pltpu.VMEM((tm, D), bf16)],
        compiler_params=_params(("parallel", "arbitrary")),
        name="rwkv_proj",
    )(x, x, g, starts, mu_tiles, wcat)


def _softplus(z):
    return jnp.maximum(z, 0.0) + jnp.log(1.0 + jnp.exp(-jnp.abs(z)))


SEG_TILE = 256


def _head_sum(x, split):
    rows, D = x.shape
    r = lax.broadcasted_iota(jnp.int32, (SEG_TILE, SEG_TILE), 0) // RWKV_HS
    c = lax.broadcasted_iota(jnp.int32, (SEG_TILE, SEG_TILE), 1) // RWKV_HS
    ones = (r == c).astype(bf16)
    out = []
    for j in range(D // SEG_TILE):
        blk = x[:, j * SEG_TILE:(j + 1) * SEG_TILE]
        hi = blk.astype(bf16)
        s = jnp.dot(hi, ones, preferred_element_type=f32)
        if split:
            lo = (blk - hi.astype(f32)).astype(bf16)
            s = s + jnp.dot(lo, ones, preferred_element_type=f32)
        out.append(s)
    return jnp.concatenate(out, axis=1)


def _chunk_cumsum(x):
    rows = x.shape[0]
    row = lax.broadcasted_iota(jnp.int32, (rows, rows), 0)
    col = lax.broadcasted_iota(jnp.int32, (rows, rows), 1)
    tri = jnp.logical_and((row // CHUNK) == (col // CHUNK), row >= col).astype(bf16)
    hi = x.astype(bf16)
    rest = x - hi.astype(f32)
    mid = rest.astype(bf16)
    lo = (rest - mid.astype(f32)).astype(bf16)
    dot = functools.partial(jnp.dot, preferred_element_type=f32)
    return dot(tri, hi) + dot(tri, mid) + dot(tri, lo)


def _rwkv_prep_kernel(p_ref, r_ref, k_ref, v_ref, w2_ref, a2_ref, g2_ref, w0_ref, a0_ref,
                      kk_ref, ka_ref, rk_ref, lnw_ref, lnb_ref,
                      at_ref, rt_ref, bt_ref, kt_ref, bh_ref, kh_ref, gl_ref, ga_ref, gb_ref):
    P = LORA_PAD
    tm = r_ref.shape[0]
    pw = jnp.tanh(p_ref[:, :P].astype(f32)).astype(bf16)
    pa = p_ref[:, P:2 * P]
    pg = jax.nn.sigmoid(p_ref[:, 2 * P:].astype(f32)).astype(bf16)
    wl = w0_ref[...] + jnp.dot(pw, w2_ref[...], preferred_element_type=f32)
    lw = -jnp.exp(-_softplus(-wl) - 0.5)
    a = jax.nn.sigmoid(a0_ref[...] + jnp.dot(pa, a2_ref[...], preferred_element_type=f32))
    g = jnp.dot(pg, g2_ref[...], preferred_element_type=f32)

    cum = _chunk_cumsum(lw)
    tot = jnp.concatenate(
        [jnp.broadcast_to(cum[(ci + 1) * CHUNK - 1:(ci + 1) * CHUNK, :], (CHUNK, cum.shape[1]))
         for ci in range(tm // CHUNK)], axis=0)

    r, k, v = r_ref[...].astype(f32), k_ref[...].astype(f32), v_ref[...].astype(f32)
    kk = k * kk_ref[...]
    kk = kk / jnp.maximum(jnp.sqrt(_head_sum(kk * kk, True)), 1e-12)
    kmod = k * (1.0 + (a - 1.0) * ka_ref[...])
    beta = kk * a
    e_neg = jnp.exp(-cum)
    e_end = jnp.exp(tot - cum)
    at_ref[...] = (-kk * jnp.exp(cum - lw)).astype(bf16)
    rt_ref[...] = (r * jnp.exp(cum)).astype(bf16)
    bt_ref[...] = (beta * e_neg).astype(bf16)
    kt_ref[...] = (kmod * e_neg).astype(bf16)
    bh_ref[...] = (beta * e_end).astype(bf16)
    kh_ref[...] = (kmod * e_end).astype(bf16)
    bonus = _head_sum(r * kmod * rk_ref[...], True) * v
    ga_ref[...] = (lnw_ref[...] * g).astype(bf16)
    gb_ref[...] = ((lnb_ref[...] + bonus) * g).astype(bf16)
    for ci in range(tm // CHUNK):
        gl_ref[ci] = jnp.exp(tot[ci * CHUNK:ci * CHUNK + 1, :])


def _rwkv_prep(proj, lora_block, w2p, a2p, g2p, w0, a0, k_k, k_a, r_k, ln_w, ln_b):
    T = proj.shape[0]
    D = w2p.shape[1]
    P = LORA_PAD
    tm = _pick(T, (128, 64))
    wspec = _resident((P, D))
    vspec = pl.BlockSpec((1, D), lambda i: (0, 0))
    ospec = pl.BlockSpec((tm, D), lambda i: (i, 0))
    o16 = jax.ShapeDtypeStruct((T, D), bf16)
    return pl.pallas_call(
        _rwkv_prep_kernel,
        out_shape=(o16,) * 6 + (jax.ShapeDtypeStruct((T // CHUNK, 1, D), f32), o16, o16),
        grid=(T // tm,),
        in_specs=[pl.BlockSpec((tm, 3 * P), lambda i: (i, lora_block)),
                  pl.BlockSpec((tm, D), lambda i: (i, 0)),
                  pl.BlockSpec((tm, D), lambda i: (i, 1)),
                  pl.BlockSpec((tm, D), lambda i: (i, 2)),
                  wspec, wspec, wspec] + [vspec] * 7,
        out_specs=(ospec,) * 6 + (pl.BlockSpec((tm // CHUNK, 1, D), lambda i: (i, 0, 0)), ospec, ospec),
        compiler_params=_params(("parallel",)),
        name="rwkv_prep",
    )(proj, proj, proj, proj, w2p, a2p, g2p, w0, a0, k_k, k_a, r_k, ln_w, ln_b)


def _rwkv_heads_chunk(heads):
    L, N = heads[0][6].shape
    assert L == N
    nh = len(heads)
    rng = range(nh)
    dot = functools.partial(jnp.dot, preferred_element_type=f32)
    row = lax.broadcasted_iota(jnp.int32, (L, 2 * L), 0)
    lane = lax.broadcasted_iota(jnp.int32, (L, 2 * L), 1)
    lo = lane < L
    tok = jnp.where(lo, lane, lane - L)
    incl = row >= tok
    strict = row > tok
    lhs = [jnp.concatenate([h[0], h[1]], axis=0) for h in heads]
    rhs = [jnp.concatenate([h[2], h[3]], axis=0) for h in heads]
    G = [lax.dot_general(lhs[i], rhs[i], NT_DIMS, preferred_element_type=f32) for i in rng]
    LS = [lax.dot_general(lhs[i], heads[i][7].astype(bf16), NT_DIMS, preferred_element_type=f32)
          for i in rng]
    top = [jnp.where(strict, g[:L], 0.0) for g in G]
    a_r = [jnp.where(incl, g[L:], 0.0).astype(bf16) for g in G]
    v2 = [jnp.concatenate([h[6], h[6]], axis=0) for h in heads]
    zero = jnp.zeros((L, N), f32)
    x0 = [LS[i][:L] + dot(jnp.where(lo, 0.0, top[i]).astype(bf16), v2[i]) for i in rng]
    p = [jnp.where(lo, top[i], jnp.concatenate([zero, x0[i]], axis=1)) for i in rng]
    steps = max(1, (L - 1).bit_length())
    for s in range(steps):
        z = [dot(p[i][:, :L].astype(bf16), p[i].astype(bf16)) for i in rng]
        if s < steps - 1:
            p = [z[i] + jnp.where(lo, 0.0, p[i]) for i in rng]
        else:
            u = [(p[i] + z[i])[:, L:] for i in rng]
    uv = [jnp.concatenate([u[i], heads[i][6].astype(f32)], axis=0) for i in rng]
    o = [LS[i][L:] + dot(a_r[i], uv[i].astype(bf16)) for i in rng]
    S_new = [heads[i][7] * heads[i][8]
             + dot(uv[i].T.astype(bf16), jnp.concatenate([heads[i][4], heads[i][5]], axis=0))
             for i in rng]
    return list(zip(o, S_new))


def _rwkv_core_kernel(at_ref, rt_ref, bt_ref, kt_ref, bh_ref, kh_ref, v_ref, gl_ref, s0_ref,
                      ga_ref, gb_ref, y_ref, sout_ref, S_ref, *, npc, nhead):
    c = pl.program_id(1)
    N = RWKV_HS

    @pl.when(c == 0)
    def _():
        S_ref[...] = jnp.zeros_like(S_ref)

    @pl.when(c >= npc)
    def _():
        S_ref[...] = s0_ref[0]

    gl = gl_ref[0]
    heads = []
    for hh in range(nhead):
        sl = slice(hh * N, (hh + 1) * N)
        heads.append((at_ref[:, sl], rt_ref[:, sl], bt_ref[:, sl], kt_ref[:, sl], bh_ref[:, sl],
                      kh_ref[:, sl], v_ref[:, sl], S_ref[hh], gl[:, sl]))
    res = _rwkv_heads_chunk(heads)
    for hh in range(nhead):
        S_ref[hh] = res[hh][1]
    outs = [o for o, _ in res]
    cen = [o - jnp.mean(o, axis=-1, keepdims=True) for o in outs]
    nrm = [d * lax.rsqrt(jnp.mean(d * d, axis=-1, keepdims=True) + RWKV_GN_EPS) for d in cen]
    y_ref[...] = (jnp.concatenate(nrm, axis=1) * ga_ref[...].astype(f32)
                  + gb_ref[...].astype(f32)).astype(y_ref.dtype)

    @pl.when(jnp.logical_or(c == npc - 1, c >= npc))
    def _():
        sout_ref[0] = S_ref[...]


RWKV_HEADS_PER_STEP = 32


def _rwkv_core(ops, proj, v_block, gl, state, ga, gb, npc, nchunks):
    T, D = ga.shape
    N = RWKV_HS
    nh = D // N
    nb = state.shape[0]
    G = RWKV_HEADS_PER_STEP
    W = G * N
    tok = pl.BlockSpec((CHUNK, W), lambda p, c: (c, p))
    vtok = pl.BlockSpec((CHUNK, W), lambda p, c: (c, v_block * (D // W) + p))

    def sin_map(p, c):
        return (jnp.maximum(c - npc, 0), p, 0, 0)

    def sout_map(p, c):
        return (jnp.maximum(c - npc + 1, 0), p, 0, 0)

    return pl.pallas_call(
        functools.partial(_rwkv_core_kernel, npc=npc, nhead=G),
        out_shape=(jax.ShapeDtypeStruct((T, D), bf16),
                   jax.ShapeDtypeStruct((nb + 1, nh, N, N), f32)),
        grid=(nh // G, nchunks),
        in_specs=[tok] * 6 + [vtok, pl.BlockSpec((1, 1, W), lambda p, c: (c, 0, p)),
                              pl.BlockSpec((1, G, N, N), sin_map), tok, tok],
        out_specs=(tok, pl.BlockSpec((1, G, N, N), sout_map)),
        scratch_shapes=[pltpu.VMEM((G, N, N), f32)],
        compiler_params=_params(("parallel", "arbitrary")),
        name="rwkv_core",
    )(*ops, proj, gl, state, ga, gb)


def _positions(n_prompt, n_b, n_s):
    return jnp.concatenate([jnp.arange(n_prompt), jnp.tile(PAST_LEN + jnp.arange(n_s), n_b)]).astype(f32)


def _attn_tables(pos):
    half = ROT_DIM // 2
    inv = ROPE_THETA ** (-jnp.arange(half, dtype=f32) / half)
    ang = pos[:, None] * inv[None, :]
    cos, sin = jnp.cos(ang), jnp.sin(ang)
    T = pos.shape[0]
    zeros = jnp.zeros((T, half), f32)
    rest = HEAD_DIM - ROT_DIM
    c = jnp.concatenate([cos, cos, jnp.ones((T, rest), f32)], axis=1)
    s1 = jnp.concatenate([zeros, sin, jnp.zeros((T, rest), f32)], axis=1)
    s2 = jnp.concatenate([-sin, zeros, jnp.zeros((T, rest), f32)], axis=1)
    rep = 256 // HEAD_DIM
    return tuple(jnp.tile(t, (1, rep)) for t in (c, s1, s2))


def _ret_tables(pos):
    half = RET_DK // 2
    inv = RET_THETA ** (-jnp.arange(half, dtype=f32) / half)
    ang = pos[:, None] * inv[None, :]
    return jnp.cos(ang), jnp.sin(ang)


def _pad_cols(w, n):
    return jnp.pad(w, ((0, 0), (0, n - w.shape[1])))


def _pad_rows(w, n):
    return jnp.pad(w, ((0, n - w.shape[0]), (0, 0)))


def kernel(x_prompt, x_sample, cache_attn_k, cache_attn_v, state_ret, state_rwkv, state_rwkv_shift,
           norm_mix, norm_mlp, norm_final,
           attn_w_qkv, attn_sinks, attn_w_o,
           ret_w_in, ret_gn_w, ret_w_o,
           rwkv_mu, rwkv_w_rkv, rwkv_w_o, rwkv_w0, rwkv_w1, rwkv_w2, rwkv_a0, rwkv_a1, rwkv_a2,
           rwkv_g1, rwkv_g2, rwkv_k_k, rwkv_k_a, rwkv_r_k, rwkv_ln_w, rwkv_ln_b,
           mlp_w_up, mlp_w_down):
    bp, tp, D = x_prompt.shape
    nb, ns, _ = x_sample.shape
    assert bp == 1 and ns == CHUNK and tp % CHUNK == 0
    depth = norm_mix.shape[0]
    npc = tp // CHUNK
    nchunks = npc + nb
    T = tp + nb * ns
    qd = ATTN_HEADS * HEAD_DIM
    kvd = ATTN_KV_HEADS * HEAD_DIM

    x = (x_prompt.reshape(tp, D), x_sample.reshape(nb * ns, D))
    pos = _positions(tp, nb, ns)
    attn_tabs = _attn_tables(pos)
    ret_tabs = _ret_tables(pos)
    lg = jnp.log1p(-jnp.exp2(-5.0 - jnp.arange(RET_HEADS, dtype=f32)))
    tm_proj = _pick(T, (512, 256, 128, 64))
    tm_ret = _pick(T, (1024, 512, 256, 128, 64))
    w_up16 = mlp_w_up.astype(bf16)
    w_down16 = mlp_w_down.astype(bf16)

    kp_l, vp_l, ks_l, vs_l, rp_l, rs_l, wp_l, ws_l, shp_l, shs_l = ([] for _ in range(10))
    for i in range(depth):
        j, kind = divmod(i, 3)
        g_mix = norm_mix[i][None, :]
        if kind == 0:
            nsub = (qd + 2 * kvd) // (2 * EPI_COLS)
            routes = [[("q", 0, s * EPI_COLS) for s in range(nsub)],
                      [("q", 0, s * EPI_COLS) for s in range(nsub, qd // EPI_COLS)]
                      + [("k", 1, 0), ("plain", 1, EPI_COLS)]]
            q, kv = _norm_matmul(
                x, g_mix, attn_w_qkv[j].astype(bf16), attn_tabs, routes, _epi_attn,
                [jax.ShapeDtypeStruct((T, qd), bf16), jax.ShapeDtypeStruct((T, 2 * kvd), f32)],
                [pl.BlockSpec((tm_proj, qd), lambda i, jj: (i, 0)),
                 pl.BlockSpec((tm_proj, 2 * kvd), lambda i, jj: (i, 0))],
                tm_proj, "attn_qkv")
            k_new = kv[:, :kvd]
            v_new = kv[:, kvd:]

            def ext(new, cache):
                samp = jnp.concatenate([cache.reshape(nb, WINDOW, kvd), new[tp:].reshape(nb, ns, kvd)], axis=1)
                rows = jnp.concatenate([jnp.zeros((WINDOW, kvd), f32), new[:tp],
                                        samp.reshape(nb * (WINDOW + ns), kvd)], axis=0)
                r4 = rows.astype(bf16).reshape(rows.shape[0], ATTN_KV_HEADS, 1, HEAD_DIM)
                dup = jnp.broadcast_to(r4, (rows.shape[0], ATTN_KV_HEADS, 2, HEAD_DIM))
                return dup.reshape(rows.shape[0], 2 * kvd), samp

            kext, k_samp = ext(k_new, cache_attn_k[j])
            vext, v_samp = ext(v_new, cache_attn_v[j])
            o = _attention(q, kext, vext, attn_sinks[j], npc, nchunks)
            x = _matmul_residual(o, attn_w_o[j].astype(bf16), x, "attn_out")
            kp_l.append(k_new[tp - WINDOW:tp].reshape(1, WINDOW, ATTN_KV_HEADS, HEAD_DIM))
            vp_l.append(v_new[tp - WINDOW:tp].reshape(1, WINDOW, ATTN_KV_HEADS, HEAD_DIM))
            ks_l.append(k_samp[:, -WINDOW:].reshape(nb, WINDOW, ATTN_KV_HEADS, HEAD_DIM))
            vs_l.append(v_samp[:, -WINDOW:].reshape(nb, WINDOW, ATTN_KV_HEADS, HEAD_DIM))
        elif kind == 1:
            nq = RET_HEADS * RET_DK // 1024
            nv = RET_HEADS * RET_DV // 1024
            tile = lambda mode: [(mode, 0, s * EPI_COLS) for s in range(1024 // EPI_COLS)]
            routes = [tile("q")] * nq + [tile("k")] * nq + [tile("plain")] * (2 * nv)
            (proj,) = _norm_matmul(
                x, g_mix, ret_w_in[j].astype(bf16), ret_tabs, routes, _epi_ret,
                [jax.ShapeDtypeStruct((T, ret_w_in.shape[2]), bf16)],
                [pl.BlockSpec((tm_ret, 1024), lambda i, jj: (i, jj))], tm_ret, "ret_proj")
            gnw = ret_gn_w[j][None, :]
            lp = _pick(tp, (256, 128, 64))
            y_p, s_p = _retention(proj, state_ret[j], gnw, lg, 0, tp, lp, False, "ret_prompt")
            y_s, s_s = _retention(proj, state_ret[j], gnw, lg, tp, nb * ns, ns, True, "ret_sample")
            x = _matmul_residual2(y_p, y_s, ret_w_o[j].astype(bf16), x, "ret_out")
            rp_l.append(s_p)
            rs_l.append(s_s)
        else:
            starts = jnp.concatenate([jnp.zeros((npc, D), f32), state_rwkv_shift[j]], axis=0)[:, None, :]
            P = LORA_PAD
            tn = 512
            wcat = jnp.concatenate(
                [rwkv_w_rkv[j][0], rwkv_w_rkv[j][1], rwkv_w_rkv[j][2],
                 _pad_cols(rwkv_w1[j], P), _pad_cols(rwkv_a1[j], P), _pad_cols(rwkv_g1[j], P)],
                axis=1).astype(bf16)
            tiles_per_d = D // tn
            mu = rwkv_mu[j]
            mu_tiles = jnp.concatenate(
                [jnp.repeat(mu[jnp.array([0, 2, 3])], tiles_per_d, axis=0), mu[jnp.array([1, 4, 5])]],
                axis=0)[:, None, :]
            proj, h_last = _lerp_matmul(x, g_mix, starts, mu_tiles, wcat, tn, tiles_per_d,
                                        3 * tiles_per_d, npc)
            prep = _rwkv_prep(
                proj, 3 * D // (3 * P),
                _pad_rows(rwkv_w2[j], P).astype(bf16), _pad_rows(rwkv_a2[j], P).astype(bf16),
                _pad_rows(rwkv_g2[j], P).astype(bf16), rwkv_w0[j][None, :], rwkv_a0[j][None, :],
                rwkv_k_k[j][None, :], rwkv_k_a[j][None, :], rwkv_r_k[j].reshape(1, D),
                rwkv_ln_w[j][None, :], rwkv_ln_b[j][None, :])
            ops, gl, ga, gb = prep[:6], prep[6], prep[7], prep[8]
            y, s_all = _rwkv_core(ops, proj, 2, gl, state_rwkv[j], ga, gb, npc, nchunks)
            x = _matmul_residual(y, rwkv_w_o[j].astype(bf16), x, "rwkv_out")
            wp_l.append(s_all[:1])
            ws_l.append(s_all[1:])
            shp_l.append(h_last[npc - 1])
            shs_l.append(h_last[npc:, 0])
        x = _mlp(x, norm_mlp[i][None, :], w_up16, w_down16, i, norm_final[None, :],
                 tp if i == depth - 1 else None, "mlp")

    y_prompt = x[0].reshape(1, tp, D)
    y_sample = x[1].reshape(nb, ns, D)
    return (y_prompt, y_sample,
            jnp.stack(kp_l), jnp.stack(vp_l), jnp.stack(ks_l), jnp.stack(vs_l),
            jnp.stack(rp_l), jnp.stack(rs_l),
            jnp.stack(wp_l), jnp.stack(ws_l), jnp.stack(shp_l), jnp.stack(shs_l))
```

```python
import functools

import jax
import jax.numpy as jnp
from jax import lax
from jax.experimental import pallas as pl
from jax.experimental.pallas import tpu as pltpu

f32 = jnp.float32
bf16 = jnp.bfloat16

CHUNK = 64
NORM_EPS = 1e-5
PAST_LEN = 4096

ATTN_HEADS = 32
ATTN_KV_HEADS = 4
ATTN_GROUP = ATTN_HEADS // ATTN_KV_HEADS
HEAD_DIM = 64
WINDOW = 128
ROT_DIM = HEAD_DIM // 4
ROPE_THETA = 500000.0

RET_HEADS = 8
RET_DK = 256
RET_DV = 512
RET_THETA = 10000.0
RET_GN_EPS = 1e-5

RWKV_HS = 64
RWKV_GN_EPS = 64e-5
LORA_PAD = 512

VMEM_LIMIT = 52 * 1024 * 1024

NT_DIMS = (((1,), (1,)), ((), ()))


def _pick(n, cands):
    for c in cands:
        if n % c == 0:
            return c
    raise ValueError(f"no tile for {n} in {cands}")


def _params(sem):
    return pltpu.CompilerParams(dimension_semantics=sem, vmem_limit_bytes=VMEM_LIMIT)


def _rms(x, g):
    return x * lax.rsqrt(jnp.mean(x * x, axis=-1, keepdims=True) + NORM_EPS) * g


def _dup_heads(x):
    pw = 2 * HEAD_DIM
    lo = lax.broadcasted_iota(jnp.int32, (1, pw), 1) < HEAD_DIM
    out = []
    for c in range(x.shape[1] // pw):
        xc = x[:, c * pw:(c + 1) * pw]
        sw = pltpu.roll(xc, HEAD_DIM, 1)
        out += [jnp.where(lo, xc, sw), jnp.where(lo, sw, xc)]
    return jnp.concatenate(out, axis=1)


def _epi_attn(blk, mode, tabs):
    if mode == "v":
        return blk, _dup_heads(blk)
    c_ref, s1_ref, s2_ref = tabs
    out = (blk * c_ref[...] + pltpu.roll(blk, 8, 1) * s1_ref[...]
           + pltpu.roll(blk, 256 - 8, 1) * s2_ref[...])
    if mode == "q":
        return (out * (HEAD_DIM ** -0.5),)
    return out, _dup_heads(out)


def _epi_ret(blk, mode, tabs):
    if mode == "plain":
        return (blk,)
    cos_ref, sin_ref = tabs
    c, s = cos_ref[...], sin_ref[...]
    x1, x2 = blk[:, :128], blk[:, 128:]
    out = jnp.concatenate([x1 * c - x2 * s, x2 * c + x1 * s], axis=1)
    return (out * (RET_DK ** -0.5) if mode == "k" else out,)


EPI_COLS = 256


def _row_split_specs(tm, cols, split_at):
    first = pl.BlockSpec((tm, cols), lambda i, *_: (jnp.minimum(i, split_at - 1), 0))
    second = pl.BlockSpec((tm, cols), lambda i, *_: (jnp.maximum(i - split_at, 0), 0))
    return [first, second]


def _normmm_kernel(*refs, routes, ntab, nout, epi, split_at):
    nx = 1 if split_at is None else 2
    x_refs = refs[:nx]
    g_ref, w_ref = refs[nx:nx + 2]
    tabs = refs[nx + 2:nx + 2 + ntab]
    outs = refs[nx + 2 + ntab:nx + 2 + ntab + nout]
    xn_ref = refs[nx + 2 + ntab + nout]
    i = pl.program_id(0)
    j = pl.program_id(1)

    if split_at is None:
        @pl.when(j == 0)
        def _():
            xn_ref[...] = _rms(x_refs[0][...], g_ref[...]).astype(bf16)
    else:
        @pl.when(jnp.logical_and(j == 0, i < split_at))
        def _():
            xn_ref[...] = _rms(x_refs[0][...], g_ref[...]).astype(bf16)

        @pl.when(jnp.logical_and(j == 0, i >= split_at))
        def _():
            xn_ref[...] = _rms(x_refs[1][...], g_ref[...]).astype(bf16)

    groups = {}
    for jj, tile_routes in enumerate(routes):
        groups.setdefault(tuple(tile_routes), []).append(jj)
    for tile_routes, jjs in groups.items():
        cond = functools.reduce(jnp.logical_or, [j == jj for jj in jjs])

        @pl.when(cond)
        def _(tile_routes=tile_routes):
            for s, (mode, dests) in enumerate(tile_routes):
                acc = jnp.dot(xn_ref[...], w_ref[:, s * EPI_COLS:(s + 1) * EPI_COLS],
                              preferred_element_type=f32)
                for (oi, col), val in zip(dests, epi(acc, mode, tabs)):
                    outs[oi][:, col:col + val.shape[1]] = val.astype(outs[oi].dtype)


def _norm_matmul(x, g, w, tabs, routes, epi, out_shapes, out_specs, tm, name):
    xs = x if isinstance(x, tuple) else (x,)
    T = sum(p.shape[0] for p in xs)
    D = xs[0].shape[1]
    N = w.shape[1]
    ntile = len(routes)
    tn = N // ntile
    assert tn * ntile == N and tn == EPI_COLS * len(routes[0])
    if len(xs) == 1:
        split_at = None
        x_specs = [pl.BlockSpec((tm, D), lambda i, j: (i, 0))]
    else:
        assert xs[0].shape[0] % tm == 0 and xs[1].shape[0] % tm == 0
        split_at = xs[0].shape[0] // tm
        x_specs = _row_split_specs(tm, D, split_at)
    tab_specs = [pl.BlockSpec((tm, t.shape[1]), lambda i, j: (i, 0)) for t in tabs]
    return pl.pallas_call(
        functools.partial(_normmm_kernel, routes=routes, ntab=len(tabs), nout=len(out_shapes), epi=epi,
                          split_at=split_at),
        out_shape=tuple(out_shapes),
        grid=(T // tm, ntile),
        in_specs=x_specs + [pl.BlockSpec((1, D), lambda i, j: (0, 0)),
                            pl.BlockSpec((D, tn), lambda i, j: (0, j))] + tab_specs,
        out_specs=tuple(out_specs),
        scratch_shapes=[pltpu.VMEM((tm, D), bf16)],
        compiler_params=_params(("parallel", "arbitrary")),
        name=name,
    )(*xs, g, w, *tabs)


def _mmres_kernel(a_ref, w_ref, *rest, split_at):
    o_ref = rest[-1]
    prod = jnp.dot(a_ref[...], w_ref[...], preferred_element_type=f32)
    if split_at is None:
        o_ref[...] = rest[0][...] + prod
    else:
        i = pl.program_id(0)

        @pl.when(i < split_at)
        def _():
            o_ref[...] = rest[0][...] + prod

        @pl.when(i >= split_at)
        def _():
            o_ref[...] = rest[1][...] + prod


def _resident(shape):
    return pl.BlockSpec(shape, lambda *_: (0,) * len(shape), pipeline_mode=pl.Buffered(1))


def _matmul_residual(a, w, res, name):
    T, K = a.shape
    N = w.shape[1]
    rs = res if isinstance(res, tuple) else (res,)
    if len(rs) == 1:
        tm = _pick(T, (512, 256, 128, 64))
        split_at = None
        r_specs = [pl.BlockSpec((tm, N), lambda i: (i, 0))]
    else:
        tm = _pick(rs[1].shape[0], (512, 256, 128, 64))
        assert rs[0].shape[0] % tm == 0
        split_at = rs[0].shape[0] // tm
        r_specs = _row_split_specs(tm, N, split_at)
    return pl.pallas_call(
        functools.partial(_mmres_kernel, split_at=split_at),
        out_shape=jax.ShapeDtypeStruct((T, N), f32),
        grid=(T // tm,),
        in_specs=[pl.BlockSpec((tm, K), lambda i: (i, 0)), _resident((K, N))] + r_specs,
        out_specs=pl.BlockSpec((tm, N), lambda i: (i, 0)),
        compiler_params=_params(("parallel",)),
        name=name,
    )(a, w, *rs)


def _mmres2_kernel(ap_ref, as_ref, w_ref, r_ref, o_ref, *, np_blocks):
    i = pl.program_id(0)

    @pl.when(i < np_blocks)
    def _():
        o_ref[...] = r_ref[...] + jnp.dot(ap_ref[...], w_ref[...], preferred_element_type=f32)

    @pl.when(i >= np_blocks)
    def _():
        o_ref[...] = r_ref[...] + jnp.dot(as_ref[...], w_ref[...], preferred_element_type=f32)


def _matmul_residual2(a_p, a_s, w, res, name):
    tp, K = a_p.shape
    ts = a_s.shape[0]
    N = w.shape[1]
    tm = _pick(ts, (256, 128, 64))
    assert tp % tm == 0
    np_blocks = tp // tm
    return pl.pallas_call(
        functools.partial(_mmres2_kernel, np_blocks=np_blocks),
        out_shape=jax.ShapeDtypeStruct((tp + ts, N), f32),
        grid=((tp + ts) // tm,),
        in_specs=[pl.BlockSpec((tm, K), lambda i: (jnp.minimum(i, np_blocks - 1), 0)),
                  pl.BlockSpec((tm, K), lambda i: (jnp.maximum(i - np_blocks, 0), 0)),
                  _resident((K, N)),
                  pl.BlockSpec((tm, N), lambda i: (i, 0))],
        out_specs=pl.BlockSpec((tm, N), lambda i: (i, 0)),
        compiler_params=_params(("parallel",)),
        name=name,
    )(a_p, a_s, w, res)


def _mlp_kernel(x_ref, g_ref, wu_ref, wd_ref, gf_ref, *rest, nk, split_at):
    if split_at is None:
        acc_ref, xn_ref = rest
    else:
        op_ref, os_ref, xn_ref, acc_ref = rest
    i = pl.program_id(0)
    k = pl.program_id(1)

    @pl.when(k == 0)
    def _():
        x = x_ref[...]
        xn_ref[...] = _rms(x, g_ref[...]).astype(bf16)
        acc_ref[...] = x

    h = jnp.dot(xn_ref[...], wu_ref[...], preferred_element_type=f32)
    h = jnp.square(jnp.maximum(h, 0.0)).astype(bf16)
    acc_ref[...] += jnp.dot(h, wd_ref[...], preferred_element_type=f32)

    if split_at is not None:
        @pl.when(jnp.logical_and(k == nk - 1, i < split_at))
        def _():
            op_ref[...] = _rms(acc_ref[...], gf_ref[...])

        @pl.when(jnp.logical_and(k == nk - 1, i >= split_at))
        def _():
            os_ref[...] = _rms(acc_ref[...], gf_ref[...])


def _mlp(x, g, w_up, w_down, layer, g_final, split_rows, name):
    T, D = x.shape
    F = w_up.shape[2]
    tf = _pick(F, (1024, 512))
    nk = F // tf
    if split_rows is None:
        tm = _pick(T, (512, 256, 128, 64))
        x_spec = pl.BlockSpec((tm, D), lambda i, k: (i, 0))
        split_at = None
        out_shape = jax.ShapeDtypeStruct((T, D), f32)
        out_specs = pl.BlockSpec((tm, D), lambda i, k: (i, 0))
        scratch = [pltpu.VMEM((tm, D), bf16)]
    else:
        tm = _pick(T - split_rows, (512, 256, 128, 64))
        x_spec = pl.BlockSpec((tm, D), lambda i, k: (i, 0))
        assert split_rows % tm == 0
        split_at = split_rows // tm
        out_shape = (jax.ShapeDtypeStruct((split_rows, D), f32), jax.ShapeDtypeStruct((T - split_rows, D), f32))
        out_specs = (pl.BlockSpec((tm, D), lambda i, k: (jnp.minimum(i, split_at - 1), 0)),
                     pl.BlockSpec((tm, D), lambda i, k: (jnp.maximum(i - split_at, 0), 0)))
        scratch = [pltpu.VMEM((tm, D), bf16), pltpu.VMEM((tm, D), f32)]
    return pl.pallas_call(
        functools.partial(_mlp_kernel, nk=nk, split_at=split_at),
        out_shape=out_shape,
        grid=(T // tm, nk),
        in_specs=[x_spec,
                  pl.BlockSpec((1, D), lambda i, k: (0, 0)),
                  pl.BlockSpec((None, D, tf), lambda i, k: (layer, 0, k)),
                  pl.BlockSpec((None, tf, D), lambda i, k: (layer, k, 0)),
                  pl.BlockSpec((1, D), lambda i, k: (0, 0))],
        out_specs=out_specs,
        scratch_shapes=scratch,
        compiler_params=_params(("arbitrary", "arbitrary")),
        name=name,
    )(x, g, w_up, w_down, g_final)


def _attn_kernel(sb_ref, q_ref, k0_ref, k1_ref, k2_ref, v0_ref, v1_ref, v2_ref, o_ref, *, npc):
    c = pl.program_id(0)
    nkeys = 3 * CHUNK
    ncols = nkeys + CHUNK
    PW = 2 * HEAD_DIM
    pairs = ATTN_GROUP // 2
    col_blk = lax.broadcasted_iota(jnp.int32, (1, ncols), 1) // CHUNK
    valid = jnp.logical_or(jnp.logical_or(c >= npc, c + col_blk >= 2), col_blk >= 3)
    bias = jnp.where(valid, 0.0, -jnp.inf).astype(f32)
    zpad = jnp.zeros((CHUNK, ATTN_KV_HEADS * PW), bf16)
    kcat = jnp.concatenate([k0_ref[...], k1_ref[...], k2_ref[...], zpad], axis=0)
    vcat = jnp.concatenate([v0_ref[...], v1_ref[...], v2_ref[...], zpad], axis=0)
    lo_half = lax.broadcasted_iota(jnp.int32, (1, PW), 1) < HEAD_DIM
    ones = jnp.ones((ncols, PW), bf16)
    kvs = range(ATTN_KV_HEADS)

    def stacked_q(kv):
        parts = []
        for p in range(pairs):
            col = (kv * pairs + p) * PW
            qp = q_ref[:, col:col + PW]
            parts += [jnp.where(lo_half, qp, 0.0), jnp.where(lo_half, 0.0, qp)]
        return jnp.concatenate(parts, axis=0).astype(bf16)

    s = [lax.dot_general(stacked_q(kv), kcat[:, kv * PW:(kv + 1) * PW], NT_DIMS,
                         preferred_element_type=f32) + sb_ref[kv] + bias for kv in kvs]
    p = [jnp.exp(s[kv] - jnp.max(s[kv], axis=-1, keepdims=True)).astype(bf16) for kv in kvs]
    oa = [jnp.dot(p[kv], jnp.concatenate([vcat[:, kv * PW:(kv + 1) * PW], ones], axis=1),
                  preferred_element_type=f32) for kv in kvs]
    for kv in kvs:
        on = oa[kv][:, :PW] / oa[kv][:, PW:]
        for pi in range(pairs):
            r0 = 2 * pi * CHUNK
            blk = jnp.where(lo_half, on[r0:r0 + CHUNK], on[r0 + CHUNK:r0 + 2 * CHUNK])
            col = (kv * pairs + pi) * PW
            o_ref[:, col:col + PW] = blk.astype(o_ref.dtype)


def _attention(qkv, kext, vext, sinks, npc, nchunks):
    T = qkv.shape[0]
    qd = ATTN_HEADS * HEAD_DIM
    kvd = kext.shape[1]

    def kv_spec(j):
        def imap(c):
            return (jnp.where(c < npc, c, npc + 2 + 3 * (c - npc)) + j, 0)
        return pl.BlockSpec((CHUNK, kvd), imap)

    nkeys = 3 * CHUNK
    col = jnp.arange(nkeys + CHUNK)[None, None, :]
    sink_rows = jnp.repeat(sinks.astype(f32).reshape(ATTN_KV_HEADS, ATTN_GROUP), CHUNK, axis=1)[:, :, None]
    sink_bias = jnp.where(col < nkeys, 0.0, jnp.where(col == nkeys, sink_rows, -jnp.inf)).astype(f32)

    return pl.pallas_call(
        functools.partial(_attn_kernel, npc=npc),
        out_shape=jax.ShapeDtypeStruct((T, qd), bf16),
        grid=(nchunks,),
        in_specs=[pl.BlockSpec(sink_bias.shape, lambda c: (0, 0, 0)),
                  pl.BlockSpec((CHUNK, qd), lambda c: (c, 0))]
                 + [kv_spec(j) for j in range(3)] + [kv_spec(j) for j in range(3)],
        out_specs=pl.BlockSpec((CHUNK, qd), lambda c: (c, 0)),
        compiler_params=_params(("parallel",)),
        name="attn_core",
    )(sink_bias, qkv, kext, kext, kext, vext, vext, vext)


RET_HEADS_PER_STEP = 4


def _ret_kernel(lg_ref, q_ref, k_ref, v_ref, g_ref, s0_ref, gnw_ref, y_ref, sout_ref, S_ref, *, from_state):
    hg = pl.program_id(0)
    c = pl.program_id(1)
    L = q_ref.shape[0]
    nh = RET_HEADS_PER_STEP
    hs = range(nh)
    dot = functools.partial(jnp.dot, preferred_element_type=f32)

    if from_state:
        S_ref[...] = s0_ref[0]
    else:
        @pl.when(c == 0)
        def _():
            S_ref[...] = jnp.zeros_like(S_ref)

    lg = [lg_ref[hg * nh + h] for h in hs]
    q = [q_ref[:, h * RET_DK:(h + 1) * RET_DK] for h in hs]
    k = [k_ref[:, h * RET_DK:(h + 1) * RET_DK] for h in hs]
    v = [v_ref[:, h * RET_DV:(h + 1) * RET_DV] for h in hs]
    row = lax.broadcasted_iota(jnp.int32, (L, L), 0)
    col = lax.broadcasted_iota(jnp.int32, (L, L), 1)
    diff = (row - col).astype(f32)
    idx = lax.broadcasted_iota(jnp.int32, (L, 1), 0).astype(f32)
    decay = [jnp.where(diff >= 0, jnp.exp(lg[h] * jnp.maximum(diff, 0.0)), 0.0) for h in hs]
    xi = [jnp.exp(lg[h] * (idx + 1.0)) for h in hs]
    zeta = [jnp.exp(lg[h] * (L - 1.0 - idx)) for h in hs]

    S = [S_ref[h] for h in hs]
    scores = [lax.dot_general(q[h], k[h], NT_DIMS, preferred_element_type=f32) * decay[h] for h in hs]
    o = [dot(scores[h].astype(bf16), v[h])
         + dot((q[h].astype(f32) * xi[h]).astype(bf16), S[h].astype(bf16)) for h in hs]
    S_new = [jnp.exp(lg[h] * L) * S[h] + dot((k[h].astype(f32) * zeta[h]).T.astype(bf16), v[h])
             for h in hs]
    for h in hs:
        S_ref[h] = S_new[h]
        sout_ref[0, h] = S_new[h]

    cen = [o[h] - jnp.mean(o[h], axis=-1, keepdims=True) for h in hs]
    on = [cen[h] * lax.rsqrt(jnp.mean(cen[h] * cen[h], axis=-1, keepdims=True) + RET_GN_EPS) for h in hs]
    for h in hs:
        sl = slice(h * RET_DV, (h + 1) * RET_DV)
        y_ref[:, sl] = (jax.nn.silu(g_ref[:, sl].astype(f32)) * on[h] * gnw_ref[:, sl]).astype(y_ref.dtype)


def _retention(proj, state, gn_w, lg, row0, nrows, L, from_state, name):
    nsteps = nrows // L
    b0 = row0 // L
    assert nsteps * L == nrows and b0 * L == row0
    nseq = nsteps if from_state else 1
    G = RET_HEADS_PER_STEP
    kw, vw = G * RET_DK, G * RET_DV
    kb = RET_HEADS * RET_DK // kw
    vb = 2 * RET_HEADS * RET_DK // vw
    gb = vb + RET_HEADS * RET_DV // vw
    state_map = (lambda h, c: (c, h, 0, 0)) if from_state else (lambda h, c: (0, h, 0, 0))

    return pl.pallas_call(
        functools.partial(_ret_kernel, from_state=from_state),
        out_shape=(jax.ShapeDtypeStruct((nrows, RET_HEADS * RET_DV), bf16),
                   jax.ShapeDtypeStruct((nseq, RET_HEADS, RET_DK, RET_DV), f32)),
        grid=(RET_HEADS // G, nsteps),
        in_specs=[pl.BlockSpec(memory_space=pltpu.SMEM),
                  pl.BlockSpec((L, kw), lambda h, c: (b0 + c, h)),
                  pl.BlockSpec((L, kw), lambda h, c: (b0 + c, kb + h)),
                  pl.BlockSpec((L, vw), lambda h, c: (b0 + c, vb + h)),
                  pl.BlockSpec((L, vw), lambda h, c: (b0 + c, gb + h)),
                  pl.BlockSpec((1, G, RET_DK, RET_DV), state_map),
                  pl.BlockSpec((1, vw), lambda h, c: (0, h))],
        out_specs=(pl.BlockSpec((L, vw), lambda h, c: (c, h)),
                   pl.BlockSpec((1, G, RET_DK, RET_DV), state_map)),
        scratch_shapes=[pltpu.VMEM((G, RET_DK, RET_DV), f32)],
        compiler_params=_params(("parallel", "arbitrary")),
        name=name,
    )(lg, proj, proj, proj, proj, state, gn_w)


SUBLANES = 8


def _lerpmm_kernel(x_ref, xprev_ref, g_ref, start_ref, mu_ref, w_ref, o_ref, hlast_ref,
                   h_ref, xx_ref, l_ref, *, tiles_per_d, n_big, npc):
    i = pl.program_id(0)
    j = pl.program_id(1)
    tm = x_ref.shape[0]
    cpb = tm // CHUNK

    @pl.when(j == 0)
    def _():
        g = g_ref[...]
        h = _rms(x_ref[...], g)
        h_ref[...] = h
        xx_ref[...] = pltpu.roll(h, 1, 0) - h
        prev = _rms(xprev_ref[...], g)[SUBLANES - 1:SUBLANES, :]
        xx_ref[0:1, :] = prev - h[0:1, :]
        for ci in range(cpb):
            gc = i * cpb + ci
            r0 = ci * CHUNK

            @pl.when(jnp.logical_or(gc == 0, gc >= npc))
            def _(ci=ci, r0=r0):
                xx_ref[r0:r0 + 1, :] = start_ref[ci] - h_ref[r0:r0 + 1, :]

            hlast_ref[ci] = h[r0 + CHUNK - 1:r0 + CHUNK, :]

    @pl.when(jnp.logical_or(j % tiles_per_d == 0, j >= n_big))
    def _():
        l_ref[...] = (h_ref[...] + xx_ref[...] * mu_ref[0]).astype(bf16)

    o_ref[...] = jnp.dot(l_ref[...], w_ref[...], preferred_element_type=f32).astype(o_ref.dtype)


def _lerp_matmul(x, g, starts, mu_tiles, wcat, tn, tiles_per_d, n_big, npc):
    T, D = x.shape
    N = wcat.shape[1]
    tm = _pick(T, (1024, 512, 256, 128, 64))
    cpb = tm // CHUNK
    return pl.pallas_call(
        functools.partial(_lerpmm_kernel, tiles_per_d=tiles_per_d, n_big=n_big, npc=npc),
        out_shape=(jax.ShapeDtypeStruct((T, N), bf16),
                   jax.ShapeDtypeStruct((T // CHUNK, 1, D), f32)),
        grid=(T // tm, N // tn),
        in_specs=[pl.BlockSpec((tm, D), lambda i, j: (i, 0)),
                  pl.BlockSpec((SUBLANES, D), lambda i, j: (jnp.maximum(i * (tm // SUBLANES) - 1, 0), 0)),
                  pl.BlockSpec((1, D), lambda i, j: (0, 0)),
                  pl.BlockSpec((cpb, 1, D), lambda i, j: (i, 0, 0)),
                  pl.BlockSpec((1, 1, D), lambda i, j: (j, 0, 0)),
                  pl.BlockSpec((D, tn), lambda i, j: (0, j))],
        out_specs=(pl.BlockSpec((tm, tn), lambda i, j: (i, j)),
                   pl.BlockSpec((cpb, 1, D), lambda i, j: (i, 0, 0))),
        scratch_shapes=[pltpu.VMEM((tm, D), f32), pltpu.VMEM((tm, D), f32), pltpu.VMEM((tm, D), bf16)],
        compiler_params=_params(("parallel", "arbitrary")),
        name="rwkv_proj",
    )(x, x, g, starts, mu_tiles, wcat)


DECAY_SCALE = 0.6065306597126334
LANES = 128


SEG_TILE = 256


def _head_sum(x):
    rows, D = x.shape
    r = lax.broadcasted_iota(jnp.int32, (SEG_TILE, SEG_TILE), 0) // RWKV_HS
    c = lax.broadcasted_iota(jnp.int32, (SEG_TILE, SEG_TILE), 1) // RWKV_HS
    ones = (r == c).astype(bf16)
    out = [jnp.dot(x[:, j * SEG_TILE:(j + 1) * SEG_TILE].astype(bf16), ones, preferred_element_type=f32)
           for j in range(D // SEG_TILE)]
    return jnp.concatenate(out, axis=1)


def _chunk_cumsum(x):
    rows = x.shape[0]
    row = lax.broadcasted_iota(jnp.int32, (rows, rows), 0)
    col = lax.broadcasted_iota(jnp.int32, (rows, rows), 1)
    tri = jnp.logical_and((row // CHUNK) == (col // CHUNK), row >= col).astype(bf16)
    hi = x.astype(bf16)
    rest = x - hi.astype(f32)
    mid = rest.astype(bf16)
    lo = (rest - mid.astype(f32)).astype(bf16)
    dot = functools.partial(jnp.dot, preferred_element_type=f32)
    return dot(tri, hi) + dot(tri, mid) + dot(tri, lo)


def _rwkv_prep_kernel(p_ref, r_ref, k_ref, v_ref, w2_ref, a2_ref, g2_ref, w0_ref, a0_ref,
                      kk_ref, ka_ref, rk_ref, lnw_ref, lnb_ref,
                      at_ref, rt_ref, bt_ref, kt_ref, bh_ref, kh_ref, gl_ref, ga_ref, gb_ref):
    P = LORA_PAD
    tm = r_ref.shape[0]
    rw, ra, rg = w2_ref.shape[0], a2_ref.shape[0], g2_ref.shape[0]
    pw = jnp.tanh(p_ref[:, :rw].astype(f32)).astype(bf16)
    pa = p_ref[:, P:P + ra]
    pg = jax.nn.sigmoid(p_ref[:, 2 * P:2 * P + rg].astype(f32)).astype(bf16)
    wl = w0_ref[...] + jnp.dot(pw, w2_ref[...], preferred_element_type=f32)
    lw = -DECAY_SCALE * jax.nn.sigmoid(wl)
    a = jax.nn.sigmoid(a0_ref[...] + jnp.dot(pa, a2_ref[...], preferred_element_type=f32))
    g = jnp.dot(pg, g2_ref[...], preferred_element_type=f32)

    cum = _chunk_cumsum(lw)
    tot = jnp.concatenate(
        [jnp.broadcast_to(cum[(ci + 1) * CHUNK - 1:(ci + 1) * CHUNK, :], (CHUNK, cum.shape[1]))
         for ci in range(tm // CHUNK)], axis=0)

    r, k, v = r_ref[...].astype(f32), k_ref[...].astype(f32), v_ref[...].astype(f32)
    kk = k * kk_ref[...]
    kk = kk / jnp.maximum(jnp.sqrt(_head_sum(kk * kk)), 1e-12)
    kmod = k * (1.0 + (a - 1.0) * ka_ref[...])
    beta = kk * a
    e_neg = jnp.exp(-cum)
    e_end = jnp.exp(tot - cum)
    at_ref[...] = (-kk * jnp.exp(cum - lw)).astype(bf16)
    rt_ref[...] = (r * jnp.exp(cum)).astype(bf16)
    bt_ref[...] = (beta * e_neg).astype(bf16)
    kt_ref[...] = (kmod * e_neg).astype(bf16)
    bh_ref[...] = (beta * e_end).astype(bf16)
    kh_ref[...] = (kmod * e_end).astype(bf16)
    bonus = _head_sum(r * kmod * rk_ref[...]) * v
    ga_ref[...] = (lnw_ref[...] * g).astype(bf16)
    gb_ref[...] = ((lnb_ref[...] + bonus) * g).astype(bf16)
    for ci in range(tm // CHUNK):
        gl_ref[ci] = jnp.exp(tot[ci * CHUNK:ci * CHUNK + 1, :])


def _rwkv_prep(proj, lora_block, w2p, a2p, g2p, w0, a0, k_k, k_a, r_k, ln_w, ln_b):
    T = proj.shape[0]
    D = w2p.shape[1]
    P = LORA_PAD
    tm = _pick(T, (128, 64))
    vspec = pl.BlockSpec((1, D), lambda i: (0, 0))
    ospec = pl.BlockSpec((tm, D), lambda i: (i, 0))
    o16 = jax.ShapeDtypeStruct((T, D), bf16)
    return pl.pallas_call(
        _rwkv_prep_kernel,
        out_shape=(o16,) * 6 + (jax.ShapeDtypeStruct((T // CHUNK, 1, D), f32), o16, o16),
        grid=(T // tm,),
        in_specs=[pl.BlockSpec((tm, 3 * P), lambda i: (i, lora_block)),
                  pl.BlockSpec((tm, D), lambda i: (i, 0)),
                  pl.BlockSpec((tm, D), lambda i: (i, 1)),
                  pl.BlockSpec((tm, D), lambda i: (i, 2)),
                  _resident(w2p.shape), _resident(a2p.shape), _resident(g2p.shape)] + [vspec] * 7,
        out_specs=(ospec,) * 6 + (pl.BlockSpec((tm // CHUNK, 1, D), lambda i: (i, 0, 0)), ospec, ospec),
        compiler_params=_params(("parallel",)),
        name="rwkv_prep",
    )(proj, proj, proj, proj, w2p, a2p, g2p, w0, a0, k_k, k_a, r_k, ln_w, ln_b)


def _rwkv_heads_chunk(heads):
    L, N = heads[0][6].shape
    assert L == N
    nh = len(heads)
    rng = range(nh)
    dot = functools.partial(jnp.dot, preferred_element_type=f32)
    row = lax.broadcasted_iota(jnp.int32, (L, 2 * L), 0)
    lane = lax.broadcasted_iota(jnp.int32, (L, 2 * L), 1)
    lo = lane < L
    tok = jnp.where(lo, lane, lane - L)
    incl = row >= tok
    strict = row > tok
    lhs = [jnp.concatenate([h[0], h[1]], axis=0) for h in heads]
    rhs = [jnp.concatenate([h[2], h[3]], axis=0) for h in heads]
    G = [lax.dot_general(lhs[i], rhs[i], NT_DIMS, preferred_element_type=f32) for i in rng]
    LS = [lax.dot_general(lhs[i], heads[i][7].astype(bf16), NT_DIMS, preferred_element_type=f32)
          for i in rng]
    top = [jnp.where(strict, g[:L], 0.0) for g in G]
    a_r = [jnp.where(incl, g[L:], 0.0).astype(bf16) for g in G]
    v2 = [jnp.concatenate([h[6], h[6]], axis=0) for h in heads]
    zero = jnp.zeros((L, N), f32)
    x0 = [LS[i][:L] + dot(jnp.where(lo, 0.0, top[i]).astype(bf16), v2[i]) for i in rng]
    p = [jnp.where(lo, top[i], jnp.concatenate([zero, x0[i]], axis=1)) for i in rng]
    steps = max(1, (L - 1).bit_length())
    for s in range(steps):
        z = [dot(p[i][:, :L].astype(bf16), p[i].astype(bf16)) for i in rng]
        if s < steps - 1:
            p = [z[i] + jnp.where(lo, 0.0, p[i]) for i in rng]
        else:
            u = [(p[i] + z[i])[:, L:] for i in rng]
    uv = [jnp.concatenate([u[i], heads[i][6].astype(f32)], axis=0) for i in rng]
    o = [LS[i][L:] + dot(a_r[i], uv[i].astype(bf16)) for i in rng]
    S_new = [heads[i][7] * heads[i][8]
             + dot(uv[i].T.astype(bf16), jnp.concatenate([heads[i][4], heads[i][5]], axis=0))
             for i in rng]
    return list(zip(o, S_new))


def _rwkv_core_kernel(at_ref, rt_ref, bt_ref, kt_ref, bh_ref, kh_ref, v_ref, gl_ref, s0_ref,
                      ga_ref, gb_ref, y_ref, sout_ref, S_ref, *, npc, nhead):
    c = pl.program_id(1)
    N = RWKV_HS

    @pl.when(c == 0)
    def _():
        S_ref[...] = jnp.zeros_like(S_ref)

    @pl.when(c >= npc)
    def _():
        S_ref[...] = s0_ref[0]

    gl = gl_ref[0]
    heads = []
    for hh in range(nhead):
        sl = slice(hh * N, (hh + 1) * N)
        heads.append((at_ref[:, sl], rt_ref[:, sl], bt_ref[:, sl], kt_ref[:, sl], bh_ref[:, sl],
                      kh_ref[:, sl], v_ref[:, sl], S_ref[hh], gl[:, sl]))
    res = _rwkv_heads_chunk(heads)
    for hh in range(nhead):
        S_ref[hh] = res[hh][1]
    outs = [o for o, _ in res]
    cen = [o - jnp.mean(o, axis=-1, keepdims=True) for o in outs]
    nrm = [d * lax.rsqrt(jnp.mean(d * d, axis=-1, keepdims=True) + RWKV_GN_EPS) for d in cen]
    y_ref[...] = (jnp.concatenate(nrm, axis=1) * ga_ref[...].astype(f32)
                  + gb_ref[...].astype(f32)).astype(y_ref.dtype)

    @pl.when(jnp.logical_or(c == npc - 1, c >= npc))
    def _():
        sout_ref[0] = S_ref[...]


RWKV_HEADS_PER_STEP = 32


def _rwkv_core(ops, proj, v_block, gl, state, ga, gb, npc, nchunks):
    T, D = ga.shape
    N = RWKV_HS
    nh = D // N
    nb = state.shape[0]
    G = RWKV_HEADS_PER_STEP
    W = G * N
    tok = pl.BlockSpec((CHUNK, W), lambda p, c: (c, p))
    vtok = pl.BlockSpec((CHUNK, W), lambda p, c: (c, v_block * (D // W) + p))

    def sin_map(p, c):
        return (jnp.maximum(c - npc, 0), p, 0, 0)

    def sout_map(p, c):
        return (jnp.maximum(c - npc + 1, 0), p, 0, 0)

    return pl.pallas_call(
        functools.partial(_rwkv_core_kernel, npc=npc, nhead=G),
        out_shape=(jax.ShapeDtypeStruct((T, D), bf16),
                   jax.ShapeDtypeStruct((nb + 1, nh, N, N), f32)),
        grid=(nh // G, nchunks),
        in_specs=[tok] * 6 + [vtok, pl.BlockSpec((1, 1, W), lambda p, c: (c, 0, p)),
                              pl.BlockSpec((1, G, N, N), sin_map), tok, tok],
        out_specs=(tok, pl.BlockSpec((1, G, N, N), sout_map)),
        scratch_shapes=[pltpu.VMEM((G, N, N), f32)],
        compiler_params=_params(("parallel", "arbitrary")),
        name="rwkv_core",
    )(*ops, proj, gl, state, ga, gb)


def _positions(n_prompt, n_b, n_s):
    return jnp.concatenate([jnp.arange(n_prompt), jnp.tile(PAST_LEN + jnp.arange(n_s), n_b)]).astype(f32)


def _attn_tables(pos):
    half = ROT_DIM // 2
    inv = ROPE_THETA ** (-jnp.arange(half, dtype=f32) / half)
    ang = pos[:, None] * inv[None, :]
    cos, sin = jnp.cos(ang), jnp.sin(ang)
    T = pos.shape[0]
    zeros = jnp.zeros((T, half), f32)
    rest = HEAD_DIM - ROT_DIM
    c = jnp.concatenate([cos, cos, jnp.ones((T, rest), f32)], axis=1)
    s1 = jnp.concatenate([zeros, sin, jnp.zeros((T, rest), f32)], axis=1)
    s2 = jnp.concatenate([-sin, zeros, jnp.zeros((T, rest), f32)], axis=1)
    rep = 256 // HEAD_DIM
    return tuple(jnp.tile(t, (1, rep)) for t in (c, s1, s2))


def _ret_tables(pos):
    half = RET_DK // 2
    inv = RET_THETA ** (-jnp.arange(half, dtype=f32) / half)
    ang = pos[:, None] * inv[None, :]
    return jnp.cos(ang), jnp.sin(ang)


def _pad_cols(w, n):
    return jnp.pad(w, ((0, 0), (0, n - w.shape[1])))


def _pad_rows_to_lanes(w):
    n = -(-w.shape[0] // LANES) * LANES
    return jnp.pad(w, ((0, n - w.shape[0]), (0, 0)))


def kernel(x_prompt, x_sample, cache_attn_k, cache_attn_v, state_ret, state_rwkv, state_rwkv_shift,
           norm_mix, norm_mlp, norm_final,
           attn_w_qkv, attn_sinks, attn_w_o,
           ret_w_in, ret_gn_w, ret_w_o,
           rwkv_mu, rwkv_w_rkv, rwkv_w_o, rwkv_w0, rwkv_w1, rwkv_w2, rwkv_a0, rwkv_a1, rwkv_a2,
           rwkv_g1, rwkv_g2, rwkv_k_k, rwkv_k_a, rwkv_r_k, rwkv_ln_w, rwkv_ln_b,
           mlp_w_up, mlp_w_down):
    bp, tp, D = x_prompt.shape
    nb, ns, _ = x_sample.shape
    assert bp == 1 and ns == CHUNK and tp % CHUNK == 0
    depth = norm_mix.shape[0]
    npc = tp // CHUNK
    nchunks = npc + nb
    T = tp + nb * ns
    qd = ATTN_HEADS * HEAD_DIM
    kvd = ATTN_KV_HEADS * HEAD_DIM

    x = (x_prompt.reshape(tp, D), x_sample.reshape(nb * ns, D))
    pos = _positions(tp, nb, ns)
    attn_tabs = _attn_tables(pos)
    ret_tabs = _ret_tables(pos)
    lg = jnp.log1p(-jnp.exp2(-5.0 - jnp.arange(RET_HEADS, dtype=f32)))
    tm_proj = _pick(T, (512, 256, 128, 64))
    tm_ret = _pick(T, (1024, 512, 256, 128, 64))
    w_up16 = mlp_w_up.astype(bf16)
    w_down16 = mlp_w_down.astype(bf16)

    kp_l, vp_l, ks_l, vs_l, rp_l, rs_l, wp_l, ws_l, shp_l, shs_l = ([] for _ in range(10))
    for i in range(depth):
        j, kind = divmod(i, 3)
        g_mix = norm_mix[i][None, :]
        if kind == 0:
            nsub = (qd + 2 * kvd) // (2 * EPI_COLS)
            routes = [[("q", ((0, s * EPI_COLS),)) for s in range(nsub)],
                      [("q", ((0, s * EPI_COLS),)) for s in range(nsub, qd // EPI_COLS)]
                      + [("k", ((1, 0), (2, 0))), ("v", ((1, kvd), (2, 2 * kvd)))]]
            q, kv, kvdup = _norm_matmul(
                x, g_mix, attn_w_qkv[j].astype(bf16), attn_tabs, routes, _epi_attn,
                [jax.ShapeDtypeStruct((T, qd), bf16), jax.ShapeDtypeStruct((T, 2 * kvd), f32),
                 jax.ShapeDtypeStruct((T, 4 * kvd), bf16)],
                [pl.BlockSpec((tm_proj, qd), lambda i, jj: (i, 0)),
                 pl.BlockSpec((tm_proj, 2 * kvd), lambda i, jj: (i, 0)),
                 pl.BlockSpec((tm_proj, 4 * kvd), lambda i, jj: (i, 0))],
                tm_proj, "attn_qkv")
            k_new = kv[:, :kvd]
            v_new = kv[:, kvd:]

            def ext(new, dup, cache):
                c3 = cache.reshape(nb, WINDOW, ATTN_KV_HEADS, 1, HEAD_DIM)
                cdup = jnp.broadcast_to(c3, (nb, WINDOW, ATTN_KV_HEADS, 2, HEAD_DIM)).astype(bf16)
                samp_dup = jnp.concatenate([cdup.reshape(nb, WINDOW, 2 * kvd),
                                            dup[tp:].reshape(nb, ns, 2 * kvd)], axis=1)
                rows = jnp.concatenate([jnp.zeros((WINDOW, 2 * kvd), bf16), dup[:tp],
                                        samp_dup.reshape(nb * (WINDOW + ns), 2 * kvd)], axis=0)
                samp = jnp.concatenate([cache.reshape(nb, WINDOW, kvd), new[tp:].reshape(nb, ns, kvd)], axis=1)
                return rows, samp

            kext, k_samp = ext(k_new, kvdup[:, :2 * kvd], cache_attn_k[j])
            vext, v_samp = ext(v_new, kvdup[:, 2 * kvd:], cache_attn_v[j])
            o = _attention(q, kext, vext, attn_sinks[j], npc, nchunks)
            x = _matmul_residual(o, attn_w_o[j].astype(bf16), x, "attn_out")
            kp_l.append(k_new[tp - WINDOW:tp].reshape(1, WINDOW, ATTN_KV_HEADS, HEAD_DIM))
            vp_l.append(v_new[tp - WINDOW:tp].reshape(1, WINDOW, ATTN_KV_HEADS, HEAD_DIM))
            ks_l.append(k_samp[:, -WINDOW:].reshape(nb, WINDOW, ATTN_KV_HEADS, HEAD_DIM))
            vs_l.append(v_samp[:, -WINDOW:].reshape(nb, WINDOW, ATTN_KV_HEADS, HEAD_DIM))
        elif kind == 1:
            nq = RET_HEADS * RET_DK // 1024
            nv = RET_HEADS * RET_DV // 1024
            tile = lambda mode: [(mode, ((0, s * EPI_COLS),)) for s in range(1024 // EPI_COLS)]
            routes = [tile("q")] * nq + [tile("k")] * nq + [tile("plain")] * (2 * nv)
            (proj,) = _norm_matmul(
                x, g_mix, ret_w_in[j].astype(bf16), ret_tabs, routes, _epi_ret,
                [jax.ShapeDtypeStruct((T, ret_w_in.shape[2]), bf16)],
                [pl.BlockSpec((tm_ret, 1024), lambda i, jj: (i, jj))], tm_ret, "ret_proj")
            gnw = ret_gn_w[j][None, :]
            lp = _pick(tp, (256, 128, 64))
            y_p, s_p = _retention(proj, state_ret[j], gnw, lg, 0, tp, lp, False, "ret_prompt")
            y_s, s_s = _retention(proj, state_ret[j], gnw, lg, tp, nb * ns, ns, True, "ret_sample")
            x = _matmul_residual2(y_p, y_s, ret_w_o[j].astype(bf16), x, "ret_out")
            rp_l.append(s_p)
            rs_l.append(s_s)
        else:
            starts = jnp.concatenate([jnp.zeros((npc, D), f32), state_rwkv_shift[j]], axis=0)[:, None, :]
            P = LORA_PAD
            tn = 512
            wcat = jnp.concatenate(
                [rwkv_w_rkv[j][0], rwkv_w_rkv[j][1], rwkv_w_rkv[j][2],
                 _pad_cols(rwkv_w1[j], P), _pad_cols(rwkv_a1[j], P), _pad_cols(rwkv_g1[j], P)],
                axis=1).astype(bf16)
            tiles_per_d = D // tn
            mu = rwkv_mu[j]
            mu_tiles = jnp.concatenate(
                [jnp.repeat(mu[jnp.array([0, 2, 3])], tiles_per_d, axis=0), mu[jnp.array([1, 4, 5])]],
                axis=0)[:, None, :]
            proj, h_last = _lerp_matmul(x, g_mix, starts, mu_tiles, wcat, tn, tiles_per_d,
                                        3 * tiles_per_d, npc)
            prep = _rwkv_prep(
                proj, 3 * D // (3 * P),
                _pad_rows_to_lanes(rwkv_w2[j]).astype(bf16), _pad_rows_to_lanes(rwkv_a2[j]).astype(bf16),
                _pad_rows_to_lanes(rwkv_g2[j]).astype(bf16), rwkv_w0[j][None, :], rwkv_a0[j][None, :],
                rwkv_k_k[j][None, :], rwkv_k_a[j][None, :], rwkv_r_k[j].reshape(1, D),
                rwkv_ln_w[j][None, :], rwkv_ln_b[j][None, :])
            ops, gl, ga, gb = prep[:6], prep[6], prep[7], prep[8]
            y, s_all = _rwkv_core(ops, proj, 2, gl, state_rwkv[j], ga, gb, npc, nchunks)
            x = _matmul_residual(y, rwkv_w_o[j].astype(bf16), x, "rwkv_out")
            wp_l.append(s_all[:1])
            ws_l.append(s_all[1:])
            shp_l.append(h_last[npc - 1])
            shs_l.append(h_last[npc:, 0])
        x = _mlp(x, norm_mlp[i][None, :], w_up16, w_down16, i, norm_final[None, :],
                 tp if i == depth - 1 else None, "mlp")

    y_prompt = x[0].reshape(1, tp, D)
    y_sample = x[1].reshape(nb, ns, D)
    return (y_prompt, y_sample,
            jnp.stack(kp_l), jnp.stack(vp_l), jnp.stack(ks_l), jnp.stack(vs_l),
            jnp.stack(rp_l), jnp.stack(rs_l),
            jnp.stack(wp_l), jnp.stack(ws_l), jnp.stack(shp_l), jnp.stack(shs_l))
```

```python
import functools

import jax
import jax.numpy as jnp
from jax import lax
from jax.experimental import pallas as pl
from jax.experimental.pallas import tpu as pltpu

f32 = jnp.float32
bf16 = jnp.bfloat16

CHUNK = 64
NORM_EPS = 1e-5
PAST_LEN = 4096

ATTN_HEADS = 32
ATTN_KV_HEADS = 4
ATTN_GROUP = ATTN_HEADS // ATTN_KV_HEADS
HEAD_DIM = 64
WINDOW = 128
ROT_DIM = HEAD_DIM // 4
ROPE_THETA = 500000.0

RET_HEADS = 8
RET_DK = 256
RET_DV = 512
RET_THETA = 10000.0
RET_GN_EPS = 1e-5

RWKV_HS = 64
RWKV_GN_EPS = 64e-5
LORA_PAD = 512

VMEM_LIMIT = 52 * 1024 * 1024

NT_DIMS = (((1,), (1,)), ((), ()))


def _pick(n, cands):
    for c in cands:
        if n % c == 0:
            return c
    raise ValueError(f"no tile for {n} in {cands}")


def _params(sem):
    return pltpu.CompilerParams(dimension_semantics=sem, vmem_limit_bytes=VMEM_LIMIT)


def _rms(x, g):
    return x * lax.rsqrt(jnp.mean(x * x, axis=-1, keepdims=True) + NORM_EPS) * g


def _dup_heads(x):
    pw = 2 * HEAD_DIM
    lo = lax.broadcasted_iota(jnp.int32, (1, pw), 1) < HEAD_DIM
    out = []
    for c in range(x.shape[1] // pw):
        xc = x[:, c * pw:(c + 1) * pw]
        sw = pltpu.roll(xc, HEAD_DIM, 1)
        out += [jnp.where(lo, xc, sw), jnp.where(lo, sw, xc)]
    return jnp.concatenate(out, axis=1)


def _epi_attn(blk, mode, tabs):
    if mode == "v":
        return blk, _dup_heads(blk)
    c, s1, s2 = (t[...] for t in tabs)
    pw, half = c.shape[1], ROT_DIM // 2
    out = jnp.concatenate(
        [xc * c + pltpu.roll(xc, half, 1) * s1 + pltpu.roll(xc, pw - half, 1) * s2
         for xc in (blk[:, i * pw:(i + 1) * pw] for i in range(blk.shape[1] // pw))], axis=1)
    if mode == "q":
        return (out * (HEAD_DIM ** -0.5),)
    return out, _dup_heads(out)


def _epi_ret(blk, mode, tabs):
    if mode == "plain":
        return (blk,)
    cos_ref, sin_ref = tabs
    c, s = cos_ref[...], sin_ref[...]
    x1, x2 = blk[:, :128], blk[:, 128:]
    out = jnp.concatenate([x1 * c - x2 * s, x2 * c + x1 * s], axis=1)
    return (out * (RET_DK ** -0.5) if mode == "k" else out,)


EPI_COLS = 256


def _row_split_specs(tm, cols, split_at):
    first = pl.BlockSpec((tm, cols), lambda i, *_: (jnp.minimum(i, split_at - 1), 0))
    second = pl.BlockSpec((tm, cols), lambda i, *_: (jnp.maximum(i - split_at, 0), 0))
    return [first, second]


def _normmm_kernel(*refs, routes, ntab, nout, epi, split_at):
    nx = 1 if split_at is None else 2
    x_refs = refs[:nx]
    g_ref, w_ref = refs[nx:nx + 2]
    tabs = refs[nx + 2:nx + 2 + ntab]
    outs = refs[nx + 2 + ntab:nx + 2 + ntab + nout]
    xn_ref = refs[nx + 2 + ntab + nout]
    i = pl.program_id(0)
    j = pl.program_id(1)

    if split_at is None:
        @pl.when(j == 0)
        def _():
            xn_ref[...] = _rms(x_refs[0][...], g_ref[...]).astype(bf16)
    else:
        @pl.when(jnp.logical_and(j == 0, i < split_at))
        def _():
            xn_ref[...] = _rms(x_refs[0][...], g_ref[...]).astype(bf16)

        @pl.when(jnp.logical_and(j == 0, i >= split_at))
        def _():
            xn_ref[...] = _rms(x_refs[1][...], g_ref[...]).astype(bf16)

    groups = {}
    for jj, tile_routes in enumerate(routes):
        groups.setdefault(tuple(tile_routes), []).append(jj)
    for tile_routes, jjs in groups.items():
        cond = functools.reduce(jnp.logical_or, [j == jj for jj in jjs])

        @pl.when(cond)
        def _(tile_routes=tile_routes):
            for s, (mode, dests) in enumerate(tile_routes):
                acc = jnp.dot(xn_ref[...], w_ref[:, s * EPI_COLS:(s + 1) * EPI_COLS],
                              preferred_element_type=f32)
                for (oi, col), val in zip(dests, epi(acc, mode, tabs)):
                    outs[oi][:, col:col + val.shape[1]] = val.astype(outs[oi].dtype)


def _norm_matmul(x, g, w, tabs, routes, epi, out_shapes, out_specs, tm, name):
    xs = x if isinstance(x, tuple) else (x,)
    T = sum(p.shape[0] for p in xs)
    D = xs[0].shape[1]
    N = w.shape[1]
    ntile = len(routes)
    tn = N // ntile
    assert tn * ntile == N and tn == EPI_COLS * len(routes[0])
    if len(xs) == 1:
        split_at = None
        x_specs = [pl.BlockSpec((tm, D), lambda i, j: (i, 0))]
    else:
        assert xs[0].shape[0] % tm == 0 and xs[1].shape[0] % tm == 0
        split_at = xs[0].shape[0] // tm
        x_specs = _row_split_specs(tm, D, split_at)
    tab_specs = [pl.BlockSpec((tm, t.shape[1]), lambda i, j: (i, 0)) for t in tabs]
    return pl.pallas_call(
        functools.partial(_normmm_kernel, routes=routes, ntab=len(tabs), nout=len(out_shapes), epi=epi,
                          split_at=split_at),
        out_shape=tuple(out_shapes),
        grid=(T // tm, ntile),
        in_specs=x_specs + [pl.BlockSpec((1, D), lambda i, j: (0, 0)),
                            pl.BlockSpec((D, tn), lambda i, j: (0, j))] + tab_specs,
        out_specs=tuple(out_specs),
        scratch_shapes=[pltpu.VMEM((tm, D), bf16)],
        compiler_params=_params(("parallel", "arbitrary")),
        name=name,
    )(*xs, g, w, *tabs)


def _mmres_kernel(a_ref, w_ref, *rest, split_at):
    o_ref = rest[-1]
    prod = jnp.dot(a_ref[...], w_ref[...], preferred_element_type=f32)
    if split_at is None:
        o_ref[...] = rest[0][...] + prod
    else:
        i = pl.program_id(0)

        @pl.when(i < split_at)
        def _():
            o_ref[...] = rest[0][...] + prod

        @pl.when(i >= split_at)
        def _():
            o_ref[...] = rest[1][...] + prod


def _resident(shape):
    return pl.BlockSpec(shape, lambda *_: (0,) * len(shape), pipeline_mode=pl.Buffered(1))


def _matmul_residual(a, w, res, name):
    T, K = a.shape
    N = w.shape[1]
    rs = res if isinstance(res, tuple) else (res,)
    if len(rs) == 1:
        tm = _pick(T, (512, 256, 128, 64))
        split_at = None
        r_specs = [pl.BlockSpec((tm, N), lambda i: (i, 0))]
    else:
        tm = _pick(rs[1].shape[0], (512, 256, 128, 64))
        assert rs[0].shape[0] % tm == 0
        split_at = rs[0].shape[0] // tm
        r_specs = _row_split_specs(tm, N, split_at)
    return pl.pallas_call(
        functools.partial(_mmres_kernel, split_at=split_at),
        out_shape=jax.ShapeDtypeStruct((T, N), f32),
        grid=(T // tm,),
        in_specs=[pl.BlockSpec((tm, K), lambda i: (i, 0)), _resident((K, N))] + r_specs,
        out_specs=pl.BlockSpec((tm, N), lambda i: (i, 0)),
        compiler_params=_params(("parallel",)),
        name=name,
    )(a, w, *rs)


def _mmres2_kernel(ap_ref, as_ref, w_ref, r_ref, o_ref, *, np_blocks):
    i = pl.program_id(0)

    @pl.when(i < np_blocks)
    def _():
        o_ref[...] = r_ref[...] + jnp.dot(ap_ref[...], w_ref[...], preferred_element_type=f32)

    @pl.when(i >= np_blocks)
    def _():
        o_ref[...] = r_ref[...] + jnp.dot(as_ref[...], w_ref[...], preferred_element_type=f32)


def _matmul_residual2(a_p, a_s, w, res, name):
    tp, K = a_p.shape
    ts = a_s.shape[0]
    N = w.shape[1]
    tm = _pick(ts, (256, 128, 64))
    assert tp % tm == 0
    np_blocks = tp // tm
    return pl.pallas_call(
        functools.partial(_mmres2_kernel, np_blocks=np_blocks),
        out_shape=jax.ShapeDtypeStruct((tp + ts, N), f32),
        grid=((tp + ts) // tm,),
        in_specs=[pl.BlockSpec((tm, K), lambda i: (jnp.minimum(i, np_blocks - 1), 0)),
                  pl.BlockSpec((tm, K), lambda i: (jnp.maximum(i - np_blocks, 0), 0)),
                  _resident((K, N)),
                  pl.BlockSpec((tm, N), lambda i: (i, 0))],
        out_specs=pl.BlockSpec((tm, N), lambda i: (i, 0)),
        compiler_params=_params(("parallel",)),
        name=name,
    )(a_p, a_s, w, res)


def _mlp_kernel(x_ref, g_ref, wu_ref, wd_ref, gf_ref, *rest, nk, split_at):
    if split_at is None:
        acc_ref, xn_ref = rest
    else:
        op_ref, os_ref, xn_ref, acc_ref = rest
    i = pl.program_id(0)
    k = pl.program_id(1)

    @pl.when(k == 0)
    def _():
        x = x_ref[...]
        xn_ref[...] = _rms(x, g_ref[...]).astype(bf16)
        acc_ref[...] = x

    h = jnp.dot(xn_ref[...], wu_ref[...], preferred_element_type=f32)
    h = jnp.square(jnp.maximum(h, 0.0)).astype(bf16)
    acc_ref[...] += jnp.dot(h, wd_ref[...], preferred_element_type=f32)

    if split_at is not None:
        @pl.when(jnp.logical_and(k == nk - 1, i < split_at))
        def _():
            op_ref[...] = _rms(acc_ref[...], gf_ref[...])

        @pl.when(jnp.logical_and(k == nk - 1, i >= split_at))
        def _():
            os_ref[...] = _rms(acc_ref[...], gf_ref[...])


def _mlp(x, g, w_up, w_down, layer, g_final, split_rows, name):
    T, D = x.shape
    F = w_up.shape[2]
    tf = _pick(F, (1024, 512))
    nk = F // tf
    if split_rows is None:
        tm = _pick(T, (512, 256, 128, 64))
        x_spec = pl.BlockSpec((tm, D), lambda i, k: (i, 0))
        split_at = None
        out_shape = jax.ShapeDtypeStruct((T, D), f32)
        out_specs = pl.BlockSpec((tm, D), lambda i, k: (i, 0))
        scratch = [pltpu.VMEM((tm, D), bf16)]
    else:
        tm = _pick(T - split_rows, (512, 256, 128, 64))
        x_spec = pl.BlockSpec((tm, D), lambda i, k: (i, 0))
        assert split_rows % tm == 0
        split_at = split_rows // tm
        out_shape = (jax.ShapeDtypeStruct((split_rows, D), f32), jax.ShapeDtypeStruct((T - split_rows, D), f32))
        out_specs = (pl.BlockSpec((tm, D), lambda i, k: (jnp.minimum(i, split_at - 1), 0)),
                     pl.BlockSpec((tm, D), lambda i, k: (jnp.maximum(i - split_at, 0), 0)))
        scratch = [pltpu.VMEM((tm, D), bf16), pltpu.VMEM((tm, D), f32)]
    return pl.pallas_call(
        functools.partial(_mlp_kernel, nk=nk, split_at=split_at),
        out_shape=out_shape,
        grid=(T // tm, nk),
        in_specs=[x_spec,
                  pl.BlockSpec((1, D), lambda i, k: (0, 0)),
                  pl.BlockSpec((None, D, tf), lambda i, k: (layer, 0, k)),
                  pl.BlockSpec((None, tf, D), lambda i, k: (layer, k, 0)),
                  pl.BlockSpec((1, D), lambda i, k: (0, 0))],
        out_specs=out_specs,
        scratch_shapes=scratch,
        compiler_params=_params(("arbitrary", "arbitrary")),
        name=name,
    )(x, g, w_up, w_down, g_final)


def _attn_kernel(sb_ref, q_ref, kp0_ref, kp1_ref, kc0_ref, kc1_ref, k2_ref,
                 vp0_ref, vp1_ref, vc0_ref, vc1_ref, v2_ref, o_ref, *, npc):
    c = pl.program_id(0)
    is_sample = c >= npc
    k0, k1, v0, v1 = (
        jnp.where(is_sample, cr[...], pr[...])
        for cr, pr in ((kc0_ref, kp0_ref), (kc1_ref, kp1_ref), (vc0_ref, vp0_ref), (vc1_ref, vp1_ref)))
    nkeys = 3 * CHUNK
    ncols = nkeys + CHUNK
    PW = 2 * HEAD_DIM
    pairs = ATTN_GROUP // 2
    col_blk = lax.broadcasted_iota(jnp.int32, (1, ncols), 1) // CHUNK
    valid = jnp.logical_or(jnp.logical_or(is_sample, c + col_blk >= 2), col_blk >= 3)
    bias = jnp.where(valid, 0.0, -jnp.inf).astype(f32)
    zpad = jnp.zeros((CHUNK, ATTN_KV_HEADS * PW), bf16)
    kcat = jnp.concatenate([k0, k1, k2_ref[...], zpad], axis=0)
    vcat = jnp.concatenate([v0, v1, v2_ref[...], zpad], axis=0)
    lo_half = lax.broadcasted_iota(jnp.int32, (1, PW), 1) < HEAD_DIM
    ones = jnp.ones((ncols, PW), bf16)
    kvs = range(ATTN_KV_HEADS)

    def stacked_q(kv):
        parts = []
        for p in range(pairs):
            col = (kv * pairs + p) * PW
            qp = q_ref[:, col:col + PW]
            parts += [jnp.where(lo_half, qp, 0.0), jnp.where(lo_half, 0.0, qp)]
        return jnp.concatenate(parts, axis=0).astype(bf16)

    s = [lax.dot_general(stacked_q(kv), kcat[:, kv * PW:(kv + 1) * PW], NT_DIMS,
                         preferred_element_type=f32) + sb_ref[kv] + bias for kv in kvs]
    p = [jnp.exp(s[kv] - jnp.max(s[kv], axis=-1, keepdims=True)).astype(bf16) for kv in kvs]
    oa = [jnp.dot(p[kv], jnp.concatenate([vcat[:, kv * PW:(kv + 1) * PW], ones], axis=1),
                  preferred_element_type=f32) for kv in kvs]
    for kv in kvs:
        on = oa[kv][:, :PW] / oa[kv][:, PW:]
        for pi in range(pairs):
            r0 = 2 * pi * CHUNK
            blk = jnp.where(lo_half, on[r0:r0 + CHUNK], on[r0 + CHUNK:r0 + 2 * CHUNK])
            col = (kv * pairs + pi) * PW
            o_ref[:, col:col + PW] = blk.astype(o_ref.dtype)


def _attention(q, kvdup, kcache, vcache, sinks, npc, nchunks):
    T = q.shape[0]
    qd = ATTN_HEADS * HEAD_DIM
    dw = kcache.shape[1]
    wblocks = WINDOW // CHUNK

    def new_spec(j, col):
        if j == 2:
            return pl.BlockSpec((CHUNK, dw), lambda c: (c, col))
        return pl.BlockSpec((CHUNK, dw), lambda c: (jnp.maximum(jnp.minimum(c, npc - 1) - 2 + j, 0), col))

    def cache_spec(j):
        return pl.BlockSpec((CHUNK, dw), lambda c: (wblocks * jnp.maximum(c - npc, 0) + j, 0))

    nkeys = 3 * CHUNK
    col = jnp.arange(nkeys + CHUNK)[None, None, :]
    sink_rows = jnp.repeat(sinks.astype(f32).reshape(ATTN_KV_HEADS, ATTN_GROUP), CHUNK, axis=1)[:, :, None]
    sink_bias = jnp.where(col < nkeys, 0.0, jnp.where(col == nkeys, sink_rows, -jnp.inf)).astype(f32)

    return pl.pallas_call(
        functools.partial(_attn_kernel, npc=npc),
        out_shape=jax.ShapeDtypeStruct((T, qd), bf16),
        grid=(nchunks,),
        in_specs=[pl.BlockSpec(sink_bias.shape, lambda c: (0, 0, 0)),
                  pl.BlockSpec((CHUNK, qd), lambda c: (c, 0)),
                  new_spec(0, 0), new_spec(1, 0), cache_spec(0), cache_spec(1), new_spec(2, 0),
                  new_spec(0, 1), new_spec(1, 1), cache_spec(0), cache_spec(1), new_spec(2, 1)],
        out_specs=pl.BlockSpec((CHUNK, qd), lambda c: (c, 0)),
        compiler_params=_params(("parallel",)),
        name="attn_core",
    )(sink_bias, q, kvdup, kvdup, kcache, kcache, kvdup, kvdup, kvdup, vcache, vcache, kvdup)


RET_HEADS_PER_STEP = 4


def _ret_kernel(lg_ref, q_ref, k_ref, v_ref, g_ref, s0_ref, gnw_ref, y_ref, sout_ref, S_ref, *, from_state):
    hg = pl.program_id(0)
    c = pl.program_id(1)
    L = q_ref.shape[0]
    nh = RET_HEADS_PER_STEP
    hs = range(nh)
    dot = functools.partial(jnp.dot, preferred_element_type=f32)

    if from_state:
        S_ref[...] = s0_ref[0]
    else:
        @pl.when(c == 0)
        def _():
            S_ref[...] = jnp.zeros_like(S_ref)

    lg = [lg_ref[hg * nh + h] for h in hs]
    q = [q_ref[:, h * RET_DK:(h + 1) * RET_DK] for h in hs]
    k = [k_ref[:, h * RET_DK:(h + 1) * RET_DK] for h in hs]
    v = [v_ref[:, h * RET_DV:(h + 1) * RET_DV] for h in hs]
    row = lax.broadcasted_iota(jnp.int32, (L, L), 0)
    col = lax.broadcasted_iota(jnp.int32, (L, L), 1)
    diff = (row - col).astype(f32)
    idx = lax.broadcasted_iota(jnp.int32, (L, 1), 0).astype(f32)
    decay = [jnp.where(diff >= 0, jnp.exp(lg[h] * jnp.maximum(diff, 0.0)), 0.0) for h in hs]
    xi = [jnp.exp(lg[h] * (idx + 1.0)) for h in hs]
    zeta = [jnp.exp(lg[h] * (L - 1.0 - idx)) for h in hs]

    S = [S_ref[h] for h in hs]
    scores = [lax.dot_general(q[h], k[h], NT_DIMS, preferred_element_type=f32) * decay[h] for h in hs]
    o = [dot(scores[h].astype(bf16), v[h])
         + dot((q[h].astype(f32) * xi[h]).astype(bf16), S[h].astype(bf16)) for h in hs]
    S_new = [jnp.exp(lg[h] * L) * S[h] + dot((k[h].astype(f32) * zeta[h]).T.astype(bf16), v[h])
             for h in hs]
    for h in hs:
        S_ref[h] = S_new[h]
        sout_ref[0, h] = S_new[h]

    cen = [o[h] - jnp.mean(o[h], axis=-1, keepdims=True) for h in hs]
    on = [cen[h] * lax.rsqrt(jnp.mean(cen[h] * cen[h], axis=-1, keepdims=True) + RET_GN_EPS) for h in hs]
    for h in hs:
        sl = slice(h * RET_DV, (h + 1) * RET_DV)
        y_ref[:, sl] = (jax.nn.silu(g_ref[:, sl].astype(f32)) * on[h] * gnw_ref[:, sl]).astype(y_ref.dtype)


def _retention(proj, state, gn_w, lg, row0, nrows, L, from_state, name):
    nsteps = nrows // L
    b0 = row0 // L
    assert nsteps * L == nrows and b0 * L == row0
    nseq = nsteps if from_state else 1
    G = RET_HEADS_PER_STEP
    kw, vw = G * RET_DK, G * RET_DV
    kb = RET_HEADS * RET_DK // kw
    vb = 2 * RET_HEADS * RET_DK // vw
    gb = vb + RET_HEADS * RET_DV // vw
    state_map = (lambda h, c: (c, h, 0, 0)) if from_state else (lambda h, c: (0, h, 0, 0))

    return pl.pallas_call(
        functools.partial(_ret_kernel, from_state=from_state),
        out_shape=(jax.ShapeDtypeStruct((nrows, RET_HEADS * RET_DV), bf16),
                   jax.ShapeDtypeStruct((nseq, RET_HEADS, RET_DK, RET_DV), f32)),
        grid=(RET_HEADS // G, nsteps),
        in_specs=[pl.BlockSpec(memory_space=pltpu.SMEM),
                  pl.BlockSpec((L, kw), lambda h, c: (b0 + c, h)),
                  pl.BlockSpec((L, kw), lambda h, c: (b0 + c, kb + h)),
                  pl.BlockSpec((L, vw), lambda h, c: (b0 + c, vb + h)),
                  pl.BlockSpec((L, vw), lambda h, c: (b0 + c, gb + h)),
                  pl.BlockSpec((1, G, RET_DK, RET_DV), state_map),
                  pl.BlockSpec((1, vw), lambda h, c: (0, h))],
        out_specs=(pl.BlockSpec((L, vw), lambda h, c: (c, h)),
                   pl.BlockSpec((1, G, RET_DK, RET_DV), state_map)),
        scratch_shapes=[pltpu.VMEM((G, RET_DK, RET_DV), f32)],
        compiler_params=_params(("parallel", "arbitrary")),
        name=name,
    )(lg, proj, proj, proj, proj, state, gn_w)


SUBLANES = 8


def _lerpmm_kernel(x_ref, xprev_ref, g_ref, start_ref, mu_ref, w_ref, o_ref, hlast_ref,
                   h_ref, xx_ref, l_ref, *, tiles_per_d, n_big, npc):
    i = pl.program_id(0)
    j = pl.program_id(1)
    tm = x_ref.shape[0]
    cpb = tm // CHUNK

    @pl.when(j == 0)
    def _():
        g = g_ref[...]
        h = _rms(x_ref[...], g)
        h_ref[...] = h
        xx_ref[...] = pltpu.roll(h, 1, 0) - h
        prev = _rms(xprev_ref[...], g)[SUBLANES - 1:SUBLANES, :]
        xx_ref[0:1, :] = prev - h[0:1, :]
        for ci in range(cpb):
            gc = i * cpb + ci
            r0 = ci * CHUNK

            @pl.when(jnp.logical_or(gc == 0, gc >= npc))
            def _(ci=ci, r0=r0):
                xx_ref[r0:r0 + 1, :] = start_ref[ci] - h_ref[r0:r0 + 1, :]

            hlast_ref[ci] = h[r0 + CHUNK - 1:r0 + CHUNK, :]

    @pl.when(jnp.logical_or(j % tiles_per_d == 0, j >= n_big))
    def _():
        l_ref[...] = (h_ref[...] + xx_ref[...] * mu_ref[0]).astype(bf16)

    o_ref[...] = jnp.dot(l_ref[...], w_ref[...], preferred_element_type=f32).astype(o_ref.dtype)


def _lerp_matmul(x, g, starts, mu_tiles, wcat, tn, tiles_per_d, n_big, npc):
    T, D = x.shape
    N = wcat.shape[1]
    tm = _pick(T, (1024, 512, 256, 128, 64))
    cpb = tm // CHUNK
    return pl.pallas_call(
        functools.partial(_lerpmm_kernel, tiles_per_d=tiles_per_d, n_big=n_big, npc=npc),
        out_shape=(jax.ShapeDtypeStruct((T, N), bf16),
                   jax.ShapeDtypeStruct((T // CHUNK, 1, D), f32)),
        grid=(T // tm, N // tn),
        in_specs=[pl.BlockSpec((tm, D), lambda i, j: (i, 0)),
                  pl.BlockSpec((SUBLANES, D), lambda i, j: (jnp.maximum(i * (tm // SUBLANES) - 1, 0), 0)),
                  pl.BlockSpec((1, D), lambda i, j: (0, 0)),
                  pl.BlockSpec((cpb, 1, D), lambda i, j: (i, 0, 0)),
                  pl.BlockSpec((1, 1, D), lambda i, j: (j, 0, 0)),
                  pl.BlockSpec((D, tn), lambda i, j: (0, j))],
        out_specs=(pl.BlockSpec((tm, tn), lambda i, j: (i, j)),
                   pl.BlockSpec((cpb, 1, D), lambda i, j: (i, 0, 0))),
        scratch_shapes=[pltpu.VMEM((tm, D), f32), pltpu.VMEM((tm, D), f32), pltpu.VMEM((tm, D), bf16)],
        compiler_params=_params(("parallel", "arbitrary")),
        name="rwkv_proj",
    )(x, x, g, starts, mu_tiles, wcat)


DECAY_SCALE = 0.6065306597126334
LANES = 128


SEG_TILE = 256


def _head_sum(x):
    rows, D = x.shape
    r = lax.broadcasted_iota(jnp.int32, (SEG_TILE, SEG_TILE), 0) // RWKV_HS
    c = lax.broadcasted_iota(jnp.int32, (SEG_TILE, SEG_TILE), 1) // RWKV_HS
    ones = (r == c).astype(bf16)
    out = [jnp.dot(x[:, j * SEG_TILE:(j + 1) * SEG_TILE].astype(bf16), ones, preferred_element_type=f32)
           for j in range(D // SEG_TILE)]
    return jnp.concatenate(out, axis=1)


def _chunk_cumsum(x):
    rows = x.shape[0]
    row = lax.broadcasted_iota(jnp.int32, (rows, rows), 0)
    col = lax.broadcasted_iota(jnp.int32, (rows, rows), 1)
    tri = jnp.logical_and((row // CHUNK) == (col // CHUNK), row >= col).astype(bf16)
    hi = x.astype(bf16)
    rest = x - hi.astype(f32)
    mid = rest.astype(bf16)
    lo = (rest - mid.astype(f32)).astype(bf16)
    dot = functools.partial(jnp.dot, preferred_element_type=f32)
    return dot(tri, hi) + dot(tri, mid) + dot(tri, lo)


def _rwkv_prep_kernel(p_ref, r_ref, k_ref, v_ref, w2_ref, a2_ref, g2_ref, w0_ref, a0_ref,
                      kk_ref, ka_ref, rk_ref, lnw_ref, lnb_ref,
                      at_ref, rt_ref, bt_ref, kt_ref, bh_ref, kh_ref, gl_ref, ga_ref, gb_ref):
    P = LORA_PAD
    tm = r_ref.shape[0]
    rw, ra, rg = w2_ref.shape[0], a2_ref.shape[0], g2_ref.shape[0]
    pw = jnp.tanh(p_ref[:, :rw].astype(f32)).astype(bf16)
    pa = p_ref[:, P:P + ra]
    pg = jax.nn.sigmoid(p_ref[:, 2 * P:2 * P + rg].astype(f32)).astype(bf16)
    wl = w0_ref[...] + jnp.dot(pw, w2_ref[...], preferred_element_type=f32)
    lw = -DECAY_SCALE * jax.nn.sigmoid(wl)
    a = jax.nn.sigmoid(a0_ref[...] + jnp.dot(pa, a2_ref[...], preferred_element_type=f32))
    g = jnp.dot(pg, g2_ref[...], preferred_element_type=f32)

    cum = _chunk_cumsum(lw)
    tot = jnp.concatenate(
        [jnp.broadcast_to(cum[(ci + 1) * CHUNK - 1:(ci + 1) * CHUNK, :], (CHUNK, cum.shape[1]))
         for ci in range(tm // CHUNK)], axis=0)

    r, k, v = r_ref[...].astype(f32), k_ref[...].astype(f32), v_ref[...].astype(f32)
    kk = k * kk_ref[...]
    kk = kk / jnp.maximum(jnp.sqrt(_head_sum(kk * kk)), 1e-12)
    kmod = k * (1.0 + (a - 1.0) * ka_ref[...])
    beta = kk * a
    e_neg = jnp.exp(-cum)
    e_end = jnp.exp(tot - cum)
    at_ref[...] = (-kk * jnp.exp(cum - lw)).astype(bf16)
    rt_ref[...] = (r * jnp.exp(cum)).astype(bf16)
    bt_ref[...] = (beta * e_neg).astype(bf16)
    kt_ref[...] = (kmod * e_neg).astype(bf16)
    bh_ref[...] = (beta * e_end).astype(bf16)
    kh_ref[...] = (kmod * e_end).astype(bf16)
    bonus = _head_sum(r * kmod * rk_ref[...]) * v
    ga_ref[...] = (lnw_ref[...] * g).astype(bf16)
    gb_ref[...] = ((lnb_ref[...] + bonus) * g).astype(bf16)
    for ci in range(tm // CHUNK):
        gl_ref[ci] = jnp.exp(tot[ci * CHUNK:ci * CHUNK + 1, :])


def _rwkv_prep(proj, lora_block, w2p, a2p, g2p, w0, a0, k_k, k_a, r_k, ln_w, ln_b):
    T = proj.shape[0]
    D = w2p.shape[1]
    P = LORA_PAD
    tm = _pick(T, (128, 64))
    vspec = pl.BlockSpec((1, D), lambda i: (0, 0))
    ospec = pl.BlockSpec((tm, D), lambda i: (i, 0))
    o16 = jax.ShapeDtypeStruct((T, D), bf16)
    return pl.pallas_call(
        _rwkv_prep_kernel,
        out_shape=(o16,) * 6 + (jax.ShapeDtypeStruct((T // CHUNK, 1, D), f32), o16, o16),
        grid=(T // tm,),
        in_specs=[pl.BlockSpec((tm, 3 * P), lambda i: (i, lora_block)),
                  pl.BlockSpec((tm, D), lambda i: (i, 0)),
                  pl.BlockSpec((tm, D), lambda i: (i, 1)),
                  pl.BlockSpec((tm, D), lambda i: (i, 2)),
                  _resident(w2p.shape), _resident(a2p.shape), _resident(g2p.shape)] + [vspec] * 7,
        out_specs=(ospec,) * 6 + (pl.BlockSpec((tm // CHUNK, 1, D), lambda i: (i, 0, 0)), ospec, ospec),
        compiler_params=_params(("parallel",)),
        name="rwkv_prep",
    )(proj, proj, proj, proj, w2p, a2p, g2p, w0, a0, k_k, k_a, r_k, ln_w, ln_b)


def _rwkv_heads_chunk(heads):
    L, N = heads[0][6].shape
    assert L == N
    nh = len(heads)
    rng = range(nh)
    dot = functools.partial(jnp.dot, preferred_element_type=f32)
    row = lax.broadcasted_iota(jnp.int32, (L, 2 * L), 0)
    lane = lax.broadcasted_iota(jnp.int32, (L, 2 * L), 1)
    lo = lane < L
    tok = jnp.where(lo, lane, lane - L)
    incl = row >= tok
    strict = row > tok
    lhs = [jnp.concatenate([h[0], h[1]], axis=0) for h in heads]
    rhs = [jnp.concatenate([h[2], h[3]], axis=0) for h in heads]
    G = [lax.dot_general(lhs[i], rhs[i], NT_DIMS, preferred_element_type=f32) for i in rng]
    LS = [lax.dot_general(lhs[i], heads[i][7].astype(bf16), NT_DIMS, preferred_element_type=f32)
          for i in rng]
    top = [jnp.where(strict, g[:L], 0.0) for g in G]
    a_r = [jnp.where(incl, g[L:], 0.0).astype(bf16) for g in G]
    v2 = [jnp.concatenate([h[6], h[6]], axis=0) for h in heads]
    zero = jnp.zeros((L, N), f32)
    x0 = [LS[i][:L] + dot(jnp.where(lo, 0.0, top[i]).astype(bf16), v2[i]) for i in rng]
    p = [jnp.where(lo, top[i], jnp.concatenate([zero, x0[i]], axis=1)) for i in rng]
    steps = max(1, (L - 1).bit_length())
    for s in range(steps):
        z = [dot(p[i][:, :L].astype(bf16), p[i].astype(bf16)) for i in rng]
        if s < steps - 1:
            p = [z[i] + jnp.where(lo, 0.0, p[i]) for i in rng]
        else:
            u = [(p[i] + z[i])[:, L:] for i in rng]
    uv = [jnp.concatenate([u[i], heads[i][6].astype(f32)], axis=0) for i in rng]
    o = [LS[i][L:] + dot(a_r[i], uv[i].astype(bf16)) for i in rng]
    S_new = [heads[i][7] * heads[i][8]
             + dot(uv[i].T.astype(bf16), jnp.concatenate([heads[i][4], heads[i][5]], axis=0))
             for i in rng]
    return list(zip(o, S_new))


def _rwkv_core_kernel(at_ref, rt_ref, bt_ref, kt_ref, bh_ref, kh_ref, v_ref, gl_ref, s0_ref,
                      ga_ref, gb_ref, y_ref, sout_ref, S_ref, *, npc, nhead):
    c = pl.program_id(1)
    N = RWKV_HS

    @pl.when(c == 0)
    def _():
        S_ref[...] = jnp.zeros_like(S_ref)

    @pl.when(c >= npc)
    def _():
        S_ref[...] = s0_ref[0]

    gl = gl_ref[0]
    heads = []
    for hh in range(nhead):
        sl = slice(hh * N, (hh + 1) * N)
        heads.append((at_ref[:, sl], rt_ref[:, sl], bt_ref[:, sl], kt_ref[:, sl], bh_ref[:, sl],
                      kh_ref[:, sl], v_ref[:, sl], S_ref[hh], gl[:, sl]))
    res = _rwkv_heads_chunk(heads)
    for hh in range(nhead):
        S_ref[hh] = res[hh][1]
    outs = [o for o, _ in res]
    cen = [o - jnp.mean(o, axis=-1, keepdims=True) for o in outs]
    nrm = [d * lax.rsqrt(jnp.mean(d * d, axis=-1, keepdims=True) + RWKV_GN_EPS) for d in cen]
    y_ref[...] = (jnp.concatenate(nrm, axis=1) * ga_ref[...].astype(f32)
                  + gb_ref[...].astype(f32)).astype(y_ref.dtype)

    @pl.when(jnp.logical_or(c == npc - 1, c >= npc))
    def _():
        sout_ref[0] = S_ref[...]


RWKV_HEADS_PER_STEP = 32


def _rwkv_core(ops, proj, v_block, gl, state, ga, gb, npc, nchunks):
    T, D = ga.shape
    N = RWKV_HS
    nh = D // N
    nb = state.shape[0]
    G = RWKV_HEADS_PER_STEP
    W = G * N
    tok = pl.BlockSpec((CHUNK, W), lambda p, c: (c, p))
    vtok = pl.BlockSpec((CHUNK, W), lambda p, c: (c, v_block * (D // W) + p))

    def sin_map(p, c):
        return (jnp.maximum(c - npc, 0), p, 0, 0)

    def sout_map(p, c):
        return (jnp.maximum(c - npc + 1, 0), p, 0, 0)

    return pl.pallas_call(
        functools.partial(_rwkv_core_kernel, npc=npc, nhead=G),
        out_shape=(jax.ShapeDtypeStruct((T, D), bf16),
                   jax.ShapeDtypeStruct((nb + 1, nh, N, N), f32)),
        grid=(nh // G, nchunks),
        in_specs=[tok] * 6 + [vtok, pl.BlockSpec((1, 1, W), lambda p, c: (c, 0, p)),
                              pl.BlockSpec((1, G, N, N), sin_map), tok, tok],
        out_specs=(tok, pl.BlockSpec((1, G, N, N), sout_map)),
        scratch_shapes=[pltpu.VMEM((G, N, N), f32)],
        compiler_params=_params(("parallel", "arbitrary")),
        name="rwkv_core",
    )(*ops, proj, gl, state, ga, gb)


def _positions(n_prompt, n_b, n_s):
    return jnp.concatenate([jnp.arange(n_prompt), jnp.tile(PAST_LEN + jnp.arange(n_s), n_b)]).astype(f32)


def _attn_tables(pos):
    half = ROT_DIM // 2
    inv = ROPE_THETA ** (-jnp.arange(half, dtype=f32) / half)
    ang = pos[:, None] * inv[None, :]
    cos, sin = jnp.cos(ang), jnp.sin(ang)
    T = pos.shape[0]
    zeros = jnp.zeros((T, half), f32)
    rest = HEAD_DIM - ROT_DIM
    c = jnp.concatenate([cos, cos, jnp.ones((T, rest), f32)], axis=1)
    s1 = jnp.concatenate([zeros, sin, jnp.zeros((T, rest), f32)], axis=1)
    s2 = jnp.concatenate([-sin, zeros, jnp.zeros((T, rest), f32)], axis=1)
    return tuple(jnp.tile(t, (1, 2)) for t in (c, s1, s2))


def _ret_tables(pos):
    half = RET_DK // 2
    inv = RET_THETA ** (-jnp.arange(half, dtype=f32) / half)
    ang = pos[:, None] * inv[None, :]
    return jnp.cos(ang), jnp.sin(ang)


def _pad_cols(w, n):
    return jnp.pad(w, ((0, 0), (0, n - w.shape[1])))


def _pad_rows_to_lanes(w):
    n = -(-w.shape[0] // LANES) * LANES
    return jnp.pad(w, ((0, n - w.shape[0]), (0, 0)))


def kernel(x_prompt, x_sample, cache_attn_k, cache_attn_v, state_ret, state_rwkv, state_rwkv_shift,
           norm_mix, norm_mlp, norm_final,
           attn_w_qkv, attn_sinks, attn_w_o,
           ret_w_in, ret_gn_w, ret_w_o,
           rwkv_mu, rwkv_w_rkv, rwkv_w_o, rwkv_w0, rwkv_w1, rwkv_w2, rwkv_a0, rwkv_a1, rwkv_a2,
           rwkv_g1, rwkv_g2, rwkv_k_k, rwkv_k_a, rwkv_r_k, rwkv_ln_w, rwkv_ln_b,
           mlp_w_up, mlp_w_down):
    bp, tp, D = x_prompt.shape
    nb, ns, _ = x_sample.shape
    assert bp == 1 and ns == CHUNK and tp % CHUNK == 0
    depth = norm_mix.shape[0]
    npc = tp // CHUNK
    nchunks = npc + nb
    T = tp + nb * ns
    qd = ATTN_HEADS * HEAD_DIM
    kvd = ATTN_KV_HEADS * HEAD_DIM

    x = (x_prompt.reshape(tp, D), x_sample.reshape(nb * ns, D))
    pos = _positions(tp, nb, ns)
    attn_tabs = _attn_tables(pos)
    ret_tabs = _ret_tables(pos)
    lg = jnp.log1p(-jnp.exp2(-5.0 - jnp.arange(RET_HEADS, dtype=f32)))
    tm_proj = _pick(T, (512, 256, 128, 64))
    tm_ret = _pick(T, (1024, 512, 256, 128, 64))
    w_up16 = mlp_w_up.astype(bf16)
    w_down16 = mlp_w_down.astype(bf16)

    kp_l, vp_l, ks_l, vs_l, rp_l, rs_l, wp_l, ws_l, shp_l, shs_l = ([] for _ in range(10))
    for i in range(depth):
        j, kind = divmod(i, 3)
        g_mix = norm_mix[i][None, :]
        if kind == 0:
            nsub = (qd + 2 * kvd) // (2 * EPI_COLS)
            routes = [[("q", ((0, s * EPI_COLS),)) for s in range(nsub)],
                      [("q", ((0, s * EPI_COLS),)) for s in range(nsub, qd // EPI_COLS)]
                      + [("k", ((1, 0), (2, 0))), ("v", ((1, kvd), (2, 2 * kvd)))]]
            q, kv, kvdup = _norm_matmul(
                x, g_mix, attn_w_qkv[j].astype(bf16), attn_tabs, routes, _epi_attn,
                [jax.ShapeDtypeStruct((T, qd), bf16), jax.ShapeDtypeStruct((T, 2 * kvd), f32),
                 jax.ShapeDtypeStruct((T, 4 * kvd), bf16)],
                [pl.BlockSpec((tm_proj, qd), lambda i, jj: (i, 0)),
                 pl.BlockSpec((tm_proj, 2 * kvd), lambda i, jj: (i, 0)),
                 pl.BlockSpec((tm_proj, 4 * kvd), lambda i, jj: (i, 0))],
                tm_proj, "attn_qkv")
            k_new = kv[:, :kvd]
            v_new = kv[:, kvd:]

            def cache_dup(cache):
                c5 = cache.astype(bf16).reshape(nb * WINDOW, ATTN_KV_HEADS, 1, HEAD_DIM)
                return jnp.broadcast_to(c5, (nb * WINDOW, ATTN_KV_HEADS, 2, HEAD_DIM)).reshape(nb * WINDOW, 2 * kvd)

            def new_window(new, cache):
                return jnp.concatenate([cache.reshape(nb, WINDOW, kvd), new[tp:].reshape(nb, ns, kvd)], axis=1)

            o = _attention(q, kvdup, cache_dup(cache_attn_k[j]), cache_dup(cache_attn_v[j]),
                           attn_sinks[j], npc, nchunks)
            k_samp = new_window(k_new, cache_attn_k[j])
            v_samp = new_window(v_new, cache_attn_v[j])
            x = _matmul_residual(o, attn_w_o[j].astype(bf16), x, "attn_out")
            kp_l.append(k_new[tp - WINDOW:tp].reshape(1, WINDOW, ATTN_KV_HEADS, HEAD_DIM))
            vp_l.append(v_new[tp - WINDOW:tp].reshape(1, WINDOW, ATTN_KV_HEADS, HEAD_DIM))
            ks_l.append(k_samp[:, -WINDOW:].reshape(nb, WINDOW, ATTN_KV_HEADS, HEAD_DIM))
            vs_l.append(v_samp[:, -WINDOW:].reshape(nb, WINDOW, ATTN_KV_HEADS, HEAD_DIM))
        elif kind == 1:
            nq = RET_HEADS * RET_DK // 1024
            nv = RET_HEADS * RET_DV // 1024
            tile = lambda mode: [(mode, ((0, s * EPI_COLS),)) for s in range(1024 // EPI_COLS)]
            routes = [tile("q")] * nq + [tile("k")] * nq + [tile("plain")] * (2 * nv)
            (proj,) = _norm_matmul(
                x, g_mix, ret_w_in[j].astype(bf16), ret_tabs, routes, _epi_ret,
                [jax.ShapeDtypeStruct((T, ret_w_in.shape[2]), bf16)],
                [pl.BlockSpec((tm_ret, 1024), lambda i, jj: (i, jj))], tm_ret, "ret_proj")
            gnw = ret_gn_w[j][None, :]
            lp = _pick(tp, (256, 128, 64))
            y_p, s_p = _retention(proj, state_ret[j], gnw, lg, 0, tp, lp, False, "ret_prompt")
            y_s, s_s = _retention(proj, state_ret[j], gnw, lg, tp, nb * ns, ns, True, "ret_sample")
            x = _matmul_residual2(y_p, y_s, ret_w_o[j].astype(bf16), x, "ret_out")
            rp_l.append(s_p)
            rs_l.append(s_s)
        else:
            starts = jnp.concatenate([jnp.zeros((npc, D), f32), state_rwkv_shift[j]], axis=0)[:, None, :]
            P = LORA_PAD
            tn = 512
            wcat = jnp.concatenate(
                [rwkv_w_rkv[j][0], rwkv_w_rkv[j][1], rwkv_w_rkv[j][2],
                 _pad_cols(rwkv_w1[j], P), _pad_cols(rwkv_a1[j], P), _pad_cols(rwkv_g1[j], P)],
                axis=1).astype(bf16)
            tiles_per_d = D // tn
            mu = rwkv_mu[j]
            mu_tiles = jnp.concatenate(
                [jnp.repeat(mu[jnp.array([0, 2, 3])], tiles_per_d, axis=0), mu[jnp.array([1, 4, 5])]],
                axis=0)[:, None, :]
            proj, h_last = _lerp_matmul(x, g_mix, starts, mu_tiles, wcat, tn, tiles_per_d,
                                        3 * tiles_per_d, npc)
            prep = _rwkv_prep(
                proj, 3 * D // (3 * P),
                _pad_rows_to_lanes(rwkv_w2[j]).astype(bf16), _pad_rows_to_lanes(rwkv_a2[j]).astype(bf16),
                _pad_rows_to_lanes(rwkv_g2[j]).astype(bf16), rwkv_w0[j][None, :], rwkv_a0[j][None, :],
                rwkv_k_k[j][None, :], rwkv_k_a[j][None, :], rwkv_r_k[j].reshape(1, D),
                rwkv_ln_w[j][None, :], rwkv_ln_b[j][None, :])
            ops, gl, ga, gb = prep[:6], prep[6], prep[7], prep[8]
            y, s_all = _rwkv_core(ops, proj, 2, gl, state_rwkv[j], ga, gb, npc, nchunks)
            x = _matmul_residual(y, rwkv_w_o[j].astype(bf16), x, "rwkv_out")
            wp_l.append(s_all[:1])
            ws_l.append(s_all[1:])
            shp_l.append(h_last[npc - 1])
            shs_l.append(h_last[npc:, 0])
        x = _mlp(x, norm_mlp[i][None, :], w_up16, w_down16, i, norm_final[None, :],
                 tp if i == depth - 1 else None, "mlp")

    y_prompt = x[0].reshape(1, tp, D)
    y_sample = x[1].reshape(nb, ns, D)
    return (y_prompt, y_sample,
            jnp.stack(kp_l), jnp.stack(vp_l), jnp.stack(ks_l), jnp.stack(vs_l),
            jnp.stack(rp_l), jnp.stack(rs_l),
            jnp.stack(wp_l), jnp.stack(ws_l), jnp.stack(shp_l), jnp.stack(shs_l))
```

```python
import functools

import jax
import jax.numpy as jnp
from jax import lax
from jax.experimental import pallas as pl
from jax.experimental.pallas import tpu as pltpu

f32 = jnp.float32
bf16 = jnp.bfloat16

CHUNK = 64
NORM_EPS = 1e-5
PAST_LEN = 4096

ATTN_HEADS = 32
ATTN_KV_HEADS = 4
ATTN_GROUP = ATTN_HEADS // ATTN_KV_HEADS
HEAD_DIM = 64
WINDOW = 128
ROT_DIM = HEAD_DIM // 4
ROPE_THETA = 500000.0

RET_HEADS = 8
RET_DK = 256
RET_DV = 512
RET_THETA = 10000.0
RET_GN_EPS = 1e-5
RET_TILE = 1024

RWKV_HS = 64
RWKV_GN_EPS = 64e-5
LORA_PAD = 512

VMEM_LIMIT = 52 * 1024 * 1024

NT_DIMS = (((1,), (1,)), ((), ()))


def _pick(n, cands):
    for c in cands:
        if n % c == 0:
            return c
    raise ValueError(f"no tile for {n} in {cands}")


def _params(sem):
    return pltpu.CompilerParams(dimension_semantics=sem, vmem_limit_bytes=VMEM_LIMIT)


def _rms(x, g):
    return x * lax.rsqrt(jnp.mean(x * x, axis=-1, keepdims=True) + NORM_EPS) * g


def _dup_heads(x):
    pw = 2 * HEAD_DIM
    lo = lax.broadcasted_iota(jnp.int32, (1, pw), 1) < HEAD_DIM
    out = []
    for c in range(x.shape[1] // pw):
        xc = x[:, c * pw:(c + 1) * pw]
        sw = pltpu.roll(xc, HEAD_DIM, 1)
        out += [jnp.where(lo, xc, sw), jnp.where(lo, sw, xc)]
    return jnp.concatenate(out, axis=1)


def _epi_attn(blk, mode, tabs):
    if mode == "v":
        return blk, _dup_heads(blk)
    c, s1, s2 = (t[...] for t in tabs)
    pw, half = c.shape[1], ROT_DIM // 2
    out = jnp.concatenate(
        [xc * c + pltpu.roll(xc, half, 1) * s1 + pltpu.roll(xc, pw - half, 1) * s2
         for xc in (blk[:, i * pw:(i + 1) * pw] for i in range(blk.shape[1] // pw))], axis=1)
    if mode == "q":
        return (out * (HEAD_DIM ** -0.5),)
    return out, _dup_heads(out)


def _epi_ret(blk, mode, tabs):
    if mode == "plain":
        return (blk,)
    cos_ref, sin_ref = tabs
    c, s = cos_ref[...], sin_ref[...]
    x1, x2 = blk[:, :RET_DK // 2], blk[:, RET_DK // 2:]
    out = jnp.concatenate([x1 * c - x2 * s, x2 * c + x1 * s], axis=1)
    return (out * (RET_DK ** -0.5) if mode == "k" else out,)


EPI_COLS = 256


def _row_split_specs(tm, cols, split_at):
    first = pl.BlockSpec((tm, cols), lambda i, *_: (jnp.minimum(i, split_at - 1), 0))
    second = pl.BlockSpec((tm, cols), lambda i, *_: (jnp.maximum(i - split_at, 0), 0))
    return [first, second]


def _normmm_kernel(*refs, routes, ntab, nout, epi, split_at):
    nx = 1 if split_at is None else 2
    x_refs = refs[:nx]
    g_ref, w_ref = refs[nx:nx + 2]
    tabs = refs[nx + 2:nx + 2 + ntab]
    outs = refs[nx + 2 + ntab:nx + 2 + ntab + nout]
    xn_ref = refs[nx + 2 + ntab + nout]
    i = pl.program_id(0)
    j = pl.program_id(1)

    if split_at is None:
        @pl.when(j == 0)
        def _():
            xn_ref[...] = _rms(x_refs[0][...], g_ref[...]).astype(bf16)
    else:
        @pl.when(jnp.logical_and(j == 0, i < split_at))
        def _():
            xn_ref[...] = _rms(x_refs[0][...], g_ref[...]).astype(bf16)

        @pl.when(jnp.logical_and(j == 0, i >= split_at))
        def _():
            xn_ref[...] = _rms(x_refs[1][...], g_ref[...]).astype(bf16)

    groups = {}
    for jj, tile_routes in enumerate(routes):
        groups.setdefault(tuple(tile_routes), []).append(jj)
    for tile_routes, jjs in groups.items():
        cond = functools.reduce(jnp.logical_or, [j == jj for jj in jjs])

        @pl.when(cond)
        def _(tile_routes=tile_routes):
            for s, (mode, dests) in enumerate(tile_routes):
                acc = jnp.dot(xn_ref[...], w_ref[:, s * EPI_COLS:(s + 1) * EPI_COLS],
                              preferred_element_type=f32)
                for (oi, col), val in zip(dests, epi(acc, mode, tabs)):
                    outs[oi][:, col:col + val.shape[1]] = val.astype(outs[oi].dtype)


def _norm_matmul(x, g, w, tabs, routes, epi, out_shapes, out_specs, tm, name):
    xs = x if isinstance(x, tuple) else (x,)
    T = sum(p.shape[0] for p in xs)
    D = xs[0].shape[1]
    N = w.shape[1]
    ntile = len(routes)
    tn = N // ntile
    assert tn * ntile == N and tn == EPI_COLS * len(routes[0])
    if len(xs) == 1:
        split_at = None
        x_specs = [pl.BlockSpec((tm, D), lambda i, j: (i, 0))]
    else:
        assert xs[0].shape[0] % tm == 0 and xs[1].shape[0] % tm == 0
        split_at = xs[0].shape[0] // tm
        x_specs = _row_split_specs(tm, D, split_at)
    tab_specs = [pl.BlockSpec((tm, t.shape[1]), lambda i, j: (i, 0)) for t in tabs]
    return pl.pallas_call(
        functools.partial(_normmm_kernel, routes=routes, ntab=len(tabs), nout=len(out_shapes), epi=epi,
                          split_at=split_at),
        out_shape=tuple(out_shapes),
        grid=(T // tm, ntile),
        in_specs=x_specs + [pl.BlockSpec((1, D), lambda i, j: (0, 0)),
                            pl.BlockSpec((D, tn), lambda i, j: (0, j))] + tab_specs,
        out_specs=tuple(out_specs),
        scratch_shapes=[pltpu.VMEM((tm, D), bf16)],
        compiler_params=_params(("parallel", "arbitrary")),
        name=name,
    )(*xs, g, w, *tabs)


def _mmres_kernel(a_ref, w_ref, *rest, split_at):
    o_ref = rest[-1]
    prod = jnp.dot(a_ref[...], w_ref[...], preferred_element_type=f32)
    if split_at is None:
        o_ref[...] = rest[0][...] + prod
    else:
        i = pl.program_id(0)

        @pl.when(i < split_at)
        def _():
            o_ref[...] = rest[0][...] + prod

        @pl.when(i >= split_at)
        def _():
            o_ref[...] = rest[1][...] + prod


def _resident(shape):
    return pl.BlockSpec(shape, lambda *_: (0,) * len(shape), pipeline_mode=pl.Buffered(1))


def _matmul_residual(a, w, res, name):
    T, K = a.shape
    N = w.shape[1]
    rs = res if isinstance(res, tuple) else (res,)
    if len(rs) == 1:
        tm = _pick(T, (512, 256, 128, 64))
        split_at = None
        r_specs = [pl.BlockSpec((tm, N), lambda i: (i, 0))]
    else:
        tm = _pick(rs[1].shape[0], (512, 256, 128, 64))
        assert rs[0].shape[0] % tm == 0
        split_at = rs[0].shape[0] // tm
        r_specs = _row_split_specs(tm, N, split_at)
    return pl.pallas_call(
        functools.partial(_mmres_kernel, split_at=split_at),
        out_shape=jax.ShapeDtypeStruct((T, N), f32),
        grid=(T // tm,),
        in_specs=[pl.BlockSpec((tm, K), lambda i: (i, 0)), _resident((K, N))] + r_specs,
        out_specs=pl.BlockSpec((tm, N), lambda i: (i, 0)),
        compiler_params=_params(("parallel",)),
        name=name,
    )(a, w, *rs)


def _mmres2_kernel(ap_ref, as_ref, w_ref, r_ref, o_ref, *, np_blocks):
    i = pl.program_id(0)

    @pl.when(i < np_blocks)
    def _():
        o_ref[...] = r_ref[...] + jnp.dot(ap_ref[...], w_ref[...], preferred_element_type=f32)

    @pl.when(i >= np_blocks)
    def _():
        o_ref[...] = r_ref[...] + jnp.dot(as_ref[...], w_ref[...], preferred_element_type=f32)


def _matmul_residual2(a_p, a_s, w, res, name):
    tp, K = a_p.shape
    ts = a_s.shape[0]
    N = w.shape[1]
    tm = _pick(ts, (256, 128, 64))
    assert tp % tm == 0
    np_blocks = tp // tm
    return pl.pallas_call(
        functools.partial(_mmres2_kernel, np_blocks=np_blocks),
        out_shape=jax.ShapeDtypeStruct((tp + ts, N), f32),
        grid=((tp + ts) // tm,),
        in_specs=[pl.BlockSpec((tm, K), lambda i: (jnp.minimum(i, np_blocks - 1), 0)),
                  pl.BlockSpec((tm, K), lambda i: (jnp.maximum(i - np_blocks, 0), 0)),
                  _resident((K, N)),
                  pl.BlockSpec((tm, N), lambda i: (i, 0))],
        out_specs=pl.BlockSpec((tm, N), lambda i: (i, 0)),
        compiler_params=_params(("parallel",)),
        name=name,
    )(a_p, a_s, w, res)


def _mlp_kernel(x_ref, g_ref, wu_ref, wd_ref, gf_ref, *rest, nk, split_at):
    if split_at is None:
        acc_ref, xn_ref = rest
    else:
        op_ref, os_ref, xn_ref, acc_ref = rest
    i = pl.program_id(0)
    k = pl.program_id(1)

    @pl.when(k == 0)
    def _():
        x = x_ref[...]
        xn_ref[...] = _rms(x, g_ref[...]).astype(bf16)
        acc_ref[...] = x

    h = jnp.dot(xn_ref[...], wu_ref[...], preferred_element_type=f32)
    h = jnp.square(jnp.maximum(h, 0.0)).astype(bf16)
    acc_ref[...] += jnp.dot(h, wd_ref[...], preferred_element_type=f32)

    if split_at is not None:
        @pl.when(jnp.logical_and(k == nk - 1, i < split_at))
        def _():
            op_ref[...] = _rms(acc_ref[...], gf_ref[...])

        @pl.when(jnp.logical_and(k == nk - 1, i >= split_at))
        def _():
            os_ref[...] = _rms(acc_ref[...], gf_ref[...])


def _mlp(x, g, w_up, w_down, layer, g_final, split_rows, name):
    T, D = x.shape
    F = w_up.shape[2]
    tf = _pick(F, (1024, 512))
    nk = F // tf
    if split_rows is None:
        tm = _pick(T, (512, 256, 128, 64))
        x_spec = pl.BlockSpec((tm, D), lambda i, k: (i, 0))
        split_at = None
        out_shape = jax.ShapeDtypeStruct((T, D), f32)
        out_specs = pl.BlockSpec((tm, D), lambda i, k: (i, 0))
        scratch = [pltpu.VMEM((tm, D), bf16)]
    else:
        tm = _pick(T - split_rows, (512, 256, 128, 64))
        x_spec = pl.BlockSpec((tm, D), lambda i, k: (i, 0))
        assert split_rows % tm == 0
        split_at = split_rows // tm
        out_shape = (jax.ShapeDtypeStruct((split_rows, D), f32), jax.ShapeDtypeStruct((T - split_rows, D), f32))
        out_specs = (pl.BlockSpec((tm, D), lambda i, k: (jnp.minimum(i, split_at - 1), 0)),
                     pl.BlockSpec((tm, D), lambda i, k: (jnp.maximum(i - split_at, 0), 0)))
        scratch = [pltpu.VMEM((tm, D), bf16), pltpu.VMEM((tm, D), f32)]
    return pl.pallas_call(
        functools.partial(_mlp_kernel, nk=nk, split_at=split_at),
        out_shape=out_shape,
        grid=(T // tm, nk),
        in_specs=[x_spec,
                  pl.BlockSpec((1, D), lambda i, k: (0, 0)),
                  pl.BlockSpec((None, D, tf), lambda i, k: (layer, 0, k)),
                  pl.BlockSpec((None, tf, D), lambda i, k: (layer, k, 0)),
                  pl.BlockSpec((1, D), lambda i, k: (0, 0))],
        out_specs=out_specs,
        scratch_shapes=scratch,
        compiler_params=_params(("arbitrary", "arbitrary")),
        name=name,
    )(x, g, w_up, w_down, g_final)


def _attn_kernel(sb_ref, q_ref, kp0_ref, kp1_ref, kc0_ref, kc1_ref, k2_ref,
                 vp0_ref, vp1_ref, vc0_ref, vc1_ref, v2_ref, o_ref, *, npc):
    c = pl.program_id(0)
    is_sample = c >= npc
    k0, k1, v0, v1 = (
        jnp.where(is_sample, cr[...], pr[...])
        for cr, pr in ((kc0_ref, kp0_ref), (kc1_ref, kp1_ref), (vc0_ref, vp0_ref), (vc1_ref, vp1_ref)))
    nkeys = 3 * CHUNK
    ncols = nkeys + CHUNK
    PW = 2 * HEAD_DIM
    pairs = ATTN_GROUP // 2
    col_blk = lax.broadcasted_iota(jnp.int32, (1, ncols), 1) // CHUNK
    valid = jnp.logical_or(jnp.logical_or(is_sample, c + col_blk >= 2), col_blk >= 3)
    bias = jnp.where(valid, 0.0, -jnp.inf).astype(f32)
    zpad = jnp.zeros((CHUNK, ATTN_KV_HEADS * PW), bf16)
    kcat = jnp.concatenate([k0, k1, k2_ref[...], zpad], axis=0)
    vcat = jnp.concatenate([v0, v1, v2_ref[...], zpad], axis=0)
    lo_half = lax.broadcasted_iota(jnp.int32, (1, PW), 1) < HEAD_DIM
    ones = jnp.ones((ncols, PW), bf16)
    kvs = range(ATTN_KV_HEADS)

    def stacked_q(kv):
        parts = []
        for p in range(pairs):
            col = (kv * pairs + p) * PW
            qp = q_ref[:, col:col + PW]
            parts += [jnp.where(lo_half, qp, 0.0), jnp.where(lo_half, 0.0, qp)]
        return jnp.concatenate(parts, axis=0).astype(bf16)

    s = [lax.dot_general(stacked_q(kv), kcat[:, kv * PW:(kv + 1) * PW], NT_DIMS,
                         preferred_element_type=f32) + sb_ref[kv] + bias for kv in kvs]
    p = [jnp.exp(s[kv] - jnp.max(s[kv], axis=-1, keepdims=True)).astype(bf16) for kv in kvs]
    oa = [jnp.dot(p[kv], jnp.concatenate([vcat[:, kv * PW:(kv + 1) * PW], ones], axis=1),
                  preferred_element_type=f32) for kv in kvs]
    for kv in kvs:
        on = oa[kv][:, :PW] / oa[kv][:, PW:]
        for pi in range(pairs):
            r0 = 2 * pi * CHUNK
            blk = jnp.where(lo_half, on[r0:r0 + CHUNK], on[r0 + CHUNK:r0 + 2 * CHUNK])
            col = (kv * pairs + pi) * PW
            o_ref[:, col:col + PW] = blk.astype(o_ref.dtype)


def _attention(q, kvdup, kcache, vcache, sinks, npc, nchunks):
    T = q.shape[0]
    qd = ATTN_HEADS * HEAD_DIM
    dw = kcache.shape[1]
    wblocks = WINDOW // CHUNK

    def new_spec(j, col):
        if j == 2:
            return pl.BlockSpec((CHUNK, dw), lambda c: (c, col))
        return pl.BlockSpec((CHUNK, dw), lambda c: (jnp.maximum(jnp.minimum(c, npc - 1) - 2 + j, 0), col))

    def cache_spec(j):
        return pl.BlockSpec((CHUNK, dw), lambda c: (wblocks * jnp.maximum(c - npc, 0) + j, 0))

    nkeys = 3 * CHUNK
    col = jnp.arange(nkeys + CHUNK)[None, None, :]
    sink_rows = jnp.repeat(sinks.astype(f32).reshape(ATTN_KV_HEADS, ATTN_GROUP), CHUNK, axis=1)[:, :, None]
    sink_bias = jnp.where(col < nkeys, 0.0, jnp.where(col == nkeys, sink_rows, -jnp.inf)).astype(f32)

    return pl.pallas_call(
        functools.partial(_attn_kernel, npc=npc),
        out_shape=jax.ShapeDtypeStruct((T, qd), bf16),
        grid=(nchunks,),
        in_specs=[pl.BlockSpec(sink_bias.shape, lambda c: (0, 0, 0)),
                  pl.BlockSpec((CHUNK, qd), lambda c: (c, 0)),
                  new_spec(0, 0), new_spec(1, 0), cache_spec(0), cache_spec(1), new_spec(2, 0),
                  new_spec(0, 1), new_spec(1, 1), cache_spec(0), cache_spec(1), new_spec(2, 1)],
        out_specs=pl.BlockSpec((CHUNK, qd), lambda c: (c, 0)),
        compiler_params=_params(("parallel",)),
        name="attn_core",
    )(sink_bias, q, kvdup, kvdup, kcache, kcache, kvdup, kvdup, kvdup, vcache, vcache, kvdup)


RET_HEADS_PER_STEP = 4


def _ret_kernel(lg_ref, q_ref, k_ref, v_ref, g_ref, s0_ref, gnw_ref, y_ref, sout_ref, S_ref, *, from_state):
    hg = pl.program_id(0)
    c = pl.program_id(1)
    L = q_ref.shape[0]
    nh = RET_HEADS_PER_STEP
    hs = range(nh)
    dot = functools.partial(jnp.dot, preferred_element_type=f32)

    if from_state:
        S_ref[...] = s0_ref[0]
    else:
        @pl.when(c == 0)
        def _():
            S_ref[...] = jnp.zeros_like(S_ref)

    lg = [lg_ref[hg * nh + h] for h in hs]
    q = [q_ref[:, h * RET_DK:(h + 1) * RET_DK] for h in hs]
    k = [k_ref[:, h * RET_DK:(h + 1) * RET_DK] for h in hs]
    v = [v_ref[:, h * RET_DV:(h + 1) * RET_DV] for h in hs]
    row = lax.broadcasted_iota(jnp.int32, (L, L), 0)
    col = lax.broadcasted_iota(jnp.int32, (L, L), 1)
    diff = (row - col).astype(f32)
    idx = lax.broadcasted_iota(jnp.int32, (L, 1), 0).astype(f32)
    decay = [jnp.where(diff >= 0, jnp.exp(lg[h] * jnp.maximum(diff, 0.0)), 0.0) for h in hs]
    xi = [jnp.exp(lg[h] * (idx + 1.0)) for h in hs]
    zeta = [jnp.exp(lg[h] * (L - 1.0 - idx)) for h in hs]

    S = [S_ref[h] for h in hs]
    scores = [lax.dot_general(q[h], k[h], NT_DIMS, preferred_element_type=f32) * decay[h] for h in hs]
    o = [dot(scores[h].astype(bf16), v[h])
         + dot((q[h].astype(f32) * xi[h]).astype(bf16), S[h].astype(bf16)) for h in hs]
    S_new = [jnp.exp(lg[h] * L) * S[h] + dot((k[h].astype(f32) * zeta[h]).T.astype(bf16), v[h])
             for h in hs]
    for h in hs:
        S_ref[h] = S_new[h]
        sout_ref[0, h] = S_new[h]

    cen = [o[h] - jnp.mean(o[h], axis=-1, keepdims=True) for h in hs]
    on = [cen[h] * lax.rsqrt(jnp.mean(cen[h] * cen[h], axis=-1, keepdims=True) + RET_GN_EPS) for h in hs]
    for h in hs:
        sl = slice(h * RET_DV, (h + 1) * RET_DV)
        y_ref[:, sl] = (jax.nn.silu(g_ref[:, sl].astype(f32)) * on[h] * gnw_ref[:, sl]).astype(y_ref.dtype)


def _retention(proj, state, gn_w, lg, row0, nrows, L, from_state, name):
    nsteps = nrows // L
    b0 = row0 // L
    assert nsteps * L == nrows and b0 * L == row0
    nseq = nsteps if from_state else 1
    G = RET_HEADS_PER_STEP
    kw, vw = G * RET_DK, G * RET_DV
    kb = RET_HEADS * RET_DK // kw
    vb = 2 * RET_HEADS * RET_DK // vw
    gb = vb + RET_HEADS * RET_DV // vw
    state_map = (lambda h, c: (c, h, 0, 0)) if from_state else (lambda h, c: (0, h, 0, 0))

    return pl.pallas_call(
        functools.partial(_ret_kernel, from_state=from_state),
        out_shape=(jax.ShapeDtypeStruct((nrows, RET_HEADS * RET_DV), bf16),
                   jax.ShapeDtypeStruct((nseq, RET_HEADS, RET_DK, RET_DV), f32)),
        grid=(RET_HEADS // G, nsteps),
        in_specs=[pl.BlockSpec(memory_space=pltpu.SMEM),
                  pl.BlockSpec((L, kw), lambda h, c: (b0 + c, h)),
                  pl.BlockSpec((L, kw), lambda h, c: (b0 + c, kb + h)),
                  pl.BlockSpec((L, vw), lambda h, c: (b0 + c, vb + h)),
                  pl.BlockSpec((L, vw), lambda h, c: (b0 + c, gb + h)),
                  pl.BlockSpec((1, G, RET_DK, RET_DV), state_map),
                  pl.BlockSpec((1, vw), lambda h, c: (0, h))],
        out_specs=(pl.BlockSpec((L, vw), lambda h, c: (c, h)),
                   pl.BlockSpec((1, G, RET_DK, RET_DV), state_map)),
        scratch_shapes=[pltpu.VMEM((G, RET_DK, RET_DV), f32)],
        compiler_params=_params(("parallel", "arbitrary")),
        name=name,
    )(lg, proj, proj, proj, proj, state, gn_w)


SUBLANES = 8


def _lerpmm_kernel(x_ref, xprev_ref, g_ref, start_ref, mu_ref, w_ref, o_ref, hlast_ref,
                   h_ref, xx_ref, l_ref, *, tiles_per_d, n_big, npc):
    i = pl.program_id(0)
    j = pl.program_id(1)
    tm = x_ref.shape[0]
    cpb = tm // CHUNK

    @pl.when(j == 0)
    def _():
        g = g_ref[...]
        h = _rms(x_ref[...], g)
        h_ref[...] = h
        xx_ref[...] = pltpu.roll(h, 1, 0) - h
        prev = _rms(xprev_ref[...], g)[SUBLANES - 1:SUBLANES, :]
        xx_ref[0:1, :] = prev - h[0:1, :]
        for ci in range(cpb):
            gc = i * cpb + ci
            r0 = ci * CHUNK

            @pl.when(jnp.logical_or(gc == 0, gc >= npc))
            def _(ci=ci, r0=r0):
                xx_ref[r0:r0 + 1, :] = start_ref[ci] - h_ref[r0:r0 + 1, :]

            hlast_ref[ci] = h[r0 + CHUNK - 1:r0 + CHUNK, :]

    @pl.when(jnp.logical_or(j % tiles_per_d == 0, j >= n_big))
    def _():
        l_ref[...] = (h_ref[...] + xx_ref[...] * mu_ref[0]).astype(bf16)

    o_ref[...] = jnp.dot(l_ref[...], w_ref[...], preferred_element_type=f32).astype(o_ref.dtype)


def _lerp_matmul(x, g, starts, mu_tiles, wcat, tn, tiles_per_d, n_big, npc):
    T, D = x.shape
    N = wcat.shape[1]
    tm = _pick(T, (1024, 512, 256, 128, 64))
    cpb = tm // CHUNK
    return pl.pallas_call(
        functools.partial(_lerpmm_kernel, tiles_per_d=tiles_per_d, n_big=n_big, npc=npc),
        out_shape=(jax.ShapeDtypeStruct((T, N), bf16),
                   jax.ShapeDtypeStruct((T // CHUNK, 1, D), f32)),
        grid=(T // tm, N // tn),
        in_specs=[pl.BlockSpec((tm, D), lambda i, j: (i, 0)),
                  pl.BlockSpec((SUBLANES, D), lambda i, j: (jnp.maximum(i * (tm // SUBLANES) - 1, 0), 0)),
                  pl.BlockSpec((1, D), lambda i, j: (0, 0)),
                  pl.BlockSpec((cpb, 1, D), lambda i, j: (i, 0, 0)),
                  pl.BlockSpec((1, 1, D), lambda i, j: (j, 0, 0)),
                  pl.BlockSpec((D, tn), lambda i, j: (0, j))],
        out_specs=(pl.BlockSpec((tm, tn), lambda i, j: (i, j)),
                   pl.BlockSpec((cpb, 1, D), lambda i, j: (i, 0, 0))),
        scratch_shapes=[pltpu.VMEM((tm, D), f32), pltpu.VMEM((tm, D), f32), pltpu.VMEM((tm, D), bf16)],
        compiler_params=_params(("parallel", "arbitrary")),
        name="rwkv_proj",
    )(x, x, g, starts, mu_tiles, wcat)


DECAY_SCALE = 0.6065306597126334
LANES = 128


SEG_TILE = 256


def _head_sum(x):
    rows, D = x.shape
    r = lax.broadcasted_iota(jnp.int32, (SEG_TILE, SEG_TILE), 0) // RWKV_HS
    c = lax.broadcasted_iota(jnp.int32, (SEG_TILE, SEG_TILE), 1) // RWKV_HS
    ones = (r == c).astype(bf16)
    out = [jnp.dot(x[:, j * SEG_TILE:(j + 1) * SEG_TILE].astype(bf16), ones, preferred_element_type=f32)
           for j in range(D // SEG_TILE)]
    return jnp.concatenate(out, axis=1)


def _chunk_cumsum(x):
    rows = x.shape[0]
    row = lax.broadcasted_iota(jnp.int32, (rows, rows), 0)
    col = lax.broadcasted_iota(jnp.int32, (rows, rows), 1)
    tri = jnp.logical_and((row // CHUNK) == (col // CHUNK), row >= col).astype(bf16)
    hi = x.astype(bf16)
    rest = x - hi.astype(f32)
    mid = rest.astype(bf16)
    lo = (rest - mid.astype(f32)).astype(bf16)
    dot = functools.partial(jnp.dot, preferred_element_type=f32)
    return dot(tri, hi) + dot(tri, mid) + dot(tri, lo)


def _rwkv_prep_kernel(p_ref, r_ref, k_ref, v_ref, w2_ref, a2_ref, g2_ref, w0_ref, a0_ref,
                      kk_ref, ka_ref, rk_ref, lnw_ref, lnb_ref,
                      at_ref, rt_ref, bt_ref, kt_ref, bh_ref, kh_ref, gl_ref, ga_ref, gb_ref):
    P = LORA_PAD
    tm = r_ref.shape[0]
    rw, ra, rg = w2_ref.shape[0], a2_ref.shape[0], g2_ref.shape[0]
    pw = jnp.tanh(p_ref[:, :rw].astype(f32)).astype(bf16)
    pa = p_ref[:, P:P + ra]
    pg = jax.nn.sigmoid(p_ref[:, 2 * P:2 * P + rg].astype(f32)).astype(bf16)
    wl = w0_ref[...] + jnp.dot(pw, w2_ref[...], preferred_element_type=f32)
    lw = -DECAY_SCALE * jax.nn.sigmoid(wl)
    a = jax.nn.sigmoid(a0_ref[...] + jnp.dot(pa, a2_ref[...], preferred_element_type=f32))
    g = jnp.dot(pg, g2_ref[...], preferred_element_type=f32)

    cum = _chunk_cumsum(lw)
    tot = jnp.concatenate(
        [jnp.broadcast_to(cum[(ci + 1) * CHUNK - 1:(ci + 1) * CHUNK, :], (CHUNK, cum.shape[1]))
         for ci in range(tm // CHUNK)], axis=0)

    r, k, v = r_ref[...].astype(f32), k_ref[...].astype(f32), v_ref[...].astype(f32)
    kk = k * kk_ref[...]
    kk = kk / jnp.maximum(jnp.sqrt(_head_sum(kk * kk)), 1e-12)
    kmod = k * (1.0 + (a - 1.0) * ka_ref[...])
    beta = kk * a
    e_neg = jnp.exp(-cum)
    e_end = jnp.exp(tot - cum)
    at_ref[...] = (-kk * jnp.exp(cum - lw)).astype(bf16)
    rt_ref[...] = (r * jnp.exp(cum)).astype(bf16)
    bt_ref[...] = (beta * e_neg).astype(bf16)
    kt_ref[...] = (kmod * e_neg).astype(bf16)
    bh_ref[...] = (beta * e_end).astype(bf16)
    kh_ref[...] = (kmod * e_end).astype(bf16)
    bonus = _head_sum(r * kmod * rk_ref[...]) * v
    ga_ref[...] = (lnw_ref[...] * g).astype(bf16)
    gb_ref[...] = ((lnb_ref[...] + bonus) * g).astype(bf16)
    for ci in range(tm // CHUNK):
        gl_ref[ci] = jnp.exp(tot[ci * CHUNK:ci * CHUNK + 1, :])


def _rwkv_prep(proj, lora_block, w2p, a2p, g2p, w0, a0, k_k, k_a, r_k, ln_w, ln_b):
    T = proj.shape[0]
    D = w2p.shape[1]
    P = LORA_PAD
    tm = _pick(T, (128, 64))
    vspec = pl.BlockSpec((1, D), lambda i: (0, 0))
    ospec = pl.BlockSpec((tm, D), lambda i: (i, 0))
    o16 = jax.ShapeDtypeStruct((T, D), bf16)
    return pl.pallas_call(
        _rwkv_prep_kernel,
        out_shape=(o16,) * 6 + (jax.ShapeDtypeStruct((T // CHUNK, 1, D), f32), o16, o16),
        grid=(T // tm,),
        in_specs=[pl.BlockSpec((tm, 3 * P), lambda i: (i, lora_block)),
                  pl.BlockSpec((tm, D), lambda i: (i, 0)),
                  pl.BlockSpec((tm, D), lambda i: (i, 1)),
                  pl.BlockSpec((tm, D), lambda i: (i, 2)),
                  _resident(w2p.shape), _resident(a2p.shape), _resident(g2p.shape)] + [vspec] * 7,
        out_specs=(ospec,) * 6 + (pl.BlockSpec((tm // CHUNK, 1, D), lambda i: (i, 0, 0)), ospec, ospec),
        compiler_params=_params(("parallel",)),
        name="rwkv_prep",
    )(proj, proj, proj, proj, w2p, a2p, g2p, w0, a0, k_k, k_a, r_k, ln_w, ln_b)


def _rwkv_heads_chunk(heads):
    L, N = heads[0][6].shape
    assert L == N
    nh = len(heads)
    rng = range(nh)
    dot = functools.partial(jnp.dot, preferred_element_type=f32)
    row = lax.broadcasted_iota(jnp.int32, (L, 2 * L), 0)
    lane = lax.broadcasted_iota(jnp.int32, (L, 2 * L), 1)
    lo = lane < L
    tok = jnp.where(lo, lane, lane - L)
    incl = row >= tok
    strict = row > tok
    lhs = [jnp.concatenate([h[0], h[1]], axis=0) for h in heads]
    rhs = [jnp.concatenate([h[2], h[3]], axis=0) for h in heads]
    G = [lax.dot_general(lhs[i], rhs[i], NT_DIMS, preferred_element_type=f32) for i in rng]
    LS = [lax.dot_general(lhs[i], heads[i][7].astype(bf16), NT_DIMS, preferred_element_type=f32)
          for i in rng]
    top = [jnp.where(strict, g[:L], 0.0) for g in G]
    a_r = [jnp.where(incl, g[L:], 0.0).astype(bf16) for g in G]
    v2 = [jnp.concatenate([h[6], h[6]], axis=0) for h in heads]
    zero = jnp.zeros((L, N), f32)
    x0 = [LS[i][:L] + dot(jnp.where(lo, 0.0, top[i]).astype(bf16), v2[i]) for i in rng]
    p = [jnp.where(lo, top[i], jnp.concatenate([zero, x0[i]], axis=1)) for i in rng]
    steps = max(1, (L - 1).bit_length())
    for s in range(steps):
        z = [dot(p[i][:, :L].astype(bf16), p[i].astype(bf16)) for i in rng]
        if s < steps - 1:
            p = [z[i] + jnp.where(lo, 0.0, p[i]) for i in rng]
        else:
            u = [(p[i] + z[i])[:, L:] for i in rng]
    uv = [jnp.concatenate([u[i], heads[i][6].astype(f32)], axis=0) for i in rng]
    o = [LS[i][L:] + dot(a_r[i], uv[i].astype(bf16)) for i in rng]
    S_new = [heads[i][7] * heads[i][8]
             + dot(uv[i].T.astype(bf16), jnp.concatenate([heads[i][4], heads[i][5]], axis=0))
             for i in rng]
    return list(zip(o, S_new))


def _rwkv_core_kernel(at_ref, rt_ref, bt_ref, kt_ref, bh_ref, kh_ref, v_ref, gl_ref, s0_ref,
                      ga_ref, gb_ref, y_ref, sp_ref, ss_ref, S_ref, *, npc, nhead):
    c = pl.program_id(1)
    N = RWKV_HS

    @pl.when(c == 0)
    def _():
        S_ref[...] = jnp.zeros_like(S_ref)

    @pl.when(c >= npc)
    def _():
        S_ref[...] = s0_ref[0]

    gl = gl_ref[0]
    heads = []
    for hh in range(nhead):
        sl = slice(hh * N, (hh + 1) * N)
        heads.append((at_ref[:, sl], rt_ref[:, sl], bt_ref[:, sl], kt_ref[:, sl], bh_ref[:, sl],
                      kh_ref[:, sl], v_ref[:, sl], S_ref[hh], gl[:, sl]))
    res = _rwkv_heads_chunk(heads)
    for hh in range(nhead):
        S_ref[hh] = res[hh][1]
    outs = [o for o, _ in res]
    cen = [o - jnp.mean(o, axis=-1, keepdims=True) for o in outs]
    nrm = [d * lax.rsqrt(jnp.mean(d * d, axis=-1, keepdims=True) + RWKV_GN_EPS) for d in cen]
    y_ref[...] = (jnp.concatenate(nrm, axis=1) * ga_ref[...].astype(f32)
                  + gb_ref[...].astype(f32)).astype(y_ref.dtype)

    @pl.when(c == npc - 1)
    def _():
        sp_ref[0] = S_ref[...]

    @pl.when(c >= npc)
    def _():
        ss_ref[0] = S_ref[...]


RWKV_HEADS_PER_STEP = 32


def _rwkv_core(ops, proj, v_block, gl, state, ga, gb, npc, nchunks):
    T, D = ga.shape
    N = RWKV_HS
    nh = D // N
    nb = state.shape[0]
    G = RWKV_HEADS_PER_STEP
    W = G * N
    tok = pl.BlockSpec((CHUNK, W), lambda p, c: (c, p))
    vtok = pl.BlockSpec((CHUNK, W), lambda p, c: (c, v_block * (D // W) + p))

    sample_state = pl.BlockSpec((1, G, N, N), lambda p, c: (jnp.maximum(c - npc, 0), p, 0, 0))

    return pl.pallas_call(
        functools.partial(_rwkv_core_kernel, npc=npc, nhead=G),
        out_shape=(jax.ShapeDtypeStruct((T, D), bf16),
                   jax.ShapeDtypeStruct((1, nh, N, N), f32),
                   jax.ShapeDtypeStruct((nb, nh, N, N), f32)),
        grid=(nh // G, nchunks),
        in_specs=[tok] * 6 + [vtok, pl.BlockSpec((1, 1, W), lambda p, c: (c, 0, p)),
                              sample_state, tok, tok],
        out_specs=(tok, pl.BlockSpec((1, G, N, N), lambda p, c: (0, p, 0, 0)), sample_state),
        scratch_shapes=[pltpu.VMEM((G, N, N), f32)],
        compiler_params=_params(("parallel", "arbitrary")),
        name="rwkv_core",
    )(*ops, proj, gl, state, ga, gb)


def _positions(n_prompt, n_b, n_s):
    return jnp.concatenate([jnp.arange(n_prompt), jnp.tile(PAST_LEN + jnp.arange(n_s), n_b)]).astype(f32)


def _attn_tables(pos):
    half = ROT_DIM // 2
    inv = ROPE_THETA ** (-jnp.arange(half, dtype=f32) / half)
    ang = pos[:, None] * inv[None, :]
    cos, sin = jnp.cos(ang), jnp.sin(ang)
    T = pos.shape[0]
    zeros = jnp.zeros((T, half), f32)
    rest = HEAD_DIM - ROT_DIM
    c = jnp.concatenate([cos, cos, jnp.ones((T, rest), f32)], axis=1)
    s1 = jnp.concatenate([zeros, sin, jnp.zeros((T, rest), f32)], axis=1)
    s2 = jnp.concatenate([-sin, zeros, jnp.zeros((T, rest), f32)], axis=1)
    return tuple(jnp.tile(t, (1, 2)) for t in (c, s1, s2))


def _ret_tables(pos):
    half = RET_DK // 2
    inv = RET_THETA ** (-jnp.arange(half, dtype=f32) / half)
    ang = pos[:, None] * inv[None, :]
    return jnp.cos(ang), jnp.sin(ang)


def _pad_cols(w, n):
    return jnp.pad(w, ((0, 0), (0, n - w.shape[1])))


def _pad_rows_to_lanes(w):
    n = -(-w.shape[0] // LANES) * LANES
    return jnp.pad(w, ((0, n - w.shape[0]), (0, 0)))


def kernel(x_prompt, x_sample, cache_attn_k, cache_attn_v, state_ret, state_rwkv, state_rwkv_shift,
           norm_mix, norm_mlp, norm_final,
           attn_w_qkv, attn_sinks, attn_w_o,
           ret_w_in, ret_gn_w, ret_w_o,
           rwkv_mu, rwkv_w_rkv, rwkv_w_o, rwkv_w0, rwkv_w1, rwkv_w2, rwkv_a0, rwkv_a1, rwkv_a2,
           rwkv_g1, rwkv_g2, rwkv_k_k, rwkv_k_a, rwkv_r_k, rwkv_ln_w, rwkv_ln_b,
           mlp_w_up, mlp_w_down):
    bp, tp, D = x_prompt.shape
    nb, ns, _ = x_sample.shape
    assert bp == 1 and ns == CHUNK and tp % CHUNK == 0
    depth = norm_mix.shape[0]
    npc = tp // CHUNK
    nchunks = npc + nb
    T = tp + nb * ns
    qd = ATTN_HEADS * HEAD_DIM
    kvd = ATTN_KV_HEADS * HEAD_DIM

    x = (x_prompt.reshape(tp, D), x_sample.reshape(nb * ns, D))
    pos = _positions(tp, nb, ns)
    attn_tabs = _attn_tables(pos)
    ret_tabs = _ret_tables(pos)
    lg = jnp.log1p(-jnp.exp2(-5.0 - jnp.arange(RET_HEADS, dtype=f32)))
    tm_proj = _pick(T, (512, 256, 128, 64))
    tm_ret = _pick(T, (1024, 512, 256, 128, 64))
    w_up16 = mlp_w_up.astype(bf16)
    w_down16 = mlp_w_down.astype(bf16)

    kp_l, vp_l, ks_l, vs_l, rp_l, rs_l, wp_l, ws_l, shp_l, shs_l = ([] for _ in range(10))
    for i in range(depth):
        j, kind = divmod(i, 3)
        g_mix = norm_mix[i][None, :]
        if kind == 0:
            nsub = (qd + 2 * kvd) // (2 * EPI_COLS)
            routes = [[("q", ((0, s * EPI_COLS),)) for s in range(nsub)],
                      [("q", ((0, s * EPI_COLS),)) for s in range(nsub, qd // EPI_COLS)]
                      + [("k", ((1, 0), (2, 0))), ("v", ((1, kvd), (2, 2 * kvd)))]]
            q, kv, kvdup = _norm_matmul(
                x, g_mix, attn_w_qkv[j].astype(bf16), attn_tabs, routes, _epi_attn,
                [jax.ShapeDtypeStruct((T, qd), bf16), jax.ShapeDtypeStruct((T, 2 * kvd), f32),
                 jax.ShapeDtypeStruct((T, 4 * kvd), bf16)],
                [pl.BlockSpec((tm_proj, qd), lambda i, jj: (i, 0)),
                 pl.BlockSpec((tm_proj, 2 * kvd), lambda i, jj: (i, 0)),
                 pl.BlockSpec((tm_proj, 4 * kvd), lambda i, jj: (i, 0))],
                tm_proj, "attn_qkv")
            k_new = kv[:, :kvd]
            v_new = kv[:, kvd:]

            def cache_dup(cache):
                c5 = cache.astype(bf16).reshape(nb * WINDOW, ATTN_KV_HEADS, 1, HEAD_DIM)
                return jnp.broadcast_to(c5, (nb * WINDOW, ATTN_KV_HEADS, 2, HEAD_DIM)).reshape(nb * WINDOW, 2 * kvd)

            def new_window(new, cache):
                return jnp.concatenate([cache.reshape(nb, WINDOW, kvd), new[tp:].reshape(nb, ns, kvd)], axis=1)

            o = _attention(q, kvdup, cache_dup(cache_attn_k[j]), cache_dup(cache_attn_v[j]),
                           attn_sinks[j], npc, nchunks)
            k_samp = new_window(k_new, cache_attn_k[j])
            v_samp = new_window(v_new, cache_attn_v[j])
            x = _matmul_residual(o, attn_w_o[j].astype(bf16), x, "attn_out")
            kp_l.append(k_new[tp - WINDOW:tp].reshape(1, WINDOW, ATTN_KV_HEADS, HEAD_DIM))
            vp_l.append(v_new[tp - WINDOW:tp].reshape(1, WINDOW, ATTN_KV_HEADS, HEAD_DIM))
            ks_l.append(k_samp[:, -WINDOW:].reshape(nb, WINDOW, ATTN_KV_HEADS, HEAD_DIM))
            vs_l.append(v_samp[:, -WINDOW:].reshape(nb, WINDOW, ATTN_KV_HEADS, HEAD_DIM))
        elif kind == 1:
            nq = RET_HEADS * RET_DK // RET_TILE
            nv = RET_HEADS * RET_DV // RET_TILE
            tile = lambda mode: [(mode, ((0, s * EPI_COLS),)) for s in range(RET_TILE // EPI_COLS)]
            routes = [tile("q")] * nq + [tile("k")] * nq + [tile("plain")] * (2 * nv)
            (proj,) = _norm_matmul(
                x, g_mix, ret_w_in[j].astype(bf16), ret_tabs, routes, _epi_ret,
                [jax.ShapeDtypeStruct((T, ret_w_in.shape[2]), bf16)],
                [pl.BlockSpec((tm_ret, RET_TILE), lambda i, jj: (i, jj))], tm_ret, "ret_proj")
            gnw = ret_gn_w[j][None, :]
            lp = _pick(tp, (256, 128, 64))
            y_p, s_p = _retention(proj, state_ret[j], gnw, lg, 0, tp, lp, False, "ret_prompt")
            y_s, s_s = _retention(proj, state_ret[j], gnw, lg, tp, nb * ns, ns, True, "ret_sample")
            x = _matmul_residual2(y_p, y_s, ret_w_o[j].astype(bf16), x, "ret_out")
            rp_l.append(s_p)
            rs_l.append(s_s)
        else:
            starts = jnp.concatenate([jnp.zeros((npc, D), f32), state_rwkv_shift[j]], axis=0)[:, None, :]
            P = LORA_PAD
            tn = LORA_PAD
            wcat = jnp.concatenate(
                [rwkv_w_rkv[j][0], rwkv_w_rkv[j][1], rwkv_w_rkv[j][2],
                 _pad_cols(rwkv_w1[j], P), _pad_cols(rwkv_a1[j], P), _pad_cols(rwkv_g1[j], P)],
                axis=1).astype(bf16)
            tiles_per_d = D // tn
            mu = rwkv_mu[j]
            mu_tiles = jnp.concatenate(
                [jnp.repeat(mu[jnp.array([0, 2, 3])], tiles_per_d, axis=0), mu[jnp.array([1, 4, 5])]],
                axis=0)[:, None, :]
            proj, h_last = _lerp_matmul(x, g_mix, starts, mu_tiles, wcat, tn, tiles_per_d,
                                        3 * tiles_per_d, npc)
            prep = _rwkv_prep(
                proj, 3 * D // (3 * P),
                _pad_rows_to_lanes(rwkv_w2[j]).astype(bf16), _pad_rows_to_lanes(rwkv_a2[j]).astype(bf16),
                _pad_rows_to_lanes(rwkv_g2[j]).astype(bf16), rwkv_w0[j][None, :], rwkv_a0[j][None, :],
                rwkv_k_k[j][None, :], rwkv_k_a[j][None, :], rwkv_r_k[j].reshape(1, D),
                rwkv_ln_w[j][None, :], rwkv_ln_b[j][None, :])
            ops, gl, ga, gb = prep[:6], prep[6], prep[7], prep[8]
            y, s_p, s_s = _rwkv_core(ops, proj, 2, gl, state_rwkv[j], ga, gb, npc, nchunks)
            x = _matmul_residual(y, rwkv_w_o[j].astype(bf16), x, "rwkv_out")
            wp_l.append(s_p)
            ws_l.append(s_s)
            shp_l.append(h_last[npc - 1])
            shs_l.append(h_last[npc:, 0])
        x = _mlp(x, norm_mlp[i][None, :], w_up16, w_down16, i, norm_final[None, :],
                 tp if i == depth - 1 else None, "mlp")

    y_prompt = x[0].reshape(1, tp, D)
    y_sample = x[1].reshape(nb, ns, D)
    return (y_prompt, y_sample,
            jnp.stack(kp_l), jnp.stack(vp_l), jnp.stack(ks_l), jnp.stack(vs_l),
            jnp.stack(rp_l), jnp.stack(rs_l),
            jnp.stack(wp_l), jnp.stack(ws_l), jnp.stack(shp_l), jnp.stack(shs_l))
```

```python
import functools

import jax
import jax.numpy as jnp
from jax import lax
from jax.experimental import pallas as pl
from jax.experimental.pallas import tpu as pltpu

f32 = jnp.float32
bf16 = jnp.bfloat16

CHUNK = 64
NORM_EPS = 1e-5
PAST_LEN = 4096

ATTN_HEADS = 32
ATTN_KV_HEADS = 4
ATTN_GROUP = ATTN_HEADS // ATTN_KV_HEADS
HEAD_DIM = 64
WINDOW = 128
ROT_DIM = HEAD_DIM // 4
ROPE_THETA = 500000.0

RET_HEADS = 8
RET_DK = 256
RET_DV = 512
RET_THETA = 10000.0
RET_GN_EPS = 1e-5
RET_TILE = 1024

RWKV_HS = 64
RWKV_GN_EPS = 64e-5
LORA_PAD = 512

VMEM_LIMIT = 52 * 1024 * 1024

NT_DIMS = (((1,), (1,)), ((), ()))


def _pick(n, cands):
    for c in cands:
        if n % c == 0:
            return c
    raise ValueError(f"no tile for {n} in {cands}")


def _params(sem):
    return pltpu.CompilerParams(dimension_semantics=sem, vmem_limit_bytes=VMEM_LIMIT)


def _rms(x, g):
    return x * lax.rsqrt(jnp.mean(x * x, axis=-1, keepdims=True) + NORM_EPS) * g


def _dup_heads(x):
    pw = 2 * HEAD_DIM
    lo = lax.broadcasted_iota(jnp.int32, (1, pw), 1) < HEAD_DIM
    out = []
    for c in range(x.shape[1] // pw):
        xc = x[:, c * pw:(c + 1) * pw]
        sw = pltpu.roll(xc, HEAD_DIM, 1)
        out += [jnp.where(lo, xc, sw), jnp.where(lo, sw, xc)]
    return jnp.concatenate(out, axis=1)


def _epi_attn(blk, mode, tabs):
    if mode == "v":
        return blk, _dup_heads(blk)
    c, s1, s2 = (t[...] for t in tabs)
    pw, half = c.shape[1], ROT_DIM // 2
    out = jnp.concatenate(
        [xc * c + pltpu.roll(xc, half, 1) * s1 + pltpu.roll(xc, pw - half, 1) * s2
         for xc in (blk[:, i * pw:(i + 1) * pw] for i in range(blk.shape[1] // pw))], axis=1)
    if mode == "q":
        return (out * (HEAD_DIM ** -0.5),)
    return out, _dup_heads(out)


def _epi_ret(blk, mode, tabs):
    if mode == "plain":
        return (blk,)
    cos_ref, sin_ref = tabs
    c, s = cos_ref[...], sin_ref[...]
    x1, x2 = blk[:, :RET_DK // 2], blk[:, RET_DK // 2:]
    out = jnp.concatenate([x1 * c - x2 * s, x2 * c + x1 * s], axis=1)
    return (out * (RET_DK ** -0.5) if mode == "k" else out,)


EPI_COLS = 256


def _row_split_specs(tm, cols, split_at):
    first = pl.BlockSpec((tm, cols), lambda i, *_: (jnp.minimum(i, split_at - 1), 0))
    second = pl.BlockSpec((tm, cols), lambda i, *_: (jnp.maximum(i - split_at, 0), 0))
    return [first, second]


def _normmm_kernel(*refs, routes, ntab, nout, epi, split_at):
    nx = 1 if split_at is None else 2
    x_refs = refs[:nx]
    g_ref, w_ref = refs[nx:nx + 2]
    tabs = refs[nx + 2:nx + 2 + ntab]
    outs = refs[nx + 2 + ntab:nx + 2 + ntab + nout]
    xn_ref = refs[nx + 2 + ntab + nout]
    i = pl.program_id(0)
    j = pl.program_id(1)

    if split_at is None:
        @pl.when(j == 0)
        def _():
            xn_ref[...] = _rms(x_refs[0][...], g_ref[...]).astype(bf16)
    else:
        @pl.when(jnp.logical_and(j == 0, i < split_at))
        def _():
            xn_ref[...] = _rms(x_refs[0][...], g_ref[...]).astype(bf16)

        @pl.when(jnp.logical_and(j == 0, i >= split_at))
        def _():
            xn_ref[...] = _rms(x_refs[1][...], g_ref[...]).astype(bf16)

    groups = {}
    for jj, tile_routes in enumerate(routes):
        groups.setdefault(tuple(tile_routes), []).append(jj)
    for tile_routes, jjs in groups.items():
        cond = functools.reduce(jnp.logical_or, [j == jj for jj in jjs])

        @pl.when(cond)
        def _(tile_routes=tile_routes):
            for s, (mode, dests) in enumerate(tile_routes):
                acc = jnp.dot(xn_ref[...], w_ref[:, s * EPI_COLS:(s + 1) * EPI_COLS],
                              preferred_element_type=f32)
                for (oi, col), val in zip(dests, epi(acc, mode, tabs)):
                    outs[oi][:, col:col + val.shape[1]] = val.astype(outs[oi].dtype)


def _norm_matmul(x, g, w, tabs, routes, epi, out_shapes, out_specs, tm, name):
    xs = x if isinstance(x, tuple) else (x,)
    T = sum(p.shape[0] for p in xs)
    D = xs[0].shape[1]
    N = w.shape[1]
    ntile = len(routes)
    tn = N // ntile
    assert tn * ntile == N and tn == EPI_COLS * len(routes[0])
    if len(xs) == 1:
        split_at = None
        x_specs = [pl.BlockSpec((tm, D), lambda i, j: (i, 0))]
    else:
        assert xs[0].shape[0] % tm == 0 and xs[1].shape[0] % tm == 0
        split_at = xs[0].shape[0] // tm
        x_specs = _row_split_specs(tm, D, split_at)
    tab_specs = [pl.BlockSpec((tm, t.shape[1]), lambda i, j: (i, 0)) for t in tabs]
    return pl.pallas_call(
        functools.partial(_normmm_kernel, routes=routes, ntab=len(tabs), nout=len(out_shapes), epi=epi,
                          split_at=split_at),
        out_shape=tuple(out_shapes),
        grid=(T // tm, ntile),
        in_specs=x_specs + [pl.BlockSpec((1, D), lambda i, j: (0, 0)),
                            pl.BlockSpec((D, tn), lambda i, j: (0, j))] + tab_specs,
        out_specs=tuple(out_specs),
        scratch_shapes=[pltpu.VMEM((tm, D), bf16)],
        compiler_params=_params(("parallel", "arbitrary")),
        name=name,
    )(*xs, g, w, *tabs)


def _mmres_kernel(a_ref, w_ref, *rest, split_at):
    o_ref = rest[-1]
    prod = jnp.dot(a_ref[...], w_ref[...], preferred_element_type=f32)
    if split_at is None:
        o_ref[...] = rest[0][...] + prod
    else:
        i = pl.program_id(0)

        @pl.when(i < split_at)
        def _():
            o_ref[...] = rest[0][...] + prod

        @pl.when(i >= split_at)
        def _():
            o_ref[...] = rest[1][...] + prod


def _resident(shape):
    return pl.BlockSpec(shape, lambda *_: (0,) * len(shape), pipeline_mode=pl.Buffered(1))


def _matmul_residual(a, w, res, name):
    T, K = a.shape
    N = w.shape[1]
    rs = res if isinstance(res, tuple) else (res,)
    if len(rs) == 1:
        tm = _pick(T, (512, 256, 128, 64))
        split_at = None
        r_specs = [pl.BlockSpec((tm, N), lambda i: (i, 0))]
    else:
        tm = _pick(rs[1].shape[0], (512, 256, 128, 64))
        assert rs[0].shape[0] % tm == 0
        split_at = rs[0].shape[0] // tm
        r_specs = _row_split_specs(tm, N, split_at)
    return pl.pallas_call(
        functools.partial(_mmres_kernel, split_at=split_at),
        out_shape=jax.ShapeDtypeStruct((T, N), f32),
        grid=(T // tm,),
        in_specs=[pl.BlockSpec((tm, K), lambda i: (i, 0)), _resident((K, N))] + r_specs,
        out_specs=pl.BlockSpec((tm, N), lambda i: (i, 0)),
        compiler_params=_params(("parallel",)),
        name=name,
    )(a, w, *rs)


def _mmres2_kernel(ap_ref, as_ref, w_ref, r_ref, o_ref, *, np_blocks):
    i = pl.program_id(0)

    @pl.when(i < np_blocks)
    def _():
        o_ref[...] = r_ref[...] + jnp.dot(ap_ref[...], w_ref[...], preferred_element_type=f32)

    @pl.when(i >= np_blocks)
    def _():
        o_ref[...] = r_ref[...] + jnp.dot(as_ref[...], w_ref[...], preferred_element_type=f32)


def _matmul_residual2(a_p, a_s, w, res, name):
    tp, K = a_p.shape
    ts = a_s.shape[0]
    N = w.shape[1]
    tm = _pick(ts, (256, 128, 64))
    assert tp % tm == 0
    np_blocks = tp // tm
    return pl.pallas_call(
        functools.partial(_mmres2_kernel, np_blocks=np_blocks),
        out_shape=jax.ShapeDtypeStruct((tp + ts, N), f32),
        grid=((tp + ts) // tm,),
        in_specs=[pl.BlockSpec((tm, K), lambda i: (jnp.minimum(i, np_blocks - 1), 0)),
                  pl.BlockSpec((tm, K), lambda i: (jnp.maximum(i - np_blocks, 0), 0)),
                  _resident((K, N)),
                  pl.BlockSpec((tm, N), lambda i: (i, 0))],
        out_specs=pl.BlockSpec((tm, N), lambda i: (i, 0)),
        compiler_params=_params(("parallel",)),
        name=name,
    )(a_p, a_s, w, res)


def _mlp_kernel(x_ref, g_ref, wu_ref, wd_ref, gf_ref, *rest, nk, split_at):
    if split_at is None:
        acc_ref, xn_ref = rest
    else:
        op_ref, os_ref, xn_ref, acc_ref = rest
    i = pl.program_id(0)
    k = pl.program_id(1)

    @pl.when(k == 0)
    def _():
        x = x_ref[...]
        xn_ref[...] = _rms(x, g_ref[...]).astype(bf16)
        acc_ref[...] = x

    h = jnp.dot(xn_ref[...], wu_ref[...], preferred_element_type=f32)
    h = jnp.square(jnp.maximum(h, 0.0)).astype(bf16)
    acc_ref[...] += jnp.dot(h, wd_ref[...], preferred_element_type=f32)

    if split_at is not None:
        @pl.when(jnp.logical_and(k == nk - 1, i < split_at))
        def _():
            op_ref[...] = _rms(acc_ref[...], gf_ref[...])

        @pl.when(jnp.logical_and(k == nk - 1, i >= split_at))
        def _():
            os_ref[...] = _rms(acc_ref[...], gf_ref[...])


def _mlp(x, g, w_up, w_down, layer, g_final, split_rows, name):
    T, D = x.shape
    F = w_up.shape[2]
    tf = _pick(F, (1024, 512))
    nk = F // tf
    if split_rows is None:
        tm = _pick(T, (512, 256, 128, 64))
        x_spec = pl.BlockSpec((tm, D), lambda i, k: (i, 0))
        split_at = None
        out_shape = jax.ShapeDtypeStruct((T, D), f32)
        out_specs = pl.BlockSpec((tm, D), lambda i, k: (i, 0))
        scratch = [pltpu.VMEM((tm, D), bf16)]
    else:
        tm = _pick(T - split_rows, (512, 256, 128, 64))
        x_spec = pl.BlockSpec((tm, D), lambda i, k: (i, 0))
        assert split_rows % tm == 0
        split_at = split_rows // tm
        out_shape = (jax.ShapeDtypeStruct((split_rows, D), f32), jax.ShapeDtypeStruct((T - split_rows, D), f32))
        out_specs = (pl.BlockSpec((tm, D), lambda i, k: (jnp.minimum(i, split_at - 1), 0)),
                     pl.BlockSpec((tm, D), lambda i, k: (jnp.maximum(i - split_at, 0), 0)))
        scratch = [pltpu.VMEM((tm, D), bf16), pltpu.VMEM((tm, D), f32)]
    return pl.pallas_call(
        functools.partial(_mlp_kernel, nk=nk, split_at=split_at),
        out_shape=out_shape,
        grid=(T // tm, nk),
        in_specs=[x_spec,
                  pl.BlockSpec((1, D), lambda i, k: (0, 0)),
                  pl.BlockSpec((None, D, tf), lambda i, k: (layer, 0, k)),
                  pl.BlockSpec((None, tf, D), lambda i, k: (layer, k, 0)),
                  pl.BlockSpec((1, D), lambda i, k: (0, 0))],
        out_specs=out_specs,
        scratch_shapes=scratch,
        compiler_params=_params(("arbitrary", "arbitrary")),
        name=name,
    )(x, g, w_up, w_down, g_final)


def _attn_kernel(sb_ref, q_ref, kp0_ref, kp1_ref, kc0_ref, kc1_ref, k2_ref,
                 vp0_ref, vp1_ref, vc0_ref, vc1_ref, v2_ref, o_ref, *, npc):
    c = pl.program_id(0)
    is_sample = c >= npc
    k0, k1, v0, v1 = (
        jnp.where(is_sample, cr[...], pr[...])
        for cr, pr in ((kc0_ref, kp0_ref), (kc1_ref, kp1_ref), (vc0_ref, vp0_ref), (vc1_ref, vp1_ref)))
    nkeys = 3 * CHUNK
    ncols = nkeys + CHUNK
    PW = 2 * HEAD_DIM
    pairs = ATTN_GROUP // 2
    zpad = jnp.zeros((CHUNK, ATTN_KV_HEADS * PW), bf16)
    kcat = jnp.concatenate([k0, k1, k2_ref[...], zpad], axis=0)
    vcat = jnp.concatenate([v0, v1, v2_ref[...], zpad], axis=0)
    lo_half = lax.broadcasted_iota(jnp.int32, (1, PW), 1) < HEAD_DIM
    ones = jnp.ones((ncols, PW), bf16)
    kvs = range(ATTN_KV_HEADS)

    def stacked_q(kv):
        parts = []
        for p in range(pairs):
            col = (kv * pairs + p) * PW
            qp = q_ref[:, col:col + PW]
            parts += [jnp.where(lo_half, qp, 0.0), jnp.where(lo_half, 0.0, qp)]
        return jnp.concatenate(parts, axis=0).astype(bf16)

    s = [lax.dot_general(stacked_q(kv), kcat[:, kv * PW:(kv + 1) * PW], NT_DIMS,
                         preferred_element_type=f32) + sb_ref[0, kv] for kv in kvs]
    p = [jnp.exp(s[kv] - jnp.max(s[kv], axis=-1, keepdims=True)).astype(bf16) for kv in kvs]
    oa = [jnp.dot(p[kv], jnp.concatenate([vcat[:, kv * PW:(kv + 1) * PW], ones], axis=1),
                  preferred_element_type=f32) for kv in kvs]
    for kv in kvs:
        on = oa[kv][:, :PW] / oa[kv][:, PW:]
        for pi in range(pairs):
            r0 = 2 * pi * CHUNK
            blk = jnp.where(lo_half, on[r0:r0 + CHUNK], on[r0 + CHUNK:r0 + 2 * CHUNK])
            col = (kv * pairs + pi) * PW
            o_ref[:, col:col + PW] = blk.astype(o_ref.dtype)


def _attention(q, kvdup, kcache, vcache, sinks, npc, nchunks):
    T = q.shape[0]
    qd = ATTN_HEADS * HEAD_DIM
    dw = kcache.shape[1]
    wblocks = WINDOW // CHUNK

    def new_spec(j, col):
        if j == 2:
            return pl.BlockSpec((CHUNK, dw), lambda c: (c, col))
        return pl.BlockSpec((CHUNK, dw), lambda c: (jnp.maximum(jnp.minimum(c, npc - 1) - 2 + j, 0), col))

    def cache_spec(j):
        return pl.BlockSpec((CHUNK, dw), lambda c: (wblocks * jnp.maximum(c - npc, 0) + j, 0))

    nkeys = 3 * CHUNK
    col = jnp.arange(nkeys + CHUNK)[None, None, None, :]
    missing = (2 - jnp.arange(3))[:, None, None, None] * CHUNK
    sink_rows = jnp.repeat(sinks.astype(f32).reshape(ATTN_KV_HEADS, ATTN_GROUP), CHUNK, axis=1)[None, :, :, None]
    sink_bias = jnp.where(col < missing, -jnp.inf,
                          jnp.where(col < nkeys, 0.0, jnp.where(col == nkeys, sink_rows, -jnp.inf))).astype(f32)

    return pl.pallas_call(
        functools.partial(_attn_kernel, npc=npc),
        out_shape=jax.ShapeDtypeStruct((T, qd), bf16),
        grid=(nchunks,),
        in_specs=[pl.BlockSpec((1,) + sink_bias.shape[1:],
                               lambda c: (jnp.where(c < npc, jnp.minimum(c, 2), 2), 0, 0, 0)),
                  pl.BlockSpec((CHUNK, qd), lambda c: (c, 0)),
                  new_spec(0, 0), new_spec(1, 0), cache_spec(0), cache_spec(1), new_spec(2, 0),
                  new_spec(0, 1), new_spec(1, 1), cache_spec(0), cache_spec(1), new_spec(2, 1)],
        out_specs=pl.BlockSpec((CHUNK, qd), lambda c: (c, 0)),
        compiler_params=_params(("parallel",)),
        name="attn_core",
    )(sink_bias, q, kvdup, kvdup, kcache, kcache, kvdup, kvdup, kvdup, vcache, vcache, kvdup)


RET_HEADS_PER_STEP = 4


def _ret_kernel(lg_ref, q_ref, k_ref, v_ref, g_ref, s0_ref, gnw_ref, y_ref, sout_ref, S_ref, *, from_state):
    hg = pl.program_id(0)
    c = pl.program_id(1)
    L = q_ref.shape[0]
    nh = RET_HEADS_PER_STEP
    hs = range(nh)
    dot = functools.partial(jnp.dot, preferred_element_type=f32)

    if from_state:
        S_ref[...] = s0_ref[0]
    else:
        @pl.when(c == 0)
        def _():
            S_ref[...] = jnp.zeros_like(S_ref)

    lg = [lg_ref[hg * nh + h] for h in hs]
    q = [q_ref[:, h * RET_DK:(h + 1) * RET_DK] for h in hs]
    k = [k_ref[:, h * RET_DK:(h + 1) * RET_DK] for h in hs]
    v = [v_ref[:, h * RET_DV:(h + 1) * RET_DV] for h in hs]
    row = lax.broadcasted_iota(jnp.int32, (L, L), 0)
    col = lax.broadcasted_iota(jnp.int32, (L, L), 1)
    diff = (row - col).astype(f32)
    idx = lax.broadcasted_iota(jnp.int32, (L, 1), 0).astype(f32)
    decay = [jnp.where(diff >= 0, jnp.exp(lg[h] * jnp.maximum(diff, 0.0)), 0.0) for h in hs]
    xi = [jnp.exp(lg[h] * (idx + 1.0)) for h in hs]
    zeta = [jnp.exp(lg[h] * (L - 1.0 - idx)) for h in hs]

    S = [S_ref[h] for h in hs]
    scores = [lax.dot_general(q[h], k[h], NT_DIMS, preferred_element_type=f32) * decay[h] for h in hs]
    o = [dot(scores[h].astype(bf16), v[h])
         + dot((q[h].astype(f32) * xi[h]).astype(bf16), S[h].astype(bf16)) for h in hs]
    S_new = [jnp.exp(lg[h] * L) * S[h] + dot((k[h].astype(f32) * zeta[h]).T.astype(bf16), v[h])
             for h in hs]
    for h in hs:
        S_ref[h] = S_new[h]
        sout_ref[0, h] = S_new[h]

    cen = [o[h] - jnp.mean(o[h], axis=-1, keepdims=True) for h in hs]
    on = [cen[h] * lax.rsqrt(jnp.mean(cen[h] * cen[h], axis=-1, keepdims=True) + RET_GN_EPS) for h in hs]
    for h in hs:
        sl = slice(h * RET_DV, (h + 1) * RET_DV)
        y_ref[:, sl] = (jax.nn.silu(g_ref[:, sl].astype(f32)) * on[h] * gnw_ref[:, sl]).astype(y_ref.dtype)


def _retention(proj, state, gn_w, lg, row0, nrows, L, from_state, name):
    nsteps = nrows // L
    b0 = row0 // L
    assert nsteps * L == nrows and b0 * L == row0
    nseq = nsteps if from_state else 1
    G = RET_HEADS_PER_STEP
    kw, vw = G * RET_DK, G * RET_DV
    kb = RET_HEADS * RET_DK // kw
    vb = 2 * RET_HEADS * RET_DK // vw
    gb = vb + RET_HEADS * RET_DV // vw
    state_map = (lambda h, c: (c, h, 0, 0)) if from_state else (lambda h, c: (0, h, 0, 0))

    return pl.pallas_call(
        functools.partial(_ret_kernel, from_state=from_state),
        out_shape=(jax.ShapeDtypeStruct((nrows, RET_HEADS * RET_DV), bf16),
                   jax.ShapeDtypeStruct((nseq, RET_HEADS, RET_DK, RET_DV), f32)),
        grid=(RET_HEADS // G, nsteps),
        in_specs=[pl.BlockSpec(memory_space=pltpu.SMEM),
                  pl.BlockSpec((L, kw), lambda h, c: (b0 + c, h)),
                  pl.BlockSpec((L, kw), lambda h, c: (b0 + c, kb + h)),
                  pl.BlockSpec((L, vw), lambda h, c: (b0 + c, vb + h)),
                  pl.BlockSpec((L, vw), lambda h, c: (b0 + c, gb + h)),
                  pl.BlockSpec((1, G, RET_DK, RET_DV), state_map),
                  pl.BlockSpec((1, vw), lambda h, c: (0, h))],
        out_specs=(pl.BlockSpec((L, vw), lambda h, c: (c, h)),
                   pl.BlockSpec((1, G, RET_DK, RET_DV), state_map)),
        scratch_shapes=[pltpu.VMEM((G, RET_DK, RET_DV), f32)],
        compiler_params=_params(("parallel", "arbitrary")),
        name=name,
    )(lg, proj, proj, proj, proj, state, gn_w)


SUBLANES = 8


def _lerpmm_kernel(x_ref, xprev_ref, g_ref, start_ref, mu_ref, w_ref, o_ref, hlast_ref,
                   h_ref, xx_ref, l_ref, *, tiles_per_d, n_big, npc):
    i = pl.program_id(0)
    j = pl.program_id(1)
    tm = x_ref.shape[0]
    cpb = tm // CHUNK

    @pl.when(j == 0)
    def _():
        g = g_ref[...]
        h = _rms(x_ref[...], g)
        h_ref[...] = h
        xx_ref[...] = pltpu.roll(h, 1, 0) - h
        prev = _rms(xprev_ref[...], g)[SUBLANES - 1:SUBLANES, :]
        xx_ref[0:1, :] = prev - h[0:1, :]
        for ci in range(cpb):
            gc = i * cpb + ci
            r0 = ci * CHUNK

            @pl.when(jnp.logical_or(gc == 0, gc >= npc))
            def _(ci=ci, r0=r0):
                xx_ref[r0:r0 + 1, :] = start_ref[ci] - h_ref[r0:r0 + 1, :]

            hlast_ref[ci] = h[r0 + CHUNK - 1:r0 + CHUNK, :]

    @pl.when(jnp.logical_or(j % tiles_per_d == 0, j >= n_big))
    def _():
        l_ref[...] = (h_ref[...] + xx_ref[...] * mu_ref[0]).astype(bf16)

    o_ref[...] = jnp.dot(l_ref[...], w_ref[...], preferred_element_type=f32).astype(o_ref.dtype)


def _lerp_matmul(x, g, starts, mu_tiles, wcat, tn, tiles_per_d, n_big, npc):
    T, D = x.shape
    N = wcat.shape[1]
    tm = _pick(T, (1024, 512, 256, 128, 64))
    cpb = tm // CHUNK
    return pl.pallas_call(
        functools.partial(_lerpmm_kernel, tiles_per_d=tiles_per_d, n_big=n_big, npc=npc),
        out_shape=(jax.ShapeDtypeStruct((T, N), bf16),
                   jax.ShapeDtypeStruct((T // CHUNK, 1, D), f32)),
        grid=(T // tm, N // tn),
        in_specs=[pl.BlockSpec((tm, D), lambda i, j: (i, 0)),
                  pl.BlockSpec((SUBLANES, D), lambda i, j: (jnp.maximum(i * (tm // SUBLANES) - 1, 0), 0)),
                  pl.BlockSpec((1, D), lambda i, j: (0, 0)),
                  pl.BlockSpec((cpb, 1, D), lambda i, j: (i, 0, 0)),
                  pl.BlockSpec((1, 1, D), lambda i, j: (j, 0, 0)),
                  pl.BlockSpec((D, tn), lambda i, j: (0, j))],
        out_specs=(pl.BlockSpec((tm, tn), lambda i, j: (i, j)),
                   pl.BlockSpec((cpb, 1, D), lambda i, j: (i, 0, 0))),
        scratch_shapes=[pltpu.VMEM((tm, D), f32), pltpu.VMEM((tm, D), f32), pltpu.VMEM((tm, D), bf16)],
        compiler_params=_params(("parallel", "arbitrary")),
        name="rwkv_proj",
    )(x, x, g, starts, mu_tiles, wcat)


DECAY_SCALE = 0.6065306597126334
LANES = 128


SEG_TILE = 256


def _head_sum(x):
    rows, D = x.shape
    r = lax.broadcasted_iota(jnp.int32, (SEG_TILE, SEG_TILE), 0) // RWKV_HS
    c = lax.broadcasted_iota(jnp.int32, (SEG_TILE, SEG_TILE), 1) // RWKV_HS
    ones = (r == c).astype(bf16)
    out = [jnp.dot(x[:, j * SEG_TILE:(j + 1) * SEG_TILE].astype(bf16), ones, preferred_element_type=f32)
           for j in range(D // SEG_TILE)]
    return jnp.concatenate(out, axis=1)


def _chunk_cumsum(x):
    rows = x.shape[0]
    row = lax.broadcasted_iota(jnp.int32, (rows, rows), 0)
    col = lax.broadcasted_iota(jnp.int32, (rows, rows), 1)
    tri = jnp.logical_and((row // CHUNK) == (col // CHUNK), row >= col).astype(bf16)
    hi = x.astype(bf16)
    rest = x - hi.astype(f32)
    mid = rest.astype(bf16)
    lo = (rest - mid.astype(f32)).astype(bf16)
    dot = functools.partial(jnp.dot, preferred_element_type=f32)
    return dot(tri, hi) + dot(tri, mid) + dot(tri, lo)


def _rwkv_prep_kernel(p_ref, r_ref, k_ref, v_ref, w2_ref, a2_ref, g2_ref, w0_ref, a0_ref,
                      kk_ref, ka_ref, rk_ref, lnw_ref, lnb_ref,
                      at_ref, rt_ref, bt_ref, kt_ref, bh_ref, kh_ref, gl_ref, ga_ref, gb_ref):
    P = LORA_PAD
    tm = r_ref.shape[0]
    rw, ra, rg = w2_ref.shape[0], a2_ref.shape[0], g2_ref.shape[0]
    pw = jnp.tanh(p_ref[:, :rw].astype(f32)).astype(bf16)
    pa = p_ref[:, P:P + ra]
    pg = jax.nn.sigmoid(p_ref[:, 2 * P:2 * P + rg].astype(f32)).astype(bf16)
    wl = w0_ref[...] + jnp.dot(pw, w2_ref[...], preferred_element_type=f32)
    lw = -DECAY_SCALE * jax.nn.sigmoid(wl)
    a = jax.nn.sigmoid(a0_ref[...] + jnp.dot(pa, a2_ref[...], preferred_element_type=f32))
    g = jnp.dot(pg, g2_ref[...], preferred_element_type=f32)

    cum = _chunk_cumsum(lw)
    chunk_decay = [jnp.exp(cum[(ci + 1) * CHUNK - 1:(ci + 1) * CHUNK, :]) for ci in range(tm // CHUNK)]

    r, k, v = r_ref[...].astype(f32), k_ref[...].astype(f32), v_ref[...].astype(f32)
    kk = k * kk_ref[...]
    kk = kk * lax.rsqrt(jnp.maximum(_head_sum(kk * kk), 1e-24))
    kmod = k * (1.0 + (a - 1.0) * ka_ref[...])
    beta = kk * a
    e_in = jnp.exp(cum)
    e_neg = 1.0 / e_in
    e_end = jnp.concatenate([chunk_decay[ci] * e_neg[ci * CHUNK:(ci + 1) * CHUNK]
                             for ci in range(tm // CHUNK)], axis=0)
    at_ref[...] = (-kk * jnp.exp(cum - lw)).astype(bf16)
    rt_ref[...] = (r * e_in).astype(bf16)
    bt_ref[...] = (beta * e_neg).astype(bf16)
    kt_ref[...] = (kmod * e_neg).astype(bf16)
    bh_ref[...] = (beta * e_end).astype(bf16)
    kh_ref[...] = (kmod * e_end).astype(bf16)
    bonus = _head_sum(r * kmod * rk_ref[...]) * v
    ga_ref[...] = (lnw_ref[...] * g).astype(bf16)
    gb_ref[...] = ((lnb_ref[...] + bonus) * g).astype(bf16)
    for ci in range(tm // CHUNK):
        gl_ref[ci] = chunk_decay[ci]


def _rwkv_prep(proj, lora_block, w2p, a2p, g2p, w0, a0, k_k, k_a, r_k, ln_w, ln_b):
    T = proj.shape[0]
    D = w2p.shape[1]
    P = LORA_PAD
    tm = _pick(T, (128, 64))
    vspec = pl.BlockSpec((1, D), lambda i: (0, 0))
    ospec = pl.BlockSpec((tm, D), lambda i: (i, 0))
    o16 = jax.ShapeDtypeStruct((T, D), bf16)
    return pl.pallas_call(
        _rwkv_prep_kernel,
        out_shape=(o16,) * 6 + (jax.ShapeDtypeStruct((T // CHUNK, 1, D), f32), o16, o16),
        grid=(T // tm,),
        in_specs=[pl.BlockSpec((tm, 3 * P), lambda i: (i, lora_block)),
                  pl.BlockSpec((tm, D), lambda i: (i, 0)),
                  pl.BlockSpec((tm, D), lambda i: (i, 1)),
                  pl.BlockSpec((tm, D), lambda i: (i, 2)),
                  _resident(w2p.shape), _resident(a2p.shape), _resident(g2p.shape)] + [vspec] * 7,
        out_specs=(ospec,) * 6 + (pl.BlockSpec((tm // CHUNK, 1, D), lambda i: (i, 0, 0)), ospec, ospec),
        compiler_params=_params(("parallel",)),
        name="rwkv_prep",
    )(proj, proj, proj, proj, w2p, a2p, g2p, w0, a0, k_k, k_a, r_k, ln_w, ln_b)


def _rwkv_heads_chunk(heads):
    L, N = heads[0][6].shape
    assert L == N
    nh = len(heads)
    rng = range(nh)
    dot = functools.partial(jnp.dot, preferred_element_type=f32)
    row = lax.broadcasted_iota(jnp.int32, (L, 2 * L), 0)
    lane = lax.broadcasted_iota(jnp.int32, (L, 2 * L), 1)
    lo = lane < L
    tok = jnp.where(lo, lane, lane - L)
    incl = row >= tok
    strict = row > tok
    lhs = [jnp.concatenate([h[0], h[1]], axis=0) for h in heads]
    rhs = [jnp.concatenate([h[2], h[3]], axis=0) for h in heads]
    G = [lax.dot_general(lhs[i], rhs[i], NT_DIMS, preferred_element_type=f32) for i in rng]
    LS = [lax.dot_general(lhs[i], heads[i][7].astype(bf16), NT_DIMS, preferred_element_type=f32)
          for i in rng]
    top = [jnp.where(strict, g[:L], 0.0) for g in G]
    a_r = [jnp.where(incl, g[L:], 0.0).astype(bf16) for g in G]
    v2 = [jnp.concatenate([h[6], h[6]], axis=0) for h in heads]
    zero = jnp.zeros((L, N), f32)
    x0 = [LS[i][:L] + dot(jnp.where(lo, 0.0, top[i]).astype(bf16), v2[i]) for i in rng]
    p = [jnp.where(lo, top[i], jnp.concatenate([zero, x0[i]], axis=1)) for i in rng]
    steps = max(1, (L - 1).bit_length())
    for s in range(steps):
        z = [dot(p[i][:, :L].astype(bf16), p[i].astype(bf16)) for i in rng]
        if s < steps - 1:
            p = [z[i] + jnp.where(lo, 0.0, p[i]) for i in rng]
        else:
            u = [(p[i] + z[i])[:, L:] for i in rng]
    uv = [jnp.concatenate([u[i], heads[i][6].astype(f32)], axis=0) for i in rng]
    o = [LS[i][L:] + dot(a_r[i], uv[i].astype(bf16)) for i in rng]
    S_new = [heads[i][7] * heads[i][8]
             + dot(uv[i].T.astype(bf16), jnp.concatenate([heads[i][4], heads[i][5]], axis=0))
             for i in rng]
    return list(zip(o, S_new))


def _rwkv_core_kernel(at_ref, rt_ref, bt_ref, kt_ref, bh_ref, kh_ref, v_ref, gl_ref, s0_ref,
                      ga_ref, gb_ref, y_ref, sp_ref, ss_ref, S_ref, *, npc, nhead):
    c = pl.program_id(1)
    N = RWKV_HS

    @pl.when(c == 0)
    def _():
        S_ref[...] = jnp.zeros_like(S_ref)

    @pl.when(c >= npc)
    def _():
        S_ref[...] = s0_ref[0]

    gl = gl_ref[0]
    heads = []
    for hh in range(nhead):
        sl = slice(hh * N, (hh + 1) * N)
        heads.append((at_ref[:, sl], rt_ref[:, sl], bt_ref[:, sl], kt_ref[:, sl], bh_ref[:, sl],
                      kh_ref[:, sl], v_ref[:, sl], S_ref[hh], gl[:, sl]))
    res = _rwkv_heads_chunk(heads)
    for hh in range(nhead):
        S_ref[hh] = res[hh][1]
    outs = [o for o, _ in res]
    cen = [o - jnp.mean(o, axis=-1, keepdims=True) for o in outs]
    nrm = [d * lax.rsqrt(jnp.mean(d * d, axis=-1, keepdims=True) + RWKV_GN_EPS) for d in cen]
    y_ref[...] = (jnp.concatenate(nrm, axis=1) * ga_ref[...].astype(f32)
                  + gb_ref[...].astype(f32)).astype(y_ref.dtype)

    @pl.when(c == npc - 1)
    def _():
        sp_ref[0] = S_ref[...]

    @pl.when(c >= npc)
    def _():
        ss_ref[0] = S_ref[...]


RWKV_HEADS_PER_STEP = 32


def _rwkv_core(ops, proj, v_block, gl, state, ga, gb, npc, nchunks):
    T, D = ga.shape
    N = RWKV_HS
    nh = D // N
    nb = state.shape[0]
    G = RWKV_HEADS_PER_STEP
    W = G * N
    tok = pl.BlockSpec((CHUNK, W), lambda p, c: (c, p))
    vtok = pl.BlockSpec((CHUNK, W), lambda p, c: (c, v_block * (D // W) + p))

    sample_state = pl.BlockSpec((1, G, N, N), lambda p, c: (jnp.maximum(c - npc, 0), p, 0, 0))

    return pl.pallas_call(
        functools.partial(_rwkv_core_kernel, npc=npc, nhead=G),
        out_shape=(jax.ShapeDtypeStruct((T, D), bf16),
                   jax.ShapeDtypeStruct((1, nh, N, N), f32),
                   jax.ShapeDtypeStruct((nb, nh, N, N), f32)),
        grid=(nh // G, nchunks),
        in_specs=[tok] * 6 + [vtok, pl.BlockSpec((1, 1, W), lambda p, c: (c, 0, p)),
                              sample_state, tok, tok],
        out_specs=(tok, pl.BlockSpec((1, G, N, N), lambda p, c: (0, p, 0, 0)), sample_state),
        scratch_shapes=[pltpu.VMEM((G, N, N), f32)],
        compiler_params=_params(("parallel", "arbitrary")),
        name="rwkv_core",
    )(*ops, proj, gl, state, ga, gb)


def _positions(n_prompt, n_b, n_s):
    return jnp.concatenate([jnp.arange(n_prompt), jnp.tile(PAST_LEN + jnp.arange(n_s), n_b)]).astype(f32)


def _attn_tables(pos):
    half = ROT_DIM // 2
    inv = ROPE_THETA ** (-jnp.arange(half, dtype=f32) / half)
    ang = pos[:, None] * inv[None, :]
    cos, sin = jnp.cos(ang), jnp.sin(ang)
    T = pos.shape[0]
    zeros = jnp.zeros((T, half), f32)
    rest = HEAD_DIM - ROT_DIM
    c = jnp.concatenate([cos, cos, jnp.ones((T, rest), f32)], axis=1)
    s1 = jnp.concatenate([zeros, sin, jnp.zeros((T, rest), f32)], axis=1)
    s2 = jnp.concatenate([-sin, zeros, jnp.zeros((T, rest), f32)], axis=1)
    return tuple(jnp.tile(t, (1, 2)) for t in (c, s1, s2))


def _ret_tables(pos):
    half = RET_DK // 2
    inv = RET_THETA ** (-jnp.arange(half, dtype=f32) / half)
    ang = pos[:, None] * inv[None, :]
    return jnp.cos(ang), jnp.sin(ang)


def _pad_cols(w, n):
    return jnp.pad(w, ((0, 0), (0, n - w.shape[1])))


def _pad_rows_to_lanes(w):
    n = -(-w.shape[0] // LANES) * LANES
    return jnp.pad(w, ((0, n - w.shape[0]), (0, 0)))


def kernel(x_prompt, x_sample, cache_attn_k, cache_attn_v, state_ret, state_rwkv, state_rwkv_shift,
           norm_mix, norm_mlp, norm_final,
           attn_w_qkv, attn_sinks, attn_w_o,
           ret_w_in, ret_gn_w, ret_w_o,
           rwkv_mu, rwkv_w_rkv, rwkv_w_o, rwkv_w0, rwkv_w1, rwkv_w2, rwkv_a0, rwkv_a1, rwkv_a2,
           rwkv_g1, rwkv_g2, rwkv_k_k, rwkv_k_a, rwkv_r_k, rwkv_ln_w, rwkv_ln_b,
           mlp_w_up, mlp_w_down):
    bp, tp, D = x_prompt.shape
    nb, ns, _ = x_sample.shape
    assert bp == 1 and ns == CHUNK and tp % CHUNK == 0
    depth = norm_mix.shape[0]
    npc = tp // CHUNK
    nchunks = npc + nb
    T = tp + nb * ns
    qd = ATTN_HEADS * HEAD_DIM
    kvd = ATTN_KV_HEADS * HEAD_DIM

    x = (x_prompt.reshape(tp, D), x_sample.reshape(nb * ns, D))
    pos = _positions(tp, nb, ns)
    attn_tabs = _attn_tables(pos)
    ret_tabs = _ret_tables(pos)
    lg = jnp.log1p(-jnp.exp2(-5.0 - jnp.arange(RET_HEADS, dtype=f32)))
    tm_proj = _pick(T, (512, 256, 128, 64))
    tm_ret = _pick(T, (1024, 512, 256, 128, 64))
    w_up16 = mlp_w_up.astype(bf16)
    w_down16 = mlp_w_down.astype(bf16)

    kp_l, vp_l, ks_l, vs_l, rp_l, rs_l, wp_l, ws_l, shp_l, shs_l = ([] for _ in range(10))
    for i in range(depth):
        j, kind = divmod(i, 3)
        g_mix = norm_mix[i][None, :]
        if kind == 0:
            nsub = (qd + 2 * kvd) // (2 * EPI_COLS)
            routes = [[("q", ((0, s * EPI_COLS),)) for s in range(nsub)],
                      [("q", ((0, s * EPI_COLS),)) for s in range(nsub, qd // EPI_COLS)]
                      + [("k", ((1, 0), (2, 0))), ("v", ((1, kvd), (2, 2 * kvd)))]]
            q, kv, kvdup = _norm_matmul(
                x, g_mix, attn_w_qkv[j].astype(bf16), attn_tabs, routes, _epi_attn,
                [jax.ShapeDtypeStruct((T, qd), bf16), jax.ShapeDtypeStruct((T, 2 * kvd), f32),
                 jax.ShapeDtypeStruct((T, 4 * kvd), bf16)],
                [pl.BlockSpec((tm_proj, qd), lambda i, jj: (i, 0)),
                 pl.BlockSpec((tm_proj, 2 * kvd), lambda i, jj: (i, 0)),
                 pl.BlockSpec((tm_proj, 4 * kvd), lambda i, jj: (i, 0))],
                tm_proj, "attn_qkv")
            k_new = kv[:, :kvd]
            v_new = kv[:, kvd:]

            def cache_dup(cache):
                c5 = cache.astype(bf16).reshape(nb * WINDOW, ATTN_KV_HEADS, 1, HEAD_DIM)
                return jnp.broadcast_to(c5, (nb * WINDOW, ATTN_KV_HEADS, 2, HEAD_DIM)).reshape(nb * WINDOW, 2 * kvd)

            def new_window(new, cache):
                return jnp.concatenate([cache.reshape(nb, WINDOW, kvd), new[tp:].reshape(nb, ns, kvd)], axis=1)

            o = _attention(q, kvdup, cache_dup(cache_attn_k[j]), cache_dup(cache_attn_v[j]),
                           attn_sinks[j], npc, nchunks)
            k_samp = new_window(k_new, cache_attn_k[j])
            v_samp = new_window(v_new, cache_attn_v[j])
            x = _matmul_residual(o, attn_w_o[j].astype(bf16), x, "attn_out")
            kp_l.append(k_new[tp - WINDOW:tp].reshape(1, WINDOW, ATTN_KV_HEADS, HEAD_DIM))
            vp_l.append(v_new[tp - WINDOW:tp].reshape(1, WINDOW, ATTN_KV_HEADS, HEAD_DIM))
            ks_l.append(k_samp[:, -WINDOW:].reshape(nb, WINDOW, ATTN_KV_HEADS, HEAD_DIM))
            vs_l.append(v_samp[:, -WINDOW:].reshape(nb, WINDOW, ATTN_KV_HEADS, HEAD_DIM))
        elif kind == 1:
            nq = RET_HEADS * RET_DK // RET_TILE
            nv = RET_HEADS * RET_DV // RET_TILE
            tile = lambda mode: [(mode, ((0, s * EPI_COLS),)) for s in range(RET_TILE // EPI_COLS)]
            routes = [tile("q")] * nq + [tile("k")] * nq + [tile("plain")] * (2 * nv)
            (proj,) = _norm_matmul(
                x, g_mix, ret_w_in[j].astype(bf16), ret_tabs, routes, _epi_ret,
                [jax.ShapeDtypeStruct((T, ret_w_in.shape[2]), bf16)],
                [pl.BlockSpec((tm_ret, RET_TILE), lambda i, jj: (i, jj))], tm_ret, "ret_proj")
            gnw = ret_gn_w[j][None, :]
            lp = _pick(tp, (256, 128, 64))
            y_p, s_p = _retention(proj, state_ret[j], gnw, lg, 0, tp, lp, False, "ret_prompt")
            y_s, s_s = _retention(proj, state_ret[j], gnw, lg, tp, nb * ns, ns, True, "ret_sample")
            x = _matmul_residual2(y_p, y_s, ret_w_o[j].astype(bf16), x, "ret_out")
            rp_l.append(s_p)
            rs_l.append(s_s)
        else:
            starts = jnp.concatenate([jnp.zeros((npc, D), f32), state_rwkv_shift[j]], axis=0)[:, None, :]
            P = LORA_PAD
            tn = LORA_PAD
            wcat = jnp.concatenate(
                [rwkv_w_rkv[j][0], rwkv_w_rkv[j][1], rwkv_w_rkv[j][2],
                 _pad_cols(rwkv_w1[j], P), _pad_cols(rwkv_a1[j], P), _pad_cols(rwkv_g1[j], P)],
                axis=1).astype(bf16)
            tiles_per_d = D // tn
            mu = rwkv_mu[j]
            mu_tiles = jnp.concatenate(
                [jnp.repeat(mu[jnp.array([0, 2, 3])], tiles_per_d, axis=0), mu[jnp.array([1, 4, 5])]],
                axis=0)[:, None, :]
            proj, h_last = _lerp_matmul(x, g_mix, starts, mu_tiles, wcat, tn, tiles_per_d,
                                        3 * tiles_per_d, npc)
            prep = _rwkv_prep(
                proj, 3 * D // (3 * P),
                _pad_rows_to_lanes(rwkv_w2[j]).astype(bf16), _pad_rows_to_lanes(rwkv_a2[j]).astype(bf16),
                _pad_rows_to_lanes(rwkv_g2[j]).astype(bf16), rwkv_w0[j][None, :], rwkv_a0[j][None, :],
                rwkv_k_k[j][None, :], rwkv_k_a[j][None, :], rwkv_r_k[j].reshape(1, D),
                rwkv_ln_w[j][None, :], rwkv_ln_b[j][None, :])
            ops, gl, ga, gb = prep[:6], prep[6], prep[7], prep[8]
            y, s_p, s_s = _rwkv_core(ops, proj, 2, gl, state_rwkv[j], ga, gb, npc, nchunks)
            x = _matmul_residual(y, rwkv_w_o[j].astype(bf16), x, "rwkv_out")
            wp_l.append(s_p)
            ws_l.append(s_s)
            shp_l.append(h_last[npc - 1])
            shs_l.append(h_last[npc:, 0])
        x = _mlp(x, norm_mlp[i][None, :], w_up16, w_down16, i, norm_final[None, :],
                 tp if i == depth - 1 else None, "mlp")

    y_prompt = x[0].reshape(1, tp, D)
    y_sample = x[1].reshape(nb, ns, D)
    return (y_prompt, y_sample,
            jnp.stack(kp_l), jnp.stack(vp_l), jnp.stack(ks_l), jnp.stack(vs_l),
            jnp.stack(rp_l), jnp.stack(rs_l),
            jnp.stack(wp_l), jnp.stack(ws_l), jnp.stack(shp_l), jnp.stack(shs_l))
```

```python
import functools

import jax
import jax.numpy as jnp
from jax import lax
from jax.experimental import pallas as pl
from jax.experimental.pallas import tpu as pltpu

f32 = jnp.float32
bf16 = jnp.bfloat16

CHUNK = 64
NORM_EPS = 1e-5
PAST_LEN = 4096

ATTN_HEADS = 32
ATTN_KV_HEADS = 4
ATTN_GROUP = ATTN_HEADS // ATTN_KV_HEADS
HEAD_DIM = 64
WINDOW = 128
ROT_DIM = HEAD_DIM // 4
ROPE_THETA = 500000.0

RET_HEADS = 8
RET_DK = 256
RET_DV = 512
RET_THETA = 10000.0
RET_GN_EPS = 1e-5
RET_TILE = 1024

RWKV_HS = 64
RWKV_GN_EPS = 64e-5
LORA_PAD = 512

VMEM_LIMIT = 52 * 1024 * 1024

NT_DIMS = (((1,), (1,)), ((), ()))


def _pick(n, cands):
    for c in cands:
        if n % c == 0:
            return c
    raise ValueError(f"no tile for {n} in {cands}")


def _params(sem):
    return pltpu.CompilerParams(dimension_semantics=sem, vmem_limit_bytes=VMEM_LIMIT)


def _rms(x, g):
    return x * lax.rsqrt(jnp.mean(x * x, axis=-1, keepdims=True) + NORM_EPS) * g


def _dup_heads(x):
    pw = 2 * HEAD_DIM
    lo = lax.broadcasted_iota(jnp.int32, (1, pw), 1) < HEAD_DIM
    out = []
    for c in range(x.shape[1] // pw):
        xc = x[:, c * pw:(c + 1) * pw]
        sw = pltpu.roll(xc, HEAD_DIM, 1)
        out += [jnp.where(lo, xc, sw), jnp.where(lo, sw, xc)]
    return jnp.concatenate(out, axis=1)


def _epi_attn(blk, mode, tabs):
    if mode == "v":
        return blk, _dup_heads(blk)
    c, s1, s2 = (t[...] for t in tabs)
    pw, half = c.shape[1], ROT_DIM // 2
    out = jnp.concatenate(
        [xc * c + pltpu.roll(xc, half, 1) * s1 + pltpu.roll(xc, pw - half, 1) * s2
         for xc in (blk[:, i * pw:(i + 1) * pw] for i in range(blk.shape[1] // pw))], axis=1)
    if mode == "q":
        return (out * (HEAD_DIM ** -0.5),)
    return out, _dup_heads(out)


def _epi_ret(blk, mode, tabs):
    if mode == "plain":
        return (blk,)
    cos_ref, sin_ref = tabs
    c, s = cos_ref[...], sin_ref[...]
    x1, x2 = blk[:, :RET_DK // 2], blk[:, RET_DK // 2:]
    out = jnp.concatenate([x1 * c - x2 * s, x2 * c + x1 * s], axis=1)
    return (out * (RET_DK ** -0.5) if mode == "k" else out,)


EPI_COLS = 256


def _row_split_specs(tm, cols, split_at):
    first = pl.BlockSpec((tm, cols), lambda i, *_: (jnp.minimum(i, split_at - 1), 0))
    second = pl.BlockSpec((tm, cols), lambda i, *_: (jnp.maximum(i - split_at, 0), 0))
    return [first, second]


def _normmm_kernel(*refs, routes, ntab, nout, epi, split_at):
    nx = 1 if split_at is None else 2
    x_refs = refs[:nx]
    g_ref, w_ref = refs[nx:nx + 2]
    tabs = refs[nx + 2:nx + 2 + ntab]
    outs = refs[nx + 2 + ntab:nx + 2 + ntab + nout]
    xn_ref = refs[nx + 2 + ntab + nout]
    i = pl.program_id(0)
    j = pl.program_id(1)

    if split_at is None:
        @pl.when(j == 0)
        def _():
            xn_ref[...] = _rms(x_refs[0][...], g_ref[...]).astype(bf16)
    else:
        @pl.when(jnp.logical_and(j == 0, i < split_at))
        def _():
            xn_ref[...] = _rms(x_refs[0][...], g_ref[...]).astype(bf16)

        @pl.when(jnp.logical_and(j == 0, i >= split_at))
        def _():
            xn_ref[...] = _rms(x_refs[1][...], g_ref[...]).astype(bf16)

    groups = {}
    for jj, tile_routes in enumerate(routes):
        groups.setdefault(tuple(tile_routes), []).append(jj)
    for tile_routes, jjs in groups.items():
        cond = functools.reduce(jnp.logical_or, [j == jj for jj in jjs])

        @pl.when(cond)
        def _(tile_routes=tile_routes):
            for s, (mode, dests) in enumerate(tile_routes):
                acc = jnp.dot(xn_ref[...], w_ref[:, s * EPI_COLS:(s + 1) * EPI_COLS],
                              preferred_element_type=f32)
                for (oi, col), val in zip(dests, epi(acc, mode, tabs)):
                    outs[oi][:, col:col + val.shape[1]] = val.astype(outs[oi].dtype)


def _norm_matmul(x, g, w, tabs, routes, epi, out_shapes, out_specs, tm, name):
    xs = x if isinstance(x, tuple) else (x,)
    T = sum(p.shape[0] for p in xs)
    D = xs[0].shape[1]
    N = w.shape[1]
    ntile = len(routes)
    tn = N // ntile
    assert tn * ntile == N and tn == EPI_COLS * len(routes[0])
    if len(xs) == 1:
        split_at = None
        x_specs = [pl.BlockSpec((tm, D), lambda i, j: (i, 0))]
    else:
        assert xs[0].shape[0] % tm == 0 and xs[1].shape[0] % tm == 0
        split_at = xs[0].shape[0] // tm
        x_specs = _row_split_specs(tm, D, split_at)
    tab_specs = [pl.BlockSpec((tm, t.shape[1]), lambda i, j: (i, 0)) for t in tabs]
    return pl.pallas_call(
        functools.partial(_normmm_kernel, routes=routes, ntab=len(tabs), nout=len(out_shapes), epi=epi,
                          split_at=split_at),
        out_shape=tuple(out_shapes),
        grid=(T // tm, ntile),
        in_specs=x_specs + [pl.BlockSpec((1, D), lambda i, j: (0, 0)),
                            pl.BlockSpec((D, tn), lambda i, j: (0, j))] + tab_specs,
        out_specs=tuple(out_specs),
        scratch_shapes=[pltpu.VMEM((tm, D), bf16)],
        compiler_params=_params(("parallel", "arbitrary")),
        name=name,
    )(*xs, g, w, *tabs)


def _mmres_kernel(a_ref, w_ref, *rest, split_at):
    o_ref = rest[-1]
    prod = jnp.dot(a_ref[...], w_ref[...], preferred_element_type=f32)
    if split_at is None:
        o_ref[...] = rest[0][...] + prod
    else:
        i = pl.program_id(0)

        @pl.when(i < split_at)
        def _():
            o_ref[...] = rest[0][...] + prod

        @pl.when(i >= split_at)
        def _():
            o_ref[...] = rest[1][...] + prod


def _resident(shape):
    return pl.BlockSpec(shape, lambda *_: (0,) * len(shape), pipeline_mode=pl.Buffered(1))


def _matmul_residual(a, w, res, name):
    T, K = a.shape
    N = w.shape[1]
    rs = res if isinstance(res, tuple) else (res,)
    if len(rs) == 1:
        tm = _pick(T, (512, 256, 128, 64))
        split_at = None
        r_specs = [pl.BlockSpec((tm, N), lambda i: (i, 0))]
    else:
        tm = _pick(rs[1].shape[0], (512, 256, 128, 64))
        assert rs[0].shape[0] % tm == 0
        split_at = rs[0].shape[0] // tm
        r_specs = _row_split_specs(tm, N, split_at)
    return pl.pallas_call(
        functools.partial(_mmres_kernel, split_at=split_at),
        out_shape=jax.ShapeDtypeStruct((T, N), f32),
        grid=(T // tm,),
        in_specs=[pl.BlockSpec((tm, K), lambda i: (i, 0)), _resident((K, N))] + r_specs,
        out_specs=pl.BlockSpec((tm, N), lambda i: (i, 0)),
        compiler_params=_params(("parallel",)),
        name=name,
    )(a, w, *rs)


def _mmres2_kernel(ap_ref, as_ref, w_ref, r_ref, o_ref, *, np_blocks):
    i = pl.program_id(0)

    @pl.when(i < np_blocks)
    def _():
        o_ref[...] = r_ref[...] + jnp.dot(ap_ref[...], w_ref[...], preferred_element_type=f32)

    @pl.when(i >= np_blocks)
    def _():
        o_ref[...] = r_ref[...] + jnp.dot(as_ref[...], w_ref[...], preferred_element_type=f32)


def _matmul_residual2(a_p, a_s, w, res, name):
    tp, K = a_p.shape
    ts = a_s.shape[0]
    N = w.shape[1]
    tm = _pick(ts, (256, 128, 64))
    assert tp % tm == 0
    np_blocks = tp // tm
    return pl.pallas_call(
        functools.partial(_mmres2_kernel, np_blocks=np_blocks),
        out_shape=jax.ShapeDtypeStruct((tp + ts, N), f32),
        grid=((tp + ts) // tm,),
        in_specs=[pl.BlockSpec((tm, K), lambda i: (jnp.minimum(i, np_blocks - 1), 0)),
                  pl.BlockSpec((tm, K), lambda i: (jnp.maximum(i - np_blocks, 0), 0)),
                  _resident((K, N)),
                  pl.BlockSpec((tm, N), lambda i: (i, 0))],
        out_specs=pl.BlockSpec((tm, N), lambda i: (i, 0)),
        compiler_params=_params(("parallel",)),
        name=name,
    )(a_p, a_s, w, res)


def _mlp_kernel(x_ref, g_ref, wu_ref, wd_ref, gf_ref, *rest, nk, split_at):
    if split_at is None:
        acc_ref, xn_ref = rest
    else:
        op_ref, os_ref, xn_ref, acc_ref = rest
    i = pl.program_id(0)
    k = pl.program_id(1)

    @pl.when(k == 0)
    def _():
        x = x_ref[...]
        xn_ref[...] = _rms(x, g_ref[...]).astype(bf16)
        acc_ref[...] = x

    h = jnp.dot(xn_ref[...], wu_ref[...], preferred_element_type=f32)
    h = jnp.square(jnp.maximum(h, 0.0)).astype(bf16)
    acc_ref[...] += jnp.dot(h, wd_ref[...], preferred_element_type=f32)

    if split_at is not None:
        @pl.when(jnp.logical_and(k == nk - 1, i < split_at))
        def _():
            op_ref[...] = _rms(acc_ref[...], gf_ref[...])

        @pl.when(jnp.logical_and(k == nk - 1, i >= split_at))
        def _():
            os_ref[...] = _rms(acc_ref[...], gf_ref[...])


def _mlp(x, g, w_up, w_down, layer, g_final, split_rows, name):
    T, D = x.shape
    F = w_up.shape[2]
    tf = _pick(F, (1024, 512))
    nk = F // tf
    if split_rows is None:
        tm = _pick(T, (512, 256, 128, 64))
        x_spec = pl.BlockSpec((tm, D), lambda i, k: (i, 0))
        split_at = None
        out_shape = jax.ShapeDtypeStruct((T, D), f32)
        out_specs = pl.BlockSpec((tm, D), lambda i, k: (i, 0))
        scratch = [pltpu.VMEM((tm, D), bf16)]
    else:
        tm = _pick(T - split_rows, (512, 256, 128, 64))
        x_spec = pl.BlockSpec((tm, D), lambda i, k: (i, 0))
        assert split_rows % tm == 0
        split_at = split_rows // tm
        out_shape = (jax.ShapeDtypeStruct((split_rows, D), f32), jax.ShapeDtypeStruct((T - split_rows, D), f32))
        out_specs = (pl.BlockSpec((tm, D), lambda i, k: (jnp.minimum(i, split_at - 1), 0)),
                     pl.BlockSpec((tm, D), lambda i, k: (jnp.maximum(i - split_at, 0), 0)))
        scratch = [pltpu.VMEM((tm, D), bf16), pltpu.VMEM((tm, D), f32)]
    return pl.pallas_call(
        functools.partial(_mlp_kernel, nk=nk, split_at=split_at),
        out_shape=out_shape,
        grid=(T // tm, nk),
        in_specs=[x_spec,
                  pl.BlockSpec((1, D), lambda i, k: (0, 0)),
                  pl.BlockSpec((None, D, tf), lambda i, k: (layer, 0, k)),
                  pl.BlockSpec((None, tf, D), lambda i, k: (layer, k, 0)),
                  pl.BlockSpec((1, D), lambda i, k: (0, 0))],
        out_specs=out_specs,
        scratch_shapes=scratch,
        compiler_params=_params(("arbitrary", "arbitrary")),
        name=name,
    )(x, g, w_up, w_down, g_final)


def _attn_kernel(sb_ref, q_ref, kp0_ref, kp1_ref, kc0_ref, kc1_ref, k2_ref,
                 vp0_ref, vp1_ref, vc0_ref, vc1_ref, v2_ref, o_ref, *, npc):
    c = pl.program_id(0)
    is_sample = c >= npc
    k0, k1, v0, v1 = (
        jnp.where(is_sample, cr[...], pr[...])
        for cr, pr in ((kc0_ref, kp0_ref), (kc1_ref, kp1_ref), (vc0_ref, vp0_ref), (vc1_ref, vp1_ref)))
    nkeys = 3 * CHUNK
    ncols = nkeys + CHUNK
    PW = 2 * HEAD_DIM
    pairs = ATTN_GROUP // 2
    zpad = jnp.zeros((CHUNK, ATTN_KV_HEADS * PW), bf16)
    kcat = jnp.concatenate([k0, k1, k2_ref[...], zpad], axis=0)
    vcat = jnp.concatenate([v0, v1, v2_ref[...], zpad], axis=0)
    lo_half = lax.broadcasted_iota(jnp.int32, (1, PW), 1) < HEAD_DIM
    ones = jnp.ones((ncols, PW), bf16)
    kvs = range(ATTN_KV_HEADS)

    def stacked_q(kv):
        parts = []
        for p in range(pairs):
            col = (kv * pairs + p) * PW
            qp = q_ref[:, col:col + PW]
            parts += [jnp.where(lo_half, qp, 0.0), jnp.where(lo_half, 0.0, qp)]
        return jnp.concatenate(parts, axis=0).astype(bf16)

    s = [lax.dot_general(stacked_q(kv), kcat[:, kv * PW:(kv + 1) * PW], NT_DIMS,
                         preferred_element_type=f32) + sb_ref[0, kv] for kv in kvs]
    p = [jnp.exp(s[kv] - jnp.max(s[kv], axis=-1, keepdims=True)).astype(bf16) for kv in kvs]
    oa = [jnp.dot(p[kv], jnp.concatenate([vcat[:, kv * PW:(kv + 1) * PW], ones], axis=1),
                  preferred_element_type=f32) for kv in kvs]
    for kv in kvs:
        on = oa[kv][:, :PW] / oa[kv][:, PW:]
        for pi in range(pairs):
            r0 = 2 * pi * CHUNK
            blk = jnp.where(lo_half, on[r0:r0 + CHUNK], on[r0 + CHUNK:r0 + 2 * CHUNK])
            col = (kv * pairs + pi) * PW
            o_ref[:, col:col + PW] = blk.astype(o_ref.dtype)


def _attention(q, kvdup, kcache, vcache, sinks, npc, nchunks):
    T = q.shape[0]
    qd = ATTN_HEADS * HEAD_DIM
    dw = kcache.shape[1]
    wblocks = WINDOW // CHUNK

    def new_spec(j, col):
        if j == 2:
            return pl.BlockSpec((CHUNK, dw), lambda c: (c, col))
        return pl.BlockSpec((CHUNK, dw), lambda c: (jnp.maximum(jnp.minimum(c, npc - 1) - 2 + j, 0), col))

    def cache_spec(j):
        return pl.BlockSpec((CHUNK, dw), lambda c: (wblocks * jnp.maximum(c - npc, 0) + j, 0))

    nkeys = 3 * CHUNK
    col = jnp.arange(nkeys + CHUNK)[None, None, None, :]
    missing = (2 - jnp.arange(3))[:, None, None, None] * CHUNK
    sink_rows = jnp.repeat(sinks.astype(f32).reshape(ATTN_KV_HEADS, ATTN_GROUP), CHUNK, axis=1)[None, :, :, None]
    sink_bias = jnp.where(col < missing, -jnp.inf,
                          jnp.where(col < nkeys, 0.0, jnp.where(col == nkeys, sink_rows, -jnp.inf))).astype(f32)

    return pl.pallas_call(
        functools.partial(_attn_kernel, npc=npc),
        out_shape=jax.ShapeDtypeStruct((T, qd), bf16),
        grid=(nchunks,),
        in_specs=[pl.BlockSpec((1,) + sink_bias.shape[1:],
                               lambda c: (jnp.where(c < npc, jnp.minimum(c, 2), 2), 0, 0, 0)),
                  pl.BlockSpec((CHUNK, qd), lambda c: (c, 0)),
                  new_spec(0, 0), new_spec(1, 0), cache_spec(0), cache_spec(1), new_spec(2, 0),
                  new_spec(0, 1), new_spec(1, 1), cache_spec(0), cache_spec(1), new_spec(2, 1)],
        out_specs=pl.BlockSpec((CHUNK, qd), lambda c: (c, 0)),
        compiler_params=_params(("parallel",)),
        name="attn_core",
    )(sink_bias, q, kvdup, kvdup, kcache, kcache, kvdup, kvdup, kvdup, vcache, vcache, kvdup)


RET_HEADS_PER_STEP = 4


def _ret_kernel(lg_ref, q_ref, k_ref, v_ref, g_ref, s0_ref, gnw_ref, y_ref, sout_ref, S_ref, *, from_state):
    hg = pl.program_id(0)
    c = pl.program_id(1)
    L = q_ref.shape[0]
    nh = RET_HEADS_PER_STEP
    hs = range(nh)
    dot = functools.partial(jnp.dot, preferred_element_type=f32)

    if from_state:
        S_ref[...] = s0_ref[0]
    else:
        @pl.when(c == 0)
        def _():
            S_ref[...] = jnp.zeros_like(S_ref)

    lg = [lg_ref[hg * nh + h] for h in hs]
    q = [q_ref[:, h * RET_DK:(h + 1) * RET_DK] for h in hs]
    k = [k_ref[:, h * RET_DK:(h + 1) * RET_DK] for h in hs]
    v = [v_ref[:, h * RET_DV:(h + 1) * RET_DV] for h in hs]
    row = lax.broadcasted_iota(jnp.int32, (L, L), 0)
    col = lax.broadcasted_iota(jnp.int32, (L, L), 1)
    diff = (row - col).astype(f32)
    idx = lax.broadcasted_iota(jnp.int32, (L, 1), 0).astype(f32)
    decay = [jnp.where(diff >= 0, jnp.exp(lg[h] * jnp.maximum(diff, 0.0)), 0.0) for h in hs]
    xi = [jnp.exp(lg[h] * (idx + 1.0)) for h in hs]
    zeta = [jnp.exp(lg[h] * (L - 1.0 - idx)) for h in hs]

    S = [S_ref[h] for h in hs]
    scores = [lax.dot_general(q[h], k[h], NT_DIMS, preferred_element_type=f32) * decay[h] for h in hs]
    o = [dot(scores[h].astype(bf16), v[h])
         + dot((q[h].astype(f32) * xi[h]).astype(bf16), S[h].astype(bf16)) for h in hs]
    S_new = [jnp.exp(lg[h] * L) * S[h] + dot((k[h].astype(f32) * zeta[h]).T.astype(bf16), v[h])
             for h in hs]
    for h in hs:
        S_ref[h] = S_new[h]
        sout_ref[0, h] = S_new[h]

    cen = [o[h] - jnp.mean(o[h], axis=-1, keepdims=True) for h in hs]
    on = [cen[h] * lax.rsqrt(jnp.mean(cen[h] * cen[h], axis=-1, keepdims=True) + RET_GN_EPS) for h in hs]
    for h in hs:
        sl = slice(h * RET_DV, (h + 1) * RET_DV)
        y_ref[:, sl] = (jax.nn.silu(g_ref[:, sl].astype(f32)) * on[h] * gnw_ref[:, sl]).astype(y_ref.dtype)


def _retention(proj, state, gn_w, lg, row0, nrows, L, from_state, name):
    nsteps = nrows // L
    b0 = row0 // L
    assert nsteps * L == nrows and b0 * L == row0
    nseq = nsteps if from_state else 1
    G = RET_HEADS_PER_STEP
    kw, vw = G * RET_DK, G * RET_DV
    kb = RET_HEADS * RET_DK // kw
    vb = 2 * RET_HEADS * RET_DK // vw
    gb = vb + RET_HEADS * RET_DV // vw
    state_map = (lambda h, c: (c, h, 0, 0)) if from_state else (lambda h, c: (0, h, 0, 0))

    return pl.pallas_call(
        functools.partial(_ret_kernel, from_state=from_state),
        out_shape=(jax.ShapeDtypeStruct((nrows, RET_HEADS * RET_DV), bf16),
                   jax.ShapeDtypeStruct((nseq, RET_HEADS, RET_DK, RET_DV), f32)),
        grid=(RET_HEADS // G, nsteps),
        in_specs=[pl.BlockSpec(memory_space=pltpu.SMEM),
                  pl.BlockSpec((L, kw), lambda h, c: (b0 + c, h)),
                  pl.BlockSpec((L, kw), lambda h, c: (b0 + c, kb + h)),
                  pl.BlockSpec((L, vw), lambda h, c: (b0 + c, vb + h)),
                  pl.BlockSpec((L, vw), lambda h, c: (b0 + c, gb + h)),
                  pl.BlockSpec((1, G, RET_DK, RET_DV), state_map),
                  pl.BlockSpec((1, vw), lambda h, c: (0, h))],
        out_specs=(pl.BlockSpec((L, vw), lambda h, c: (c, h)),
                   pl.BlockSpec((1, G, RET_DK, RET_DV), state_map)),
        scratch_shapes=[pltpu.VMEM((G, RET_DK, RET_DV), f32)],
        compiler_params=_params(("parallel", "arbitrary")),
        name=name,
    )(lg, proj, proj, proj, proj, state, gn_w)


SUBLANES = 8


def _lerpmm_kernel(x_ref, xprev_ref, g_ref, start_ref, mu_ref, w_ref, o_ref, hlast_ref,
                   h_ref, xx_ref, l_ref, *, tiles_per_d, n_big, npc):
    i = pl.program_id(0)
    j = pl.program_id(1)
    tm = x_ref.shape[0]
    cpb = tm // CHUNK

    @pl.when(j == 0)
    def _():
        g = g_ref[...]
        h = _rms(x_ref[...], g)
        h_ref[...] = h
        xx_ref[...] = pltpu.roll(h, 1, 0) - h
        prev = _rms(xprev_ref[...], g)[SUBLANES - 1:SUBLANES, :]
        xx_ref[0:1, :] = prev - h[0:1, :]
        for ci in range(cpb):
            gc = i * cpb + ci
            r0 = ci * CHUNK

            @pl.when(jnp.logical_or(gc == 0, gc >= npc))
            def _(ci=ci, r0=r0):
                xx_ref[r0:r0 + 1, :] = start_ref[ci] - h_ref[r0:r0 + 1, :]

            hlast_ref[ci] = h[r0 + CHUNK - 1:r0 + CHUNK, :]

    @pl.when(jnp.logical_or(j % tiles_per_d == 0, j >= n_big))
    def _():
        l_ref[...] = (h_ref[...] + xx_ref[...] * mu_ref[0]).astype(bf16)

    o_ref[...] = jnp.dot(l_ref[...], w_ref[...], preferred_element_type=f32).astype(o_ref.dtype)


def _lerp_matmul(x, g, starts, mu_tiles, wcat, tn, tiles_per_d, n_big, npc):
    T, D = x.shape
    N = wcat.shape[1]
    tm = _pick(T, (1024, 512, 256, 128, 64))
    cpb = tm // CHUNK
    return pl.pallas_call(
        functools.partial(_lerpmm_kernel, tiles_per_d=tiles_per_d, n_big=n_big, npc=npc),
        out_shape=(jax.ShapeDtypeStruct((T, N), bf16),
                   jax.ShapeDtypeStruct((T // CHUNK, 1, D), f32)),
        grid=(T // tm, N // tn),
        in_specs=[pl.BlockSpec((tm, D), lambda i, j: (i, 0)),
                  pl.BlockSpec((SUBLANES, D), lambda i, j: (jnp.maximum(i * (tm // SUBLANES) - 1, 0), 0)),
                  pl.BlockSpec((1, D), lambda i, j: (0, 0)),
                  pl.BlockSpec((cpb, 1, D), lambda i, j: (i, 0, 0)),
                  pl.BlockSpec((1, 1, D), lambda i, j: (j, 0, 0)),
                  pl.BlockSpec((D, tn), lambda i, j: (0, j))],
        out_specs=(pl.BlockSpec((tm, tn), lambda i, j: (i, j)),
                   pl.BlockSpec((cpb, 1, D), lambda i, j: (i, 0, 0))),
        scratch_shapes=[pltpu.VMEM((tm, D), f32), pltpu.VMEM((tm, D), f32), pltpu.VMEM((tm, D), bf16)],
        compiler_params=_params(("parallel", "arbitrary")),
        name="rwkv_proj",
    )(x, x, g, starts, mu_tiles, wcat)


DECAY_SCALE = 0.6065306597126334
LANES = 128


SEG_TILE = 256
PREP_STRIP = 512


def _head_sum(x):
    rows, D = x.shape
    r = lax.broadcasted_iota(jnp.int32, (SEG_TILE, SEG_TILE), 0) // RWKV_HS
    c = lax.broadcasted_iota(jnp.int32, (SEG_TILE, SEG_TILE), 1) // RWKV_HS
    ones = (r == c).astype(bf16)
    out = [jnp.dot(x[:, j * SEG_TILE:(j + 1) * SEG_TILE].astype(bf16), ones, preferred_element_type=f32)
           for j in range(D // SEG_TILE)]
    return jnp.concatenate(out, axis=1)


def _chunk_cumsum(x):
    rows = x.shape[0]
    row = lax.broadcasted_iota(jnp.int32, (rows, rows), 0)
    col = lax.broadcasted_iota(jnp.int32, (rows, rows), 1)
    tri = jnp.logical_and((row // CHUNK) == (col // CHUNK), row >= col).astype(bf16)
    hi = x.astype(bf16)
    rest = x - hi.astype(f32)
    mid = rest.astype(bf16)
    lo = (rest - mid.astype(f32)).astype(bf16)
    dot = functools.partial(jnp.dot, preferred_element_type=f32)
    return dot(tri, hi) + dot(tri, mid) + dot(tri, lo)


def _rwkv_prep_kernel(p_ref, r_ref, k_ref, v_ref, w2_ref, a2_ref, g2_ref, w0_ref, a0_ref,
                      kk_ref, ka_ref, rk_ref, lnw_ref, lnb_ref,
                      at_ref, rt_ref, bt_ref, kt_ref, bh_ref, kh_ref, gl_ref, ga_ref, gb_ref):
    P = LORA_PAD
    tm = r_ref.shape[0]
    rw, ra, rg = w2_ref.shape[0], a2_ref.shape[0], g2_ref.shape[0]
    pw = jnp.tanh(p_ref[:, :rw].astype(f32)).astype(bf16)
    pa = p_ref[:, P:P + ra]
    pg = jax.nn.sigmoid(p_ref[:, 2 * P:2 * P + rg].astype(f32)).astype(bf16)
    dot = functools.partial(jnp.dot, preferred_element_type=f32)

    for c0 in range(0, r_ref.shape[1], PREP_STRIP):
        cs = slice(c0, c0 + PREP_STRIP)
        wl = w0_ref[:, cs] + dot(pw, w2_ref[:, cs])
        lw = -DECAY_SCALE * jax.nn.sigmoid(wl)
        a = jax.nn.sigmoid(a0_ref[:, cs] + dot(pa, a2_ref[:, cs]))
        g = dot(pg, g2_ref[:, cs])

        cum = _chunk_cumsum(lw)
        chunk_decay = [jnp.exp(cum[(ci + 1) * CHUNK - 1:(ci + 1) * CHUNK, :]) for ci in range(tm // CHUNK)]

        r, k, v = r_ref[:, cs].astype(f32), k_ref[:, cs].astype(f32), v_ref[:, cs].astype(f32)
        kk = k * kk_ref[:, cs]
        kk = kk * lax.rsqrt(jnp.maximum(_head_sum(kk * kk), 1e-24))
        kmod = k * (1.0 + (a - 1.0) * ka_ref[:, cs])
        beta = kk * a
        e_in = jnp.exp(cum)
        e_neg = 1.0 / e_in
        e_end = jnp.concatenate([chunk_decay[ci] * e_neg[ci * CHUNK:(ci + 1) * CHUNK]
                                 for ci in range(tm // CHUNK)], axis=0)
        at_ref[:, cs] = (-kk * jnp.exp(cum - lw)).astype(bf16)
        rt_ref[:, cs] = (r * e_in).astype(bf16)
        bt_ref[:, cs] = (beta * e_neg).astype(bf16)
        kt_ref[:, cs] = (kmod * e_neg).astype(bf16)
        bh_ref[:, cs] = (beta * e_end).astype(bf16)
        kh_ref[:, cs] = (kmod * e_end).astype(bf16)
        bonus = _head_sum(r * kmod * rk_ref[:, cs]) * v
        ga_ref[:, cs] = (lnw_ref[:, cs] * g).astype(bf16)
        gb_ref[:, cs] = ((lnb_ref[:, cs] + bonus) * g).astype(bf16)
        for ci in range(tm // CHUNK):
            gl_ref[ci, :, cs] = chunk_decay[ci]


def _rwkv_prep(proj, lora_block, w2p, a2p, g2p, w0, a0, k_k, k_a, r_k, ln_w, ln_b):
    T = proj.shape[0]
    D = w2p.shape[1]
    P = LORA_PAD
    tm = _pick(T, (128, 64))
    vspec = pl.BlockSpec((1, D), lambda i: (0, 0))
    ospec = pl.BlockSpec((tm, D), lambda i: (i, 0))
    o16 = jax.ShapeDtypeStruct((T, D), bf16)
    return pl.pallas_call(
        _rwkv_prep_kernel,
        out_shape=(o16,) * 6 + (jax.ShapeDtypeStruct((T // CHUNK, 1, D), f32), o16, o16),
        grid=(T // tm,),
        in_specs=[pl.BlockSpec((tm, 3 * P), lambda i: (i, lora_block)),
                  pl.BlockSpec((tm, D), lambda i: (i, 0)),
                  pl.BlockSpec((tm, D), lambda i: (i, 1)),
                  pl.BlockSpec((tm, D), lambda i: (i, 2)),
                  _resident(w2p.shape), _resident(a2p.shape), _resident(g2p.shape)] + [vspec] * 7,
        out_specs=(ospec,) * 6 + (pl.BlockSpec((tm // CHUNK, 1, D), lambda i: (i, 0, 0)), ospec, ospec),
        compiler_params=_params(("parallel",)),
        name="rwkv_prep",
    )(proj, proj, proj, proj, w2p, a2p, g2p, w0, a0, k_k, k_a, r_k, ln_w, ln_b)


def _rwkv_heads_chunk(heads):
    L, N = heads[0][6].shape
    assert L == N
    nh = len(heads)
    rng = range(nh)
    dot = functools.partial(jnp.dot, preferred_element_type=f32)
    row = lax.broadcasted_iota(jnp.int32, (L, 2 * L), 0)
    lane = lax.broadcasted_iota(jnp.int32, (L, 2 * L), 1)
    lo = lane < L
    tok = jnp.where(lo, lane, lane - L)
    incl = row >= tok
    strict = row > tok
    lhs = [jnp.concatenate([h[0], h[1]], axis=0) for h in heads]
    rhs = [jnp.concatenate([h[2], h[3]], axis=0) for h in heads]
    G = [lax.dot_general(lhs[i], rhs[i], NT_DIMS, preferred_element_type=f32) for i in rng]
    LS = [lax.dot_general(lhs[i], heads[i][7].astype(bf16), NT_DIMS, preferred_element_type=f32)
          for i in rng]
    top = [jnp.where(strict, g[:L], 0.0) for g in G]
    a_r = [jnp.where(incl, g[L:], 0.0).astype(bf16) for g in G]
    v2 = [jnp.concatenate([h[6], h[6]], axis=0) for h in heads]
    zero = jnp.zeros((L, N), f32)
    x0 = [LS[i][:L] + dot(jnp.where(lo, 0.0, top[i]).astype(bf16), v2[i]) for i in rng]
    p = [jnp.where(lo, top[i], jnp.concatenate([zero, x0[i]], axis=1)) for i in rng]
    steps = max(1, (L - 1).bit_length())
    for s in range(steps):
        z = [dot(p[i][:, :L].astype(bf16), p[i].astype(bf16)) for i in rng]
        if s < steps - 1:
            p = [z[i] + jnp.where(lo, 0.0, p[i]) for i in rng]
        else:
            u = [(p[i] + z[i])[:, L:] for i in rng]
    uv = [jnp.concatenate([u[i], heads[i][6].astype(f32)], axis=0) for i in rng]
    o = [LS[i][L:] + dot(a_r[i], uv[i].astype(bf16)) for i in rng]
    S_new = [heads[i][7] * heads[i][8]
             + dot(uv[i].T.astype(bf16), jnp.concatenate([heads[i][4], heads[i][5]], axis=0))
             for i in rng]
    return list(zip(o, S_new))


def _rwkv_core_kernel(at_ref, rt_ref, bt_ref, kt_ref, bh_ref, kh_ref, v_ref, gl_ref, s0_ref,
                      ga_ref, gb_ref, y_ref, sp_ref, ss_ref, S_ref, *, npc, nhead):
    c = pl.program_id(1)
    N = RWKV_HS

    @pl.when(c == 0)
    def _():
        S_ref[...] = jnp.zeros_like(S_ref)

    @pl.when(c >= npc)
    def _():
        S_ref[...] = s0_ref[0]

    gl = gl_ref[0]
    heads = []
    for hh in range(nhead):
        sl = slice(hh * N, (hh + 1) * N)
        heads.append((at_ref[:, sl], rt_ref[:, sl], bt_ref[:, sl], kt_ref[:, sl], bh_ref[:, sl],
                      kh_ref[:, sl], v_ref[:, sl], S_ref[hh], gl[:, sl]))
    res = _rwkv_heads_chunk(heads)
    for hh in range(nhead):
        S_ref[hh] = res[hh][1]
    outs = [o for o, _ in res]
    cen = [o - jnp.mean(o, axis=-1, keepdims=True) for o in outs]
    nrm = [d * lax.rsqrt(jnp.mean(d * d, axis=-1, keepdims=True) + RWKV_GN_EPS) for d in cen]
    y_ref[...] = (jnp.concatenate(nrm, axis=1) * ga_ref[...].astype(f32)
                  + gb_ref[...].astype(f32)).astype(y_ref.dtype)

    @pl.when(c == npc - 1)
    def _():
        sp_ref[0] = S_ref[...]

    @pl.when(c >= npc)
    def _():
        ss_ref[0] = S_ref[...]


RWKV_HEADS_PER_STEP = 32


def _rwkv_core(ops, proj, v_block, gl, state, ga, gb, npc, nchunks):
    T, D = ga.shape
    N = RWKV_HS
    nh = D // N
    nb = state.shape[0]
    G = RWKV_HEADS_PER_STEP
    W = G * N
    tok = pl.BlockSpec((CHUNK, W), lambda p, c: (c, p))
    vtok = pl.BlockSpec((CHUNK, W), lambda p, c: (c, v_block * (D // W) + p))

    sample_state = pl.BlockSpec((1, G, N, N), lambda p, c: (jnp.maximum(c - npc, 0), p, 0, 0))

    return pl.pallas_call(
        functools.partial(_rwkv_core_kernel, npc=npc, nhead=G),
        out_shape=(jax.ShapeDtypeStruct((T, D), bf16),
                   jax.ShapeDtypeStruct((1, nh, N, N), f32),
                   jax.ShapeDtypeStruct((nb, nh, N, N), f32)),
        grid=(nh // G, nchunks),
        in_specs=[tok] * 6 + [vtok, pl.BlockSpec((1, 1, W), lambda p, c: (c, 0, p)),
                              sample_state, tok, tok],
        out_specs=(tok, pl.BlockSpec((1, G, N, N), lambda p, c: (0, p, 0, 0)), sample_state),
        scratch_shapes=[pltpu.VMEM((G, N, N), f32)],
        compiler_params=_params(("parallel", "arbitrary")),
        name="rwkv_core",
    )(*ops, proj, gl, state, ga, gb)


def _positions(n_prompt, n_b, n_s):
    return jnp.concatenate([jnp.arange(n_prompt), jnp.tile(PAST_LEN + jnp.arange(n_s), n_b)]).astype(f32)


def _attn_tables(pos):
    half = ROT_DIM // 2
    inv = ROPE_THETA ** (-jnp.arange(half, dtype=f32) / half)
    ang = pos[:, None] * inv[None, :]
    cos, sin = jnp.cos(ang), jnp.sin(ang)
    T = pos.shape[0]
    zeros = jnp.zeros((T, half), f32)
    rest = HEAD_DIM - ROT_DIM
    c = jnp.concatenate([cos, cos, jnp.ones((T, rest), f32)], axis=1)
    s1 = jnp.concatenate([zeros, sin, jnp.zeros((T, rest), f32)], axis=1)
    s2 = jnp.concatenate([-sin, zeros, jnp.zeros((T, rest), f32)], axis=1)
    return tuple(jnp.tile(t, (1, 2)) for t in (c, s1, s2))


def _ret_tables(pos):
    half = RET_DK // 2
    inv = RET_THETA ** (-jnp.arange(half, dtype=f32) / half)
    ang = pos[:, None] * inv[None, :]
    return jnp.cos(ang), jnp.sin(ang)


def _pad_cols(w, n):
    return jnp.pad(w, ((0, 0), (0, n - w.shape[1])))


def _pad_rows_to_lanes(w):
    n = -(-w.shape[0] // LANES) * LANES
    return jnp.pad(w, ((0, n - w.shape[0]), (0, 0)))


def kernel(x_prompt, x_sample, cache_attn_k, cache_attn_v, state_ret, state_rwkv, state_rwkv_shift,
           norm_mix, norm_mlp, norm_final,
           attn_w_qkv, attn_sinks, attn_w_o,
           ret_w_in, ret_gn_w, ret_w_o,
           rwkv_mu, rwkv_w_rkv, rwkv_w_o, rwkv_w0, rwkv_w1, rwkv_w2, rwkv_a0, rwkv_a1, rwkv_a2,
           rwkv_g1, rwkv_g2, rwkv_k_k, rwkv_k_a, rwkv_r_k, rwkv_ln_w, rwkv_ln_b,
           mlp_w_up, mlp_w_down):
    bp, tp, D = x_prompt.shape
    nb, ns, _ = x_sample.shape
    assert bp == 1 and ns == CHUNK and tp % CHUNK == 0
    depth = norm_mix.shape[0]
    npc = tp // CHUNK
    nchunks = npc + nb
    T = tp + nb * ns
    qd = ATTN_HEADS * HEAD_DIM
    kvd = ATTN_KV_HEADS * HEAD_DIM

    x = (x_prompt.reshape(tp, D), x_sample.reshape(nb * ns, D))
    pos = _positions(tp, nb, ns)
    attn_tabs = _attn_tables(pos)
    ret_tabs = _ret_tables(pos)
    lg = jnp.log1p(-jnp.exp2(-5.0 - jnp.arange(RET_HEADS, dtype=f32)))
    tm_proj = _pick(T, (512, 256, 128, 64))
    tm_ret = _pick(T, (1024, 512, 256, 128, 64))
    w_up16 = mlp_w_up.astype(bf16)
    w_down16 = mlp_w_down.astype(bf16)

    kp_l, vp_l, ks_l, vs_l, rp_l, rs_l, wp_l, ws_l, shp_l, shs_l = ([] for _ in range(10))
    for i in range(depth):
        j, kind = divmod(i, 3)
        g_mix = norm_mix[i][None, :]
        if kind == 0:
            nsub = (qd + 2 * kvd) // (2 * EPI_COLS)
            routes = [[("q", ((0, s * EPI_COLS),)) for s in range(nsub)],
                      [("q", ((0, s * EPI_COLS),)) for s in range(nsub, qd // EPI_COLS)]
                      + [("k", ((1, 0), (2, 0))), ("v", ((1, kvd), (2, 2 * kvd)))]]
            q, kv, kvdup = _norm_matmul(
                x, g_mix, attn_w_qkv[j].astype(bf16), attn_tabs, routes, _epi_attn,
                [jax.ShapeDtypeStruct((T, qd), bf16), jax.ShapeDtypeStruct((T, 2 * kvd), f32),
                 jax.ShapeDtypeStruct((T, 4 * kvd), bf16)],
                [pl.BlockSpec((tm_proj, qd), lambda i, jj: (i, 0)),
                 pl.BlockSpec((tm_proj, 2 * kvd), lambda i, jj: (i, 0)),
                 pl.BlockSpec((tm_proj, 4 * kvd), lambda i, jj: (i, 0))],
                tm_proj, "attn_qkv")
            k_new = kv[:, :kvd]
            v_new = kv[:, kvd:]

            def cache_dup(cache):
                c5 = cache.astype(bf16).reshape(nb * WINDOW, ATTN_KV_HEADS, 1, HEAD_DIM)
                return jnp.broadcast_to(c5, (nb * WINDOW, ATTN_KV_HEADS, 2, HEAD_DIM)).reshape(nb * WINDOW, 2 * kvd)

            def new_window(new, cache):
                return jnp.concatenate([cache.reshape(nb, WINDOW, kvd), new[tp:].reshape(nb, ns, kvd)], axis=1)

            o = _attention(q, kvdup, cache_dup(cache_attn_k[j]), cache_dup(cache_attn_v[j]),
                           attn_sinks[j], npc, nchunks)
            k_samp = new_window(k_new, cache_attn_k[j])
            v_samp = new_window(v_new, cache_attn_v[j])
            x = _matmul_residual(o, attn_w_o[j].astype(bf16), x, "attn_out")
            kp_l.append(k_new[tp - WINDOW:tp].reshape(1, WINDOW, ATTN_KV_HEADS, HEAD_DIM))
            vp_l.append(v_new[tp - WINDOW:tp].reshape(1, WINDOW, ATTN_KV_HEADS, HEAD_DIM))
            ks_l.append(k_samp[:, -WINDOW:].reshape(nb, WINDOW, ATTN_KV_HEADS, HEAD_DIM))
            vs_l.append(v_samp[:, -WINDOW:].reshape(nb, WINDOW, ATTN_KV_HEADS, HEAD_DIM))
        elif kind == 1:
            nq = RET_HEADS * RET_DK // RET_TILE
            nv = RET_HEADS * RET_DV // RET_TILE
            tile = lambda mode: [(mode, ((0, s * EPI_COLS),)) for s in range(RET_TILE // EPI_COLS)]
            routes = [tile("q")] * nq + [tile("k")] * nq + [tile("plain")] * (2 * nv)
            (proj,) = _norm_matmul(
                x, g_mix, ret_w_in[j].astype(bf16), ret_tabs, routes, _epi_ret,
                [jax.ShapeDtypeStruct((T, ret_w_in.shape[2]), bf16)],
                [pl.BlockSpec((tm_ret, RET_TILE), lambda i, jj: (i, jj))], tm_ret, "ret_proj")
            gnw = ret_gn_w[j][None, :]
            lp = _pick(tp, (256, 128, 64))
            y_p, s_p = _retention(proj, state_ret[j], gnw, lg, 0, tp, lp, False, "ret_prompt")
            y_s, s_s = _retention(proj, state_ret[j], gnw, lg, tp, nb * ns, ns, True, "ret_sample")
            x = _matmul_residual2(y_p, y_s, ret_w_o[j].astype(bf16), x, "ret_out")
            rp_l.append(s_p)
            rs_l.append(s_s)
        else:
            starts = jnp.concatenate([jnp.zeros((npc, D), f32), state_rwkv_shift[j]], axis=0)[:, None, :]
            P = LORA_PAD
            tn = LORA_PAD
            wcat = jnp.concatenate(
                [rwkv_w_rkv[j][0], rwkv_w_rkv[j][1], rwkv_w_rkv[j][2],
                 _pad_cols(rwkv_w1[j], P), _pad_cols(rwkv_a1[j], P), _pad_cols(rwkv_g1[j], P)],
                axis=1).astype(bf16)
            tiles_per_d = D // tn
            mu = rwkv_mu[j]
            mu_tiles = jnp.concatenate(
                [jnp.repeat(mu[jnp.array([0, 2, 3])], tiles_per_d, axis=0), mu[jnp.array([1, 4, 5])]],
                axis=0)[:, None, :]
            proj, h_last = _lerp_matmul(x, g_mix, starts, mu_tiles, wcat, tn, tiles_per_d,
                                        3 * tiles_per_d, npc)
            prep = _rwkv_prep(
                proj, 3 * D // (3 * P),
                _pad_rows_to_lanes(rwkv_w2[j]).astype(bf16), _pad_rows_to_lanes(rwkv_a2[j]).astype(bf16),
                _pad_rows_to_lanes(rwkv_g2[j]).astype(bf16), rwkv_w0[j][None, :], rwkv_a0[j][None, :],
                rwkv_k_k[j][None, :], rwkv_k_a[j][None, :], rwkv_r_k[j].reshape(1, D),
                rwkv_ln_w[j][None, :], rwkv_ln_b[j][None, :])
            ops, gl, ga, gb = prep[:6], prep[6], prep[7], prep[8]
            y, s_p, s_s = _rwkv_core(ops, proj, 2, gl, state_rwkv[j], ga, gb, npc, nchunks)
            x = _matmul_residual(y, rwkv_w_o[j].astype(bf16), x, "rwkv_out")
            wp_l.append(s_p)
            ws_l.append(s_s)
            shp_l.append(h_last[npc - 1])
            shs_l.append(h_last[npc:, 0])
        x = _mlp(x, norm_mlp[i][None, :], w_up16, w_down16, i, norm_final[None, :],
                 tp if i == depth - 1 else None, "mlp")

    y_prompt = x[0].reshape(1, tp, D)
    y_sample = x[1].reshape(nb, ns, D)
    return (y_prompt, y_sample,
            jnp.stack(kp_l), jnp.stack(vp_l), jnp.stack(ks_l), jnp.stack(vs_l),
            jnp.stack(rp_l), jnp.stack(rs_l),
            jnp.stack(wp_l), jnp.stack(ws_l), jnp.stack(shp_l), jnp.stack(shs_l))
```
